```python
import jax
import jax.numpy as jnp
from jax import lax
import numpy as np

D_MODEL = 1024
BATCH = 4
SEQ = 4096
DEPTH = 1

CHUNK = 64
NORM_EPS = 1e-6
ROPE_THETA = 500000.0

HEAD_DIM = 64
ATT_HEADS = (D_MODEL // 2) // HEAD_DIM
ATT_KV_HEADS = ATT_HEADS // 4
Q_PER_KV = ATT_HEADS // ATT_KV_HEADS
ATT_ROT_DIM = HEAD_DIM // 4
IDX_HEADS = 8
IDX_DIM = 32
IDX_ROT_DIM = IDX_DIM // 4
TOPK_MAX = 256
Q_BLOCK = 128
ATT_WIDTH = ATT_HEADS * HEAD_DIM

MLSTM_HEAD_DIM = 128
MLSTM_HEADS = (D_MODEL // 2) // MLSTM_HEAD_DIM
MLSTM_WIDTH = MLSTM_HEADS * MLSTM_HEAD_DIM
CONV_WIDTH = 4

MIX_WIDTH = ATT_WIDTH + MLSTM_WIDTH

PROJ_SIZES = (
    ATT_WIDTH,
    ATT_KV_HEADS * HEAD_DIM,
    ATT_KV_HEADS * HEAD_DIM,
    IDX_HEADS * IDX_DIM,
    IDX_DIM,
    IDX_HEADS,
    MLSTM_WIDTH,
    MLSTM_WIDTH,
    MLSTM_WIDTH,
    MLSTM_HEADS,
    MLSTM_HEADS,
    MLSTM_WIDTH,
)
PROJ_WIDTH = sum(PROJ_SIZES)

N_EXPERTS = 32
TOP_K = 4
EXPERT_DIM = D_MODEL
SWIGLU_ALPHA = 1.702
SWIGLU_LIMIT = 7.0
EXPERT_BLOCK = 128

kernel_name = 'hybrid_dsa_mlstm_moe_block'


def rms_norm(x, w):
    xf = x.astype(jnp.float32)
    y = xf * lax.rsqrt(jnp.mean(xf * xf, axis=-1, keepdims=True) + NORM_EPS)
    return (y * w.astype(jnp.float32)).astype(x.dtype)


def partial_rope(x, rot_dim):
    seq = x.shape[1]
    pos = jnp.arange(seq, dtype=jnp.float32)
    inv_freq = ROPE_THETA ** (-jnp.arange(0, rot_dim, 2, dtype=jnp.float32) / rot_dim)
    ang = pos[:, None] * inv_freq[None, :]
    cos = jnp.cos(ang)[None, :, None, :]
    sin = jnp.sin(ang)[None, :, None, :]
    xr = x[..., :rot_dim].astype(jnp.float32)
    x1, x2 = jnp.split(xr, 2, axis=-1)
    rot = jnp.concatenate([x1 * cos - x2 * sin, x2 * cos + x1 * sin], axis=-1)
    return jnp.concatenate([rot.astype(x.dtype), x[..., rot_dim:]], axis=-1)


def dsa_attention(q, k, v, qi, ki, wi):
    bsz, seq = q.shape[0], q.shape[1]
    k_sel = min(TOPK_MAX, seq // 4)
    n_blk = seq // Q_BLOCK
    key_chunk = jnp.arange(seq) // CHUNK
    qg = q.reshape(bsz, seq, ATT_KV_HEADS, Q_PER_KV, HEAD_DIM)
    scale = HEAD_DIM ** -0.5

    def block(i):
        s0 = i * Q_BLOCK
        qb = lax.dynamic_slice_in_dim(qg, s0, Q_BLOCK, axis=1)
        qib = lax.dynamic_slice_in_dim(qi, s0, Q_BLOCK, axis=1)
        wib = lax.dynamic_slice_in_dim(wi, s0, Q_BLOCK, axis=1)
        q_chunk = (s0 + jnp.arange(Q_BLOCK)) // CHUNK
        admissible = key_chunk[None, :] <= q_chunk[:, None]
        idx_score = jnp.einsum('bqh,bqhs->bqs', wib,
                               jax.nn.relu(jnp.einsum('bqhd,bsd->bqhs', qib, ki)))
        idx_score = jnp.where(admissible[None], idx_score.astype(jnp.float32), -jnp.inf)
        _, sel = lax.top_k(idx_score, k_sel)
        valid = key_chunk[sel] <= q_chunk[None, :, None]
        ks = jax.vmap(lambda kb, ib: kb[ib])(k, sel)
        vs = jax.vmap(lambda vb, ib: vb[ib])(v, sel)
        logits = jnp.einsum('bqgrd,bqkgd->bqgrk', qb, ks).astype(jnp.float32) * scale
        logits = jnp.where(valid[:, :, None, None, :], logits, -jnp.inf)
        p = jax.nn.softmax(logits, axis=-1).astype(v.dtype)
        return jnp.einsum('bqgrk,bqkgd->bqgrd', p, vs)

    out = lax.map(block, jnp.arange(n_blk))
    return out.transpose(1, 0, 2, 3, 4, 5).reshape(bsz, seq, ATT_WIDTH)


def causal_depthwise_conv(x, w, b):
    c = x.shape[-1]
    y = lax.conv_general_dilated(x, w.reshape(CONV_WIDTH, 1, c), window_strides=(1,),
                                 padding=[(CONV_WIDTH - 1, 0)],
                                 dimension_numbers=('NWC', 'WIO', 'NWC'),
                                 feature_group_count=c)
    return y + b


def mlstm_chunkwise(q, k, v, i_pre, log_f):
    bsz, nh, seq, dh = q.shape
    nc = seq // CHUNK

    def to_chunks(t):
        t = t.reshape(t.shape[:2] + (nc, CHUNK) + t.shape[3:])
        return jnp.moveaxis(t, 2, 0)

    qc, kc, vc, ic, fc = (to_chunks(t) for t in (q, k * dh ** -0.5, v, i_pre, log_f))
    causal = jnp.tril(jnp.ones((CHUNK, CHUNK), dtype=bool))

    def step(carry, inp):
        c_mat, n_vec, m_prev = carry
        qb, kb, vb, ib, fb = inp
        b = jnp.cumsum(fb, axis=-1)
        d_log = jnp.where(causal, b[..., :, None] - b[..., None, :] + ib[..., None, :], -jnp.inf)
        inter_log = b + m_prev[..., None]
        m_t = jnp.maximum(inter_log, jnp.max(d_log, axis=-1))
        d_w = jnp.exp(d_log - m_t[..., None])
        inter_w = jnp.exp(inter_log - m_t)
        s = jnp.einsum('bhtd,bhsd->bhts', qb, kb) * d_w
        num = (jnp.einsum('bhts,bhse->bhte', s, vb)
               + inter_w[..., None] * jnp.einsum('bhtd,bhde->bhte', qb, c_mat))
        den = jnp.sum(s, axis=-1) + inter_w * jnp.einsum('bhtd,bhd->bht', qb, n_vec)
        h = num / jnp.maximum(jnp.abs(den), jnp.exp(-m_t))[..., None]
        b_last = b[..., -1]
        w_log = b_last[..., None] - b + ib
        m_new = jnp.maximum(b_last + m_prev, jnp.max(w_log, axis=-1))
        decay = jnp.exp(b_last + m_prev - m_new)
        w_s = jnp.exp(w_log - m_new[..., None])
        c_new = decay[..., None, None] * c_mat + jnp.einsum('bhsd,bhse->bhde', kb * w_s[..., None], vb)
        n_new = decay[..., None] * n_vec + jnp.einsum('bhs,bhsd->bhd', w_s, kb)
        return (c_new, n_new, m_new), h

    init = (jnp.zeros((bsz, nh, dh, dh), jnp.float32),
            jnp.zeros((bsz, nh, dh), jnp.float32),
            jnp.zeros((bsz, nh), jnp.float32))
    _, hs = lax.scan(step, init, (qc, kc, vc, ic, fc))
    return jnp.moveaxis(hs, 0, 2).reshape(bsz, nh, seq, dh)


def mlstm_mixer(mq, mk, mv, mi, mf, mo, conv_w, conv_b, igate_b, fgate_b, norm_w):
    bsz, seq = mq.shape[0], mq.shape[1]
    qk = jax.nn.silu(causal_depthwise_conv(jnp.concatenate([mq, mk], axis=-1), conv_w, conv_b))
    q, k = jnp.split(qk, 2, axis=-1)

    def heads(t):
        return t.reshape(bsz, seq, MLSTM_HEADS, MLSTM_HEAD_DIM).transpose(0, 2, 1, 3).astype(jnp.float32)

    i_pre = (mi + igate_b).astype(jnp.float32).transpose(0, 2, 1)
    log_f = jax.nn.log_sigmoid((mf + fgate_b).astype(jnp.float32)).transpose(0, 2, 1)
    h = mlstm_chunkwise(heads(q), heads(k), heads(mv), i_pre, log_f)
    h = h * lax.rsqrt(jnp.mean(h * h, axis=-1, keepdims=True) + NORM_EPS)
    h = h.transpose(0, 2, 1, 3).reshape(bsz, seq, MLSTM_WIDTH) * norm_w.astype(jnp.float32)
    return (h * jax.nn.sigmoid(mo.astype(jnp.float32))).astype(mq.dtype)


def moe_ffn(h, router_w, router_b, w_gu, b_gu, w_dn, b_dn):
    bsz, seq, d = h.shape
    t = bsz * seq
    xf = h.reshape(t, d)
    logits = (xf @ router_w + router_b).astype(jnp.float32)
    top_vals, top_idx = lax.top_k(logits, TOP_K)
    gates = jax.nn.softmax(top_vals, axis=-1).astype(h.dtype)
    a = t * TOP_K
    e_flat = top_idx.reshape(a)
    tok_flat = jnp.repeat(jnp.arange(t), TOP_K)
    g_flat = gates.reshape(a)
    order = jnp.argsort(e_flat)
    e_sorted, tok_sorted, g_sorted = e_flat[order], tok_flat[order], g_flat[order]
    counts = jnp.bincount(e_flat, length=N_EXPERTS)
    start = jnp.cumsum(counts) - counts
    padded = (counts + EXPERT_BLOCK - 1) // EXPERT_BLOCK * EXPERT_BLOCK
    pad_end = jnp.cumsum(padded)
    pad_start = pad_end - padded
    dest = pad_start[e_sorted] + (jnp.arange(a) - start[e_sorted])
    n_rows = -(-a // EXPERT_BLOCK) * EXPERT_BLOCK + N_EXPERTS * EXPERT_BLOCK
    n_blocks = n_rows // EXPERT_BLOCK
    x_buf = jnp.zeros((n_rows, d), h.dtype).at[dest].set(xf[tok_sorted])
    blk_expert = jnp.minimum(
        jnp.searchsorted(pad_end, jnp.arange(n_blocks) * EXPERT_BLOCK, side='right'), N_EXPERTS - 1)

    def expert_block(args):
        xb, e = args
        gu = xb @ w_gu[e] + b_gu[e]
        glu = jnp.minimum(gu[:, :EXPERT_DIM], SWIGLU_LIMIT)
        lin = jnp.clip(gu[:, EXPERT_DIM:], -SWIGLU_LIMIT, SWIGLU_LIMIT)
        act = glu * jax.nn.sigmoid(SWIGLU_ALPHA * glu) * (lin + 1.0)
        return act @ w_dn[e] + b_dn[e]

    y_buf = lax.map(expert_block, (x_buf.reshape(n_blocks, EXPERT_BLOCK, d), blk_expert))
    y = y_buf.reshape(n_rows, d)[dest] * g_sorted[:, None]
    return jax.ops.segment_sum(y, tok_sorted, num_segments=t).reshape(bsz, seq, d)


def setup_inputs(seed: int = 0) -> dict:
    key = jax.random.key(seed)
    ks = jax.random.split(key, 20)

    def nrm(k, shape, scale):
        return jax.random.normal(k, shape, jnp.float32) * scale

    return {
        'x': nrm(ks[0], (BATCH, SEQ, D_MODEL), 1.0),
        'mix_norm_w': 1.0 + nrm(ks[1], (DEPTH, D_MODEL), 0.02),
        'w_in': nrm(ks[2], (DEPTH, D_MODEL, PROJ_WIDTH), D_MODEL ** -0.5),
        'conv_w': nrm(ks[3], (DEPTH, CONV_WIDTH, 2 * MLSTM_WIDTH), CONV_WIDTH ** -0.5),
        'conv_b': nrm(ks[4], (DEPTH, 2 * MLSTM_WIDTH), 0.02),
        'igate_b': nrm(ks[5], (DEPTH, MLSTM_HEADS), 0.1),
        'fgate_b': 3.0 + nrm(ks[6], (DEPTH, MLSTM_HEADS), 0.5),
        'mlstm_norm_w': 1.0 + nrm(ks[7], (DEPTH, MLSTM_WIDTH), 0.02),
        'w_out': nrm(ks[8], (DEPTH, MIX_WIDTH, D_MODEL), MIX_WIDTH ** -0.5),
        'ffn_norm_w': 1.0 + nrm(ks[9], (DEPTH, D_MODEL), 0.02),
        'router_w': nrm(ks[10], (DEPTH, D_MODEL, N_EXPERTS), D_MODEL ** -0.5),
        'router_b': nrm(ks[11], (DEPTH, N_EXPERTS), 0.01),
        'w_gate_up': nrm(ks[12], (DEPTH, N_EXPERTS, D_MODEL, 2 * EXPERT_DIM), D_MODEL ** -0.5),
        'b_gate_up': nrm(ks[13], (DEPTH, N_EXPERTS, 2 * EXPERT_DIM), 0.02),
        'w_down': nrm(ks[14], (DEPTH, N_EXPERTS, EXPERT_DIM, D_MODEL), EXPERT_DIM ** -0.5),
        'b_down': nrm(ks[15], (DEPTH, N_EXPERTS, D_MODEL), 0.02),
        'final_norm_w': 1.0 + nrm(ks[16], (D_MODEL,), 0.02),
    }


def reference(x, mix_norm_w, w_in, conv_w, conv_b, igate_b, fgate_b, mlstm_norm_w, w_out,
              ffn_norm_w, router_w, router_b, w_gate_up, b_gate_up, w_down, b_down, final_norm_w):
    bsz, seq = x.shape[0], x.shape[1]
    split_points = [int(p) for p in np.cumsum(PROJ_SIZES)[:-1]]
    idx_w_scale = (IDX_HEADS ** -0.5) * (IDX_DIM ** -0.5)
    for layer in range(DEPTH):
        h = rms_norm(x, mix_norm_w[layer])
        proj = h @ w_in[layer]
        aq, ak, av, iq, ik, iw, mq, mk, mv, mi, mf, mo = jnp.split(proj, split_points, axis=-1)
        aq = partial_rope(aq.reshape(bsz, seq, ATT_HEADS, HEAD_DIM), ATT_ROT_DIM)
        ak = partial_rope(ak.reshape(bsz, seq, ATT_KV_HEADS, HEAD_DIM), ATT_ROT_DIM)
        av = av.reshape(bsz, seq, ATT_KV_HEADS, HEAD_DIM)
        iq = partial_rope(iq.reshape(bsz, seq, IDX_HEADS, IDX_DIM), IDX_ROT_DIM)
        ik = partial_rope(ik.reshape(bsz, seq, 1, IDX_DIM), IDX_ROT_DIM)[:, :, 0, :]
        att_out = dsa_attention(aq, ak, av, iq, ik, iw * idx_w_scale)
        mem_out = mlstm_mixer(mq, mk, mv, mi, mf, mo, conv_w[layer], conv_b[layer],
                              igate_b[layer], fgate_b[layer], mlstm_norm_w[layer])
        x = x + jnp.concatenate([att_out, mem_out], axis=-1) @ w_out[layer]
        x = x + moe_ffn(rms_norm(x, ffn_norm_w[layer]), router_w[layer], router_b[layer],
                        w_gate_up[layer], b_gate_up[layer], w_down[layer], b_down[layer])
    return rms_norm(x, final_norm_w)
```

```python
import functools

import jax
import jax.numpy as jnp
import numpy as np
from jax import lax
from jax.experimental import pallas as pl
from jax.experimental.pallas import tpu as pltpu

F32 = jnp.float32
BF16 = jnp.bfloat16

LANES = 128
SUBLANES = 8
VMEM_LIMIT_BYTES = 56 * 1024 * 1024

NORM_EPS = 1e-6
ROPE_THETA = 500000.0
CHUNK = 64

HEAD_DIM = 64
ATT_HEADS = 8
ATT_KV_HEADS = 2
Q_PER_KV = ATT_HEADS // ATT_KV_HEADS
ATT_ROT_DIM = HEAD_DIM // 4
IDX_HEADS = 8
IDX_DIM = 32
IDX_ROT_DIM = IDX_DIM // 4
TOPK_MAX = 256
ATT_WIDTH = ATT_HEADS * HEAD_DIM

MLSTM_HEAD_DIM = 128
MLSTM_HEADS = 4
MLSTM_WIDTH = MLSTM_HEADS * MLSTM_HEAD_DIM
CONV_WIDTH = 4

N_EXPERTS = 32
TOP_K = 4
SWIGLU_ALPHA = 1.702
SWIGLU_LIMIT = 7.0

Q_BLOCK = 128
KEY_TILE = 256
BISECT_ITERS = 32
MLSTM_CHUNK = 256
EXPERT_ROWS = 256
NEG_BIG = -1e30


def _cparams(*sem):
    return pltpu.CompilerParams(dimension_semantics=sem, vmem_limit_bytes=VMEM_LIMIT_BYTES)


_SEG_WIDTHS = (("q", 512), ("k", 128), ("v", 128), ("iq", 256), ("ik", 128), ("iw", 128),
               ("mq", 512), ("mk", 512), ("mv", 512), ("mo", 512), ("g", 128))
_SEG = {}
_off = 0
for _name, _w in _SEG_WIDTHS:
    _SEG[_name] = (_off, _w)
    _off += _w
PACKED_WIDTH = _off


def _lane_col(x, idx):
    lane = lax.broadcasted_iota(jnp.int32, x.shape, 1)
    return jnp.sum(jnp.where(lane == idx, x, 0.0), axis=1, keepdims=True)


def _inproj_kernel(x_ref, nw_ref, w_ref, gb_ref, ca_ref, sa1_ref, sa2_ref, ci_ref, si1_ref, si2_ref,
                   q_ref, k_ref, v_ref, iq_ref, ik_ref, iw_ref, mq_ref, mk_ref, mv_ref, mo_ref, g_ref,
                   *, idx_w_scale):
    x = x_ref[...]
    var = jnp.mean(x * x, axis=-1, keepdims=True)
    h = (x * lax.rsqrt(var + NORM_EPS)) * nw_ref[...]
    hb = h.astype(BF16)

    def proj(name):
        lo, width = _SEG[name]
        return jnp.dot(hb, w_ref[:, lo:lo + width], preferred_element_type=F32)

    def rope(p, c_ref, s1_ref, s2_ref, half):
        c, s1, s2 = c_ref[...], s1_ref[...], s2_ref[...]
        outs = []
        for j in range(p.shape[1] // LANES):
            xs = p[:, j * LANES:(j + 1) * LANES]
            outs.append(xs * c + pltpu.roll(xs, LANES - half, 1) * s1 + pltpu.roll(xs, half, 1) * s2)
        return outs[0] if len(outs) == 1 else jnp.concatenate(outs, axis=1)

    att_scale = HEAD_DIM ** -0.5
    q_ref[...] = (rope(proj("q"), ca_ref, sa1_ref, sa2_ref, ATT_ROT_DIM // 2) * att_scale).astype(BF16)
    k_ref[...] = rope(proj("k"), ca_ref, sa1_ref, sa2_ref, ATT_ROT_DIM // 2).astype(BF16)
    v_ref[...] = proj("v").astype(BF16)
    iq_ref[...] = rope(proj("iq"), ci_ref, si1_ref, si2_ref, IDX_ROT_DIM // 2).astype(BF16)
    ik_ref[...] = rope(proj("ik"), ci_ref, si1_ref, si2_ref, IDX_ROT_DIM // 2).astype(BF16)
    iw_ref[...] = proj("iw") * idx_w_scale
    mq_ref[...] = proj("mq")
    mk_ref[...] = proj("mk")
    mv_ref[...] = proj("mv").astype(BF16)
    mo_ref[...] = proj("mo")
    g = proj("g") + gb_ref[...]
    lane = lax.broadcasted_iota(jnp.int32, g.shape, 1)
    log_f = jnp.minimum(g, 0.0) - jnp.log(1.0 + jnp.exp(-jnp.abs(g)))
    g_ref[...] = jnp.where(lane < MLSTM_HEADS, g, log_f)


def _rope_tables(seq, rot_dim, head_dim):
    pos = jnp.arange(seq, dtype=F32)
    inv_freq = ROPE_THETA ** (-jnp.arange(0, rot_dim, 2, dtype=F32) / rot_dim)
    ang = pos[:, None] * inv_freq[None, :]
    cos, sin = jnp.cos(ang), jnp.sin(ang)
    half = rot_dim // 2
    lane = np.arange(LANES) % head_dim
    fidx = lane % half
    first = lane < half
    second = (lane >= half) & (lane < rot_dim)
    cos_l, sin_l = cos[:, fidx], sin[:, fidx]
    c = jnp.where(jnp.asarray(first | second)[None, :], cos_l, 1.0)
    s1 = jnp.where(jnp.asarray(first)[None, :], -sin_l, 0.0)
    s2 = jnp.where(jnp.asarray(second)[None, :], sin_l, 0.0)
    return c, s1, s2


def _pack_w_in(w_in):
    sizes = (ATT_WIDTH, ATT_KV_HEADS * HEAD_DIM, ATT_KV_HEADS * HEAD_DIM, IDX_HEADS * IDX_DIM, IDX_DIM,
             IDX_HEADS, MLSTM_WIDTH, MLSTM_WIDTH, MLSTM_WIDTH, MLSTM_HEADS, MLSTM_HEADS, MLSTM_WIDTH)
    offs = np.concatenate([[0], np.cumsum(sizes)])
    aq, ak, av, iq, ik, iw, mq, mk, mv, mi, mf, mo = (w_in[:, offs[i]:offs[i + 1]] for i in range(12))
    d = w_in.shape[0]
    head_order = [h for s in range(Q_PER_KV) for h in (s, Q_PER_KV + s)]
    aq = aq.reshape(d, ATT_HEADS, HEAD_DIM)[:, head_order, :].reshape(d, ATT_WIDTH)
    ik_rep = jnp.tile(ik, (1, LANES // IDX_DIM))
    iw_pad = jnp.pad(iw, ((0, 0), (0, LANES - IDX_HEADS)))
    g_pad = jnp.pad(jnp.concatenate([mi, mf], axis=1), ((0, 0), (0, LANES - 2 * MLSTM_HEADS)))
    packed = jnp.concatenate([aq, ak, av, iq, ik_rep, iw_pad, mq, mk, mv, mo, g_pad], axis=1)
    assert packed.shape[1] == PACKED_WIDTH
    return packed.astype(BF16)


def _in_projection(x2, norm_w, w_in, igate_b, fgate_b, seq, tm=512):
    t, d = x2.shape
    assert t % tm == 0 and seq % tm == 0
    wp = _pack_w_in(w_in)
    gb = jnp.pad(jnp.concatenate([igate_b, fgate_b]), (0, LANES - 2 * MLSTM_HEADS)).reshape(1, LANES).astype(F32)
    ca, sa1, sa2 = _rope_tables(seq, ATT_ROT_DIM, HEAD_DIM)
    ci, si1, si2 = _rope_tables(seq, IDX_ROT_DIM, IDX_DIM)
    n_pos_blocks = seq // tm
    row = lambda i: (i, 0)
    fixed = lambda i: (0, 0)
    pos = lambda i: (i % n_pos_blocks, 0)
    tab = pl.BlockSpec((tm, LANES), pos)
    out_defs = (("q", 512, BF16), ("k", 128, BF16), ("v", 128, BF16), ("iq", 256, BF16), ("ik", 128, BF16),
                ("iw", 128, F32), ("mq", 512, F32), ("mk", 512, F32), ("mv", 512, BF16), ("mo", 512, F32),
                ("g", 128, F32))
    idx_w_scale = (IDX_HEADS ** -0.5) * (IDX_DIM ** -0.5)
    outs = pl.pallas_call(
        functools.partial(_inproj_kernel, idx_w_scale=idx_w_scale),
        grid=(t // tm,),
        in_specs=[pl.BlockSpec((tm, d), row), pl.BlockSpec((1, d), fixed),
                  pl.BlockSpec((d, PACKED_WIDTH), fixed), pl.BlockSpec((1, LANES), fixed),
                  tab, tab, tab, tab, tab, tab],
        out_specs=[pl.BlockSpec((tm, w), row) for _, w, _ in out_defs],
        out_shape=[jax.ShapeDtypeStruct((t, w), dt) for _, w, dt in out_defs],
        compiler_params=_cparams("parallel"),
        name="in_projection",
    )(x2, norm_w.reshape(1, d), wp, gb, ca, sa1, sa2, ci, si1, si2)
    return dict(zip([n for n, _, _ in out_defs], outs))


def _dsa_kernel(q_ref, iq_ref, iw_ref, k_ref, vt_ref, ik_ref, o_ref, score_ref, acc_ref, vk_ref, jcut_ref,
                *, k_sel, index_iters):
    i = pl.program_id(1)
    n_keys = Q_BLOCK * (i + 1)
    n_tiles = (n_keys + KEY_TILE - 1) // KEY_TILE
    n_stack = ATT_HEADS * Q_BLOCK

    lane = lax.broadcasted_iota(jnp.int32, (Q_BLOCK, LANES), 1)
    iq = iq_ref[0]
    per_slab = LANES // IDX_DIM
    qi_stack = jnp.concatenate(
        [jnp.where((lane // IDX_DIM) == (h % per_slab), iq[:, (h // per_slab) * LANES:(h // per_slab + 1) * LANES], 0)
         for h in range(IDX_HEADS)], axis=0)
    q = q_ref[0]
    q_stack = jnp.concatenate(
        [jnp.where((lane // HEAD_DIM) == g, q[:, s * LANES:(s + 1) * LANES], 0)
         for s in range(Q_PER_KV) for g in range(ATT_KV_HEADS)], axis=0)
    w_t = iw_ref[0].T

    q_lane = lax.broadcasted_iota(jnp.int32, (1, Q_BLOCK), 1)
    key_limit = (2 * i + 1 + (q_lane >= CHUNK).astype(jnp.int32)) * CHUNK
    k_eff = jnp.minimum(key_limit, k_sel).astype(F32)
    key_iota = lax.broadcasted_iota(jnp.int32, (KEY_TILE, Q_BLOCK), 0)
    nt_dims = (((1,), (1,)), ((), ()))

    def idx_body(kt, carry):
        rmin, rmax = carry
        start = pl.multiple_of(kt * KEY_TILE, KEY_TILE)
        z = lax.dot_general(ik_ref[0, pl.ds(start, KEY_TILE), :], qi_stack, nt_dims,
                            preferred_element_type=F32)
        sc = jnp.zeros((KEY_TILE, Q_BLOCK), F32)
        for h in range(IDX_HEADS):
            sc = sc + jnp.maximum(z[:, h * Q_BLOCK:(h + 1) * Q_BLOCK], 0.0) * w_t[h:h + 1, :]
        adm = (key_iota + start) < key_limit
        score_ref[pl.ds(start, KEY_TILE), :] = jnp.where(adm, sc, -jnp.inf)
        rmin = jnp.minimum(rmin, jnp.min(jnp.where(adm, sc, jnp.inf), axis=0, keepdims=True))
        rmax = jnp.maximum(rmax, jnp.max(jnp.where(adm, sc, -jnp.inf), axis=0, keepdims=True))
        return rmin, rmax

    init = (jnp.full((1, Q_BLOCK), jnp.inf, F32), jnp.full((1, Q_BLOCK), -jnp.inf, F32))
    lo, hi = lax.fori_loop(0, n_tiles, idx_body, init)

    def fold(x):
        return x.reshape(KEY_TILE // SUBLANES, SUBLANES, Q_BLOCK)

    def count(pred):
        def body(kt, acc):
            start = pl.multiple_of(kt * KEY_TILE, KEY_TILE)
            hit = pred(score_ref[pl.ds(start, KEY_TILE), :], key_iota + start).astype(F32)
            return acc + jnp.sum(fold(hit), axis=0)
        acc = lax.fori_loop(0, n_tiles, body, jnp.zeros((SUBLANES, Q_BLOCK), F32))
        return jnp.sum(acc, axis=0, keepdims=True)

    def bisect(_, carry):
        lo, hi, c_lo = carry
        mid = 0.5 * (lo + hi)
        c_mid = count(lambda x, _: x >= mid)
        ok = c_mid >= k_eff
        return jnp.where(ok, mid, lo), jnp.where(ok, hi, mid), jnp.where(ok, c_mid, c_lo)

    lo, _, c_lo = lax.fori_loop(0, BISECT_ITERS, bisect, (lo, hi, key_limit.astype(F32)))

    vk_ref[...] = jnp.broadcast_to(lo, vk_ref.shape)
    jcut_ref[...] = jnp.full(jcut_ref.shape, jnp.iinfo(jnp.int32).max, jnp.int32)

    @pl.when(jnp.max(c_lo - k_eff) > 0.0)
    def _():
        def min_body(kt, acc):
            start = pl.multiple_of(kt * KEY_TILE, KEY_TILE)
            x = score_ref[pl.ds(start, KEY_TILE), :]
            return jnp.minimum(acc, jnp.min(fold(jnp.where(x >= lo, x, jnp.inf)), axis=0))
        vk = jnp.min(lax.fori_loop(0, n_tiles, min_body, jnp.full((SUBLANES, Q_BLOCK), jnp.inf, F32)),
                     axis=0, keepdims=True)
        need = k_eff - count(lambda x, _: x > vk)

        def jbisect(_, carry):
            jlo, jhi = carry
            jmid = jnp.right_shift(jlo + jhi, 1)
            ok = count(lambda x, kidx: (x == vk) & (kidx < jmid)) >= need
            return jnp.where(ok, jlo, jmid), jnp.where(ok, jmid, jhi)

        j0 = (jnp.zeros((1, Q_BLOCK), jnp.int32), jnp.full((1, Q_BLOCK), n_tiles * KEY_TILE, jnp.int32))
        _, jcut = lax.fori_loop(0, index_iters, jbisect, j0)
        vk_ref[...] = jnp.broadcast_to(vk, vk_ref.shape)
        jcut_ref[...] = jnp.broadcast_to(jcut, jcut_ref.shape)

    vk = vk_ref[0:1, :]
    jcut = jcut_ref[0:1, :]

    acc_ref[...] = jnp.zeros_like(acc_ref)

    def att_body(kt, carry):
        m, l = carry
        start = pl.multiple_of(kt * KEY_TILE, KEY_TILE)
        x = score_ref[pl.ds(start, KEY_TILE), :]
        sel = (x > vk) | ((x == vk) & ((key_iota + start) < jcut))
        s = lax.dot_general(k_ref[0, pl.ds(start, KEY_TILE), :], q_stack, nt_dims,
                            preferred_element_type=F32)
        s = jnp.concatenate([jnp.where(sel, s[:, p * Q_BLOCK:(p + 1) * Q_BLOCK], NEG_BIG)
                             for p in range(ATT_HEADS)], axis=1)
        m_new = jnp.maximum(m, jnp.max(s, axis=0, keepdims=True))
        alpha = jnp.exp(m - m_new)
        p_t = jnp.exp(s - m_new)
        l = alpha * l + jnp.sum(p_t, axis=0, keepdims=True)
        acc_ref[...] = alpha * acc_ref[...] + jnp.dot(vt_ref[0, kt], p_t.astype(BF16),
                                                      preferred_element_type=F32)
        return m_new, l

    m0 = jnp.full((1, n_stack), NEG_BIG, F32)
    l0 = jnp.zeros((1, n_stack), F32)
    _, l = lax.fori_loop(0, n_tiles, att_body, (m0, l0))

    out_t = acc_ref[...] / l
    for s in range(Q_PER_KV):
        o0 = out_t[:, (2 * s) * Q_BLOCK:(2 * s + 1) * Q_BLOCK].T
        o1 = out_t[:, (2 * s + 1) * Q_BLOCK:(2 * s + 2) * Q_BLOCK].T
        o_ref[0, :, s * LANES:(s + 1) * LANES] = jnp.where(lane < HEAD_DIM, o0, o1).astype(o_ref.dtype)


def _dsa_attention(q, iq, iw, k, v, ik, k_sel):
    bsz, seq, _ = q.shape
    assert seq % KEY_TILE == 0 and seq % Q_BLOCK == 0
    n_kt = seq // KEY_TILE
    vt = v.reshape(bsz, n_kt, KEY_TILE, LANES).transpose(0, 1, 3, 2)
    blk = lambda w: pl.BlockSpec((1, Q_BLOCK, w), lambda b, i: (b, i, 0))
    full = pl.BlockSpec((1, seq, LANES), lambda b, i: (b, 0, 0))
    return pl.pallas_call(
        functools.partial(_dsa_kernel, k_sel=k_sel, index_iters=seq.bit_length()),
        grid=(bsz, seq // Q_BLOCK),
        in_specs=[blk(ATT_WIDTH), blk(IDX_HEADS * IDX_DIM), blk(LANES), full,
                  pl.BlockSpec((1, n_kt, LANES, KEY_TILE), lambda b, i: (b, 0, 0, 0)), full],
        out_specs=blk(ATT_WIDTH),
        out_shape=jax.ShapeDtypeStruct((bsz, seq, ATT_WIDTH), BF16),
        scratch_shapes=[pltpu.VMEM((seq, Q_BLOCK), F32), pltpu.VMEM((LANES, ATT_HEADS * Q_BLOCK), F32),
                        pltpu.VMEM((SUBLANES, Q_BLOCK), F32), pltpu.VMEM((SUBLANES, Q_BLOCK), jnp.int32)],
        compiler_params=_cparams("parallel", "parallel"),
        name="dsa_attention",
    )(q, iq, iw, k, vt, ik)


def _mlstm_kernel(mq_ref, mk_ref, mv_ref, mo_ref, gc_ref, gr_ref, cw_ref, cb_ref, nw_ref, o_ref,
                  c_ref, n_ref, m_ref, prev_ref):
    L = MLSTM_CHUNK
    W = MLSTM_WIDTH
    dh = MLSTM_HEAD_DIM

    @pl.when(pl.program_id(1) == 0)
    def _():
        c_ref[...] = jnp.zeros_like(c_ref)
        n_ref[...] = jnp.zeros_like(n_ref)
        m_ref[...] = jnp.zeros_like(m_ref)
        prev_ref[...] = jnp.zeros_like(prev_ref)

    row = lax.broadcasted_iota(jnp.int32, (L, W), 0)

    def conv_silu(x, prev8, w4, b):
        y = x * w4[CONV_WIDTH - 1:CONV_WIDTH, :] + b
        for d in range(1, CONV_WIDTH):
            xr = pltpu.roll(x, d, 0)
            pr = pltpu.roll(prev8, d, 0)
            pr_full = jnp.broadcast_to(pr[None], (L // SUBLANES, SUBLANES, W)).reshape(L, W)
            y = y + jnp.where(row < d, pr_full, xr) * w4[CONV_WIDTH - 1 - d:CONV_WIDTH - d, :]
        return y / (1.0 + jnp.exp(-y))

    xq, xk = mq_ref[0], mk_ref[0]
    cw, cb = cw_ref[...], cb_ref[...]
    q = conv_silu(xq, prev_ref[:, :W], cw[:, :W], cb[:, :W])
    k = conv_silu(xk, prev_ref[:, W:], cw[:, W:], cb[:, W:]) * (dh ** -0.5)
    prev_ref[:, :W] = xq[L - SUBLANES:, :]
    prev_ref[:, W:] = xk[L - SUBLANES:, :]

    g_col = gc_ref[0]
    g_row = gr_ref[0]
    r_i = lax.broadcasted_iota(jnp.int32, (L, L), 0)
    c_i = lax.broadcasted_iota(jnp.int32, (L, L), 1)
    causal = r_i >= c_i
    tri = causal.astype(F32)
    b_col_all = jnp.dot(tri, g_col, preferred_element_type=F32, precision=lax.Precision.HIGHEST)
    b_row_all = jnp.dot(g_row, (r_i <= c_i).astype(F32), preferred_element_type=F32,
                        precision=lax.Precision.HIGHEST)

    v = mv_ref[0]
    mo = mo_ref[0]
    nw = nw_ref[...]
    for h in range(MLSTM_HEADS):
        hs = slice(h * dh, (h + 1) * dh)
        i_col = _lane_col(g_col, h)
        i_row = g_row[h:h + 1, :]
        b_col = _lane_col(b_col_all, MLSTM_HEADS + h)
        b_row = b_row_all[MLSTM_HEADS + h:MLSTM_HEADS + h + 1, :]
        b_last = jnp.sum(g_row[MLSTM_HEADS + h:MLSTM_HEADS + h + 1, :], axis=1, keepdims=True)
        m_prev = m_ref[h:h + 1, 0:1]
        c_mat = c_ref[h]
        n_vec = n_ref[h:h + 1, :]

        d_log = jnp.where(causal, b_col + (i_row - b_row), -jnp.inf)
        inter_log = b_col + m_prev
        m_t = jnp.maximum(inter_log, jnp.max(d_log, axis=1, keepdims=True))
        d_w = jnp.exp(d_log - m_t)
        inter_w = jnp.exp(inter_log - m_t)
        qh, kh, vh = q[:, hs], k[:, hs], v[:, hs]
        qb, kb = qh.astype(BF16), kh.astype(BF16)
        s = lax.dot_general(qb, kb, (((1,), (1,)), ((), ())), preferred_element_type=F32) * d_w
        num = (jnp.dot(s.astype(BF16), vh, preferred_element_type=F32)
               + inter_w * jnp.dot(qb, c_mat.astype(BF16), preferred_element_type=F32))
        den = jnp.sum(s, axis=1, keepdims=True) + inter_w * jnp.sum(qh * n_vec, axis=1, keepdims=True)
        hout = num / jnp.maximum(jnp.abs(den), jnp.exp(-m_t))

        w_log = b_last - b_col + i_col
        m_new = jnp.maximum(b_last + m_prev, jnp.max(w_log, axis=0, keepdims=True))
        decay = jnp.exp(b_last + m_prev - m_new)
        kw = kh * jnp.exp(w_log - m_new)
        c_ref[h] = decay * c_mat + jnp.dot(kw.T.astype(BF16), vh, preferred_element_type=F32)
        n_ref[h:h + 1, :] = decay * n_vec + jnp.sum(kw, axis=0, keepdims=True)
        m_ref[h:h + 1, :] = jnp.broadcast_to(m_new, (1, LANES))

        hn = hout * lax.rsqrt(jnp.mean(hout * hout, axis=-1, keepdims=True) + NORM_EPS)
        gate = 1.0 / (1.0 + jnp.exp(-mo[:, hs]))
        o_ref[0, :, hs] = (hn * nw[:, hs] * gate).astype(o_ref.dtype)


def _mlstm_mixer(mq, mk, mv, mo, g, conv_w, conv_b, norm_w):
    bsz, seq, w = mq.shape
    L = MLSTM_CHUNK
    assert seq % L == 0
    g_row = g[:, :, :SUBLANES].transpose(0, 2, 1)
    blk = lambda width: pl.BlockSpec((1, L, width), lambda b, c: (b, c, 0))
    fixed = lambda shape: pl.BlockSpec(shape, lambda b, c: (0, 0))
    return pl.pallas_call(
        _mlstm_kernel,
        grid=(bsz, seq // L),
        in_specs=[blk(w), blk(w), blk(w), blk(w), blk(LANES),
                  pl.BlockSpec((1, SUBLANES, L), lambda b, c: (b, 0, c)),
                  fixed((CONV_WIDTH, 2 * w)), fixed((1, 2 * w)), fixed((1, w))],
        out_specs=blk(w),
        out_shape=jax.ShapeDtypeStruct((bsz, seq, w), BF16),
        scratch_shapes=[pltpu.VMEM((MLSTM_HEADS, MLSTM_HEAD_DIM, MLSTM_HEAD_DIM), F32),
                        pltpu.VMEM((SUBLANES, MLSTM_HEAD_DIM), F32),
                        pltpu.VMEM((SUBLANES, LANES), F32),
                        pltpu.VMEM((SUBLANES, 2 * w), F32)],
        compiler_params=_cparams("parallel", "arbitrary"),
        name="mlstm_mixer",
    )(mq, mk, mv, mo, g, g_row, conv_w, conv_b.reshape(1, 2 * w), norm_w.reshape(1, w))


def _outproj_router_kernel(att_ref, mem_ref, x_ref, wa_ref, wm_ref, nw_ref, rw_ref, rb_ref,
                           x1_ref, hn_ref, ti_ref, tg_ref):
    y = (jnp.dot(att_ref[...], wa_ref[...], preferred_element_type=F32)
         + jnp.dot(mem_ref[...], wm_ref[...], preferred_element_type=F32))
    x1 = x_ref[...] + y
    x1_ref[...] = x1
    var = jnp.mean(x1 * x1, axis=-1, keepdims=True)
    hn = (x1 * lax.rsqrt(var + NORM_EPS)) * nw_ref[...]
    hn_ref[...] = hn
    logits = jnp.dot(hn, rw_ref[...], preferred_element_type=F32, precision=lax.Precision.HIGHEST) + rb_ref[...]
    lane_i = lax.broadcasted_iota(jnp.int32, logits.shape, 1)
    lane = lane_i.astype(F32)
    cur = jnp.where(lane_i < N_EXPERTS, logits, -jnp.inf)
    vals, idxs = [], []
    for _ in range(TOP_K):
        mx = jnp.max(cur, axis=1, keepdims=True)
        am = jnp.min(jnp.where(cur == mx, lane, float(LANES)), axis=1, keepdims=True)
        vals.append(mx)
        idxs.append(am)
        cur = jnp.where(lane == am, -jnp.inf, cur)
    exps = [jnp.exp(vk - vals[0]) for vk in vals]
    denom = exps[0]
    for e in exps[1:]:
        denom = denom + e
    ti = jnp.zeros(logits.shape, F32)
    tg = jnp.zeros(logits.shape, F32)
    for kk in range(TOP_K):
        ti = jnp.where(lane_i == kk, idxs[kk], ti)
        tg = jnp.where(lane_i == kk, exps[kk] / denom, tg)
    ti_ref[...] = ti.astype(jnp.int32)
    tg_ref[...] = tg


def _outproj_router(att, mem, x2, w_att, w_mem, ffn_norm_w, router_w, router_b, tm=256):
    t, d = x2.shape
    row = lambda i: (i, 0)
    fixed = lambda i: (0, 0)
    rw = jnp.pad(router_w, ((0, 0), (0, LANES - N_EXPERTS)))
    rb = jnp.pad(router_b, (0, LANES - N_EXPERTS)).reshape(1, LANES)
    return pl.pallas_call(
        _outproj_router_kernel,
        grid=(t // tm,),
        in_specs=[pl.BlockSpec((tm, ATT_WIDTH), row), pl.BlockSpec((tm, MLSTM_WIDTH), row),
                  pl.BlockSpec((tm, d), row), pl.BlockSpec((ATT_WIDTH, d), fixed),
                  pl.BlockSpec((MLSTM_WIDTH, d), fixed), pl.BlockSpec((1, d), fixed),
                  pl.BlockSpec((d, LANES), fixed), pl.BlockSpec((1, LANES), fixed)],
        out_specs=[pl.BlockSpec((tm, d), row), pl.BlockSpec((tm, d), row),
                   pl.BlockSpec((tm, LANES), row), pl.BlockSpec((tm, LANES), row)],
        out_shape=[jax.ShapeDtypeStruct((t, d), F32), jax.ShapeDtypeStruct((t, d), F32),
                   jax.ShapeDtypeStruct((t, LANES), jnp.int32), jax.ShapeDtypeStruct((t, LANES), F32)],
        compiler_params=_cparams("parallel"),
        name="outproj_router",
    )(att, mem, x2, w_att, w_mem, ffn_norm_w.reshape(1, d), rw, rb)


GATHER_ROWS = 1024


def _row_copy(src_hbm, dst_hbm, sem, src_row, dst_row):
    return pltpu.make_async_copy(src_hbm.at[pl.ds(src_row, 1), :], dst_hbm.at[pl.ds(dst_row, 1), :], sem)


def _gather_kernel(tok_ref, src_hbm, dst_hbm, sem):
    base = pl.program_id(0) * GATHER_ROWS

    def issue(r, c):
        _row_copy(src_hbm, dst_hbm, sem, tok_ref[0, 0, r], base + r).start()
        return c
    lax.fori_loop(0, GATHER_ROWS, issue, 0)

    def drain(r, c):
        _row_copy(src_hbm, dst_hbm, sem, 0, base + r).wait()
        return c
    lax.fori_loop(0, GATHER_ROWS, drain, 0)


def _gather_rows(src, row_tok):
    n_rows = row_tok.shape[0]
    assert n_rows % GATHER_ROWS == 0
    nb = n_rows // GATHER_ROWS
    return pl.pallas_call(
        _gather_kernel,
        grid=(nb,),
        in_specs=[pl.BlockSpec((1, 1, GATHER_ROWS), lambda i: (i, 0, 0), memory_space=pltpu.SMEM),
                  pl.BlockSpec(memory_space=pl.ANY)],
        out_specs=pl.BlockSpec(memory_space=pl.ANY),
        out_shape=jax.ShapeDtypeStruct((n_rows, src.shape[1]), src.dtype),
        scratch_shapes=[pltpu.SemaphoreType.DMA(())],
        compiler_params=_cparams("arbitrary"),
        name="moe_gather",
    )(row_tok.reshape(nb, 1, GATHER_ROWS), src)


def _expert_kernel(be_ref, nu_ref, x_ref, wgu_ref, bgu_ref, wdn_ref, bdn_ref, o_ref, wgu_s, wdn_s):
    b = pl.program_id(0)
    e = be_ref[b]
    e_prev = be_ref[jnp.maximum(b - 1, 0)]
    d_e = wdn_s.shape[0]

    @pl.when((b == 0) | (e != e_prev))
    def _():
        wgu_s[...] = wgu_ref[0].astype(BF16)
        wdn_s[...] = wdn_ref[0].astype(BF16)

    @pl.when(b < nu_ref[0])
    def _():
        xb = x_ref[...].astype(BF16)
        bgu = bgu_ref[0]
        glu = jnp.dot(xb, wgu_s[:, :d_e], preferred_element_type=F32) + bgu[:, :d_e]
        lin = jnp.dot(xb, wgu_s[:, d_e:], preferred_element_type=F32) + bgu[:, d_e:]
        glu = jnp.minimum(glu, SWIGLU_LIMIT)
        lin = jnp.clip(lin, -SWIGLU_LIMIT, SWIGLU_LIMIT)
        act = glu * (1.0 / (1.0 + jnp.exp(-SWIGLU_ALPHA * glu))) * (lin + 1.0)
        o_ref[...] = jnp.dot(act.astype(BF16), wdn_s[...], preferred_element_type=F32) + bdn_ref[0]

    @pl.when(b >= nu_ref[0])
    def _():
        o_ref[...] = jnp.zeros_like(o_ref)


def _expert_ffn(x_buf, blk_expert, n_blocks_used, w_gu, b_gu, w_dn, b_dn):
    n_rows, d = x_buf.shape
    ne, _, d2 = w_gu.shape
    d_e = w_dn.shape[1]
    nb = n_rows // EXPERT_ROWS
    last_used = lambda b, be, nu: (jnp.minimum(b, jnp.maximum(nu[0] - 1, 0)), 0)
    return pl.pallas_call(
        _expert_kernel,
        grid_spec=pltpu.PrefetchScalarGridSpec(
            num_scalar_prefetch=2,
            grid=(nb,),
            in_specs=[pl.BlockSpec((EXPERT_ROWS, d), last_used),
                      pl.BlockSpec((1, d, d2), lambda b, be, nu: (be[b], 0, 0)),
                      pl.BlockSpec((1, 1, d2), lambda b, be, nu: (be[b], 0, 0)),
                      pl.BlockSpec((1, d_e, d), lambda b, be, nu: (be[b], 0, 0)),
                      pl.BlockSpec((1, 1, d), lambda b, be, nu: (be[b], 0, 0))],
            out_specs=pl.BlockSpec((EXPERT_ROWS, d), lambda b, be, nu: (b, 0)),
            scratch_shapes=[pltpu.VMEM((d, d2), BF16), pltpu.VMEM((d_e, d), BF16)],
        ),
        out_shape=jax.ShapeDtypeStruct((n_rows, d), F32),
        compiler_params=_cparams("arbitrary"),
        name="moe_experts",
    )(blk_expert, n_blocks_used, x_buf, w_gu, b_gu.reshape(ne, 1, d2), w_dn, b_dn.reshape(ne, 1, d))


COMBINE_ROWS = 256


def _combine_kernel(pos_ref, g_ref, x1_ref, nw_ref, y_hbm, o_ref, buf, sem, *, apply_norm):
    tm = COMBINE_ROWS

    def issue(j, c):
        for kk in range(TOP_K):
            pltpu.make_async_copy(y_hbm.at[pl.ds(pos_ref[0, 0, j * TOP_K + kk], 1), :],
                                  buf.at[kk, pl.ds(j, 1), :], sem).start()
        return c
    lax.fori_loop(0, tm, issue, 0)

    def drain(j, c):
        for kk in range(TOP_K):
            pltpu.make_async_copy(y_hbm.at[pl.ds(0, 1), :], buf.at[kk, pl.ds(j, 1), :], sem).wait()
        return c
    lax.fori_loop(0, tm, drain, 0)

    g = g_ref[...]
    acc = x1_ref[...]
    for kk in range(TOP_K):
        acc = acc + buf[kk] * _lane_col(g, kk)
    if apply_norm:
        var = jnp.mean(acc * acc, axis=-1, keepdims=True)
        acc = (acc * lax.rsqrt(var + NORM_EPS)) * nw_ref[...]
    o_ref[...] = acc


def _combine(y_buf, pos, gates, x1, final_norm_w, apply_norm):
    t, d = x1.shape
    tm = COMBINE_ROWS
    nb = t // tm
    row = lambda i: (i, 0)
    return pl.pallas_call(
        functools.partial(_combine_kernel, apply_norm=apply_norm),
        grid=(nb,),
        in_specs=[pl.BlockSpec((1, 1, tm * TOP_K), lambda i: (i, 0, 0), memory_space=pltpu.SMEM),
                  pl.BlockSpec((tm, LANES), row), pl.BlockSpec((tm, d), row),
                  pl.BlockSpec((1, d), lambda i: (0, 0)), pl.BlockSpec(memory_space=pl.ANY)],
        out_specs=pl.BlockSpec((tm, d), row),
        out_shape=jax.ShapeDtypeStruct((t, d), F32),
        scratch_shapes=[pltpu.VMEM((TOP_K, tm, d), F32), pltpu.SemaphoreType.DMA(())],
        compiler_params=_cparams("arbitrary"),
        name="moe_combine",
    )(pos.reshape(nb, 1, tm * TOP_K), gates, x1, final_norm_w.reshape(1, d), y_buf)


def _routing_tables(top_idx, n_tokens):
    onehot = (top_idx[:, :, None] == jnp.arange(N_EXPERTS, dtype=jnp.int32)[None, None, :]).astype(jnp.int32)
    cnt_tok = jnp.sum(onehot, axis=1)
    excl = jnp.cumsum(cnt_tok, axis=0) - cnt_tok
    rank = jnp.take_along_axis(excl, top_idx, axis=1)
    counts = jnp.sum(cnt_tok, axis=0)
    padded = (counts + EXPERT_ROWS - 1) // EXPERT_ROWS * EXPERT_ROWS
    pad_end = jnp.cumsum(padded)
    pad_start = pad_end - padded
    dest = (pad_start[top_idx] + rank).astype(jnp.int32)
    a = n_tokens * TOP_K
    n_rows = -(-(a + N_EXPERTS * EXPERT_ROWS) // GATHER_ROWS) * GATHER_ROWS
    n_blocks = n_rows // EXPERT_ROWS
    tok = jnp.repeat(jnp.arange(n_tokens, dtype=jnp.int32), TOP_K)
    row_tok = jnp.zeros((n_rows,), jnp.int32).at[dest.reshape(a)].set(tok)
    blk_start = jnp.arange(n_blocks, dtype=jnp.int32) * EXPERT_ROWS
    blk_expert = jnp.minimum(jnp.sum((pad_end[None, :] <= blk_start[:, None]).astype(jnp.int32), axis=1),
                             N_EXPERTS - 1).astype(jnp.int32)
    n_blocks_used = (pad_end[-1:] // EXPERT_ROWS).astype(jnp.int32)
    return dest, row_tok, blk_expert, n_blocks_used


def kernel(x, mix_norm_w, w_in, conv_w, conv_b, igate_b, fgate_b, mlstm_norm_w, w_out, ffn_norm_w,
           router_w, router_b, w_gate_up, b_gate_up, w_down, b_down, final_norm_w):
    bsz, seq, d = x.shape
    depth = w_in.shape[0]
    t = bsz * seq
    k_sel = min(TOPK_MAX, seq // 4)
    head_order = [h for s in range(Q_PER_KV) for h in (s, Q_PER_KV + s)]
    x2 = x.reshape(t, d)
    for layer in range(depth):
        p = _in_projection(x2, mix_norm_w[layer], w_in[layer], igate_b[layer], fgate_b[layer], seq)
        r3 = lambda a: a.reshape(bsz, seq, a.shape[-1])
        att = _dsa_attention(r3(p["q"]), r3(p["iq"]), r3(p["iw"]), r3(p["k"]), r3(p["v"]), r3(p["ik"]), k_sel)
        mem = _mlstm_mixer(r3(p["mq"]), r3(p["mk"]), r3(p["mv"]), r3(p["mo"]), r3(p["g"]),
                           conv_w[layer], conv_b[layer], mlstm_norm_w[layer])
        wo = w_out[layer]
        w_att = wo[:ATT_WIDTH].reshape(ATT_HEADS, HEAD_DIM, d)[jnp.asarray(head_order)].reshape(ATT_WIDTH, d)
        x1, hn, top_i, top_g = _outproj_router(
            att.reshape(t, ATT_WIDTH), mem.reshape(t, MLSTM_WIDTH), x2, w_att.astype(BF16),
            wo[ATT_WIDTH:].astype(BF16), ffn_norm_w[layer], router_w[layer], router_b[layer])
        dest, row_tok, blk_expert, n_blocks_used = _routing_tables(top_i[:, :TOP_K], t)
        x_buf = _gather_rows(hn, row_tok)
        y_buf = _expert_ffn(x_buf, blk_expert, n_blocks_used, w_gate_up[layer], b_gate_up[layer],
                            w_down[layer], b_down[layer])
        x2 = _combine(y_buf, dest, top_g, x1, final_norm_w, apply_norm=(layer == depth - 1))
    return x2.reshape(bsz, seq, d)
```

```python
import functools

import jax
import jax.numpy as jnp
import numpy as np
from jax import lax
from jax.experimental import pallas as pl
from jax.experimental.pallas import tpu as pltpu

F32 = jnp.float32
BF16 = jnp.bfloat16

LANES = 128
SUBLANES = 8
VMEM_LIMIT_BYTES = 56 * 1024 * 1024

NORM_EPS = 1e-6
ROPE_THETA = 500000.0
CHUNK = 64

HEAD_DIM = 64
ATT_HEADS = 8
ATT_KV_HEADS = 2
Q_PER_KV = ATT_HEADS // ATT_KV_HEADS
ATT_ROT_DIM = HEAD_DIM // 4
IDX_HEADS = 8
IDX_DIM = 32
IDX_ROT_DIM = IDX_DIM // 4
TOPK_MAX = 256
ATT_WIDTH = ATT_HEADS * HEAD_DIM

MLSTM_HEAD_DIM = 128
MLSTM_HEADS = 4
MLSTM_WIDTH = MLSTM_HEADS * MLSTM_HEAD_DIM
CONV_WIDTH = 4

N_EXPERTS = 32
TOP_K = 4
SWIGLU_ALPHA = 1.702
SWIGLU_LIMIT = 7.0

Q_BLOCK = 128
KEY_TILE = 256
BISECT_ITERS = 32
BISECT_GROUP = 4
FOLD_ROWS = 32
TILES_PER_STEP = 2
V_ROWS = LANES + 16
MLSTM_CHUNK = 256
EXPERT_ROWS = 256
NEG_BIG = -1e30


def _cparams(*sem):
    return pltpu.CompilerParams(dimension_semantics=sem, vmem_limit_bytes=VMEM_LIMIT_BYTES)


_SEG_WIDTHS = (("q", 512), ("k", 128), ("v", 128), ("iq", 256), ("ik", 128), ("iw", 128),
               ("mq", 512), ("mk", 512), ("mv", 512), ("mo", 512), ("g", 128))
_SEG = {}
_off = 0
for _name, _w in _SEG_WIDTHS:
    _SEG[_name] = (_off, _w)
    _off += _w
PACKED_WIDTH = _off


def _lane_col(x, idx):
    lane = lax.broadcasted_iota(jnp.int32, x.shape, 1)
    return jnp.sum(jnp.where(lane == idx, x, 0.0), axis=1, keepdims=True)


def _inproj_kernel(x_ref, nw_ref, w_ref, gb_ref, ca_ref, sa1_ref, sa2_ref, ci_ref, si1_ref, si2_ref,
                   q_ref, k_ref, v_ref, iq_ref, ik_ref, iw_ref, mq_ref, mk_ref, mv_ref, mo_ref, g_ref,
                   *, idx_w_scale):
    x = x_ref[...]
    var = jnp.mean(x * x, axis=-1, keepdims=True)
    h = (x * lax.rsqrt(var + NORM_EPS)) * nw_ref[...]
    hb = h.astype(BF16)

    def proj(name):
        lo, width = _SEG[name]
        return jnp.dot(hb, w_ref[:, lo:lo + width], preferred_element_type=F32)

    def rope(p, c_ref, s1_ref, s2_ref, half):
        c, s1, s2 = c_ref[...], s1_ref[...], s2_ref[...]
        outs = []
        for j in range(p.shape[1] // LANES):
            xs = p[:, j * LANES:(j + 1) * LANES]
            outs.append(xs * c + pltpu.roll(xs, LANES - half, 1) * s1 + pltpu.roll(xs, half, 1) * s2)
        return outs[0] if len(outs) == 1 else jnp.concatenate(outs, axis=1)

    att_scale = float(HEAD_DIM ** -0.5 * np.log2(np.e))
    q_ref[...] = (rope(proj("q"), ca_ref, sa1_ref, sa2_ref, ATT_ROT_DIM // 2) * att_scale).astype(BF16)
    k_ref[...] = rope(proj("k"), ca_ref, sa1_ref, sa2_ref, ATT_ROT_DIM // 2).astype(BF16)
    v_ref[...] = proj("v").astype(BF16)
    iq_ref[...] = rope(proj("iq"), ci_ref, si1_ref, si2_ref, IDX_ROT_DIM // 2).astype(BF16)
    ik_ref[...] = rope(proj("ik"), ci_ref, si1_ref, si2_ref, IDX_ROT_DIM // 2).astype(BF16)
    iw_ref[...] = proj("iw") * idx_w_scale
    mq_ref[...] = proj("mq")
    mk_ref[...] = proj("mk")
    mv_ref[...] = proj("mv").astype(BF16)
    mo_ref[...] = proj("mo")
    g = proj("g") + gb_ref[...]
    lane = lax.broadcasted_iota(jnp.int32, g.shape, 1)
    log_f = jnp.minimum(g, 0.0) - jnp.log(1.0 + jnp.exp(-jnp.abs(g)))
    g_ref[...] = jnp.where(lane < MLSTM_HEADS, g, log_f)


def _rope_tables(seq, rot_dim, head_dim):
    pos = jnp.arange(seq, dtype=F32)
    inv_freq = ROPE_THETA ** (-jnp.arange(0, rot_dim, 2, dtype=F32) / rot_dim)
    ang = pos[:, None] * inv_freq[None, :]
    cos, sin = jnp.cos(ang), jnp.sin(ang)
    half = rot_dim // 2
    lane = np.arange(LANES) % head_dim
    fidx = lane % half
    first = lane < half
    second = (lane >= half) & (lane < rot_dim)
    cos_l, sin_l = cos[:, fidx], sin[:, fidx]
    c = jnp.where(jnp.asarray(first | second)[None, :], cos_l, 1.0)
    s1 = jnp.where(jnp.asarray(first)[None, :], -sin_l, 0.0)
    s2 = jnp.where(jnp.asarray(second)[None, :], sin_l, 0.0)
    return c, s1, s2


def _pack_w_in(w_in):
    sizes = (ATT_WIDTH, ATT_KV_HEADS * HEAD_DIM, ATT_KV_HEADS * HEAD_DIM, IDX_HEADS * IDX_DIM, IDX_DIM,
             IDX_HEADS, MLSTM_WIDTH, MLSTM_WIDTH, MLSTM_WIDTH, MLSTM_HEADS, MLSTM_HEADS, MLSTM_WIDTH)
    offs = np.concatenate([[0], np.cumsum(sizes)])
    aq, ak, av, iq, ik, iw, mq, mk, mv, mi, mf, mo = (w_in[:, offs[i]:offs[i + 1]] for i in range(12))
    d = w_in.shape[0]
    head_order = [h for s in range(Q_PER_KV) for h in (s, Q_PER_KV + s)]
    aq = aq.reshape(d, ATT_HEADS, HEAD_DIM)[:, head_order, :].reshape(d, ATT_WIDTH)
    ik_rep = jnp.tile(ik, (1, LANES // IDX_DIM))
    iw_pad = jnp.pad(iw, ((0, 0), (0, LANES - IDX_HEADS)))
    g_pad = jnp.pad(jnp.concatenate([mi, mf], axis=1), ((0, 0), (0, LANES - 2 * MLSTM_HEADS)))
    packed = jnp.concatenate([aq, ak, av, iq, ik_rep, iw_pad, mq, mk, mv, mo, g_pad], axis=1)
    assert packed.shape[1] == PACKED_WIDTH
    return packed.astype(BF16)


def _in_projection(x2, norm_w, w_in, igate_b, fgate_b, seq, tm=512):
    t, d = x2.shape
    assert t % tm == 0 and seq % tm == 0
    wp = _pack_w_in(w_in)
    gb = jnp.pad(jnp.concatenate([igate_b, fgate_b]), (0, LANES - 2 * MLSTM_HEADS)).reshape(1, LANES).astype(F32)
    ca, sa1, sa2 = _rope_tables(seq, ATT_ROT_DIM, HEAD_DIM)
    ci, si1, si2 = _rope_tables(seq, IDX_ROT_DIM, IDX_DIM)
    n_pos_blocks = seq // tm
    row = lambda i: (i, 0)
    fixed = lambda i: (0, 0)
    pos = lambda i: (i % n_pos_blocks, 0)
    tab = pl.BlockSpec((tm, LANES), pos)
    out_defs = (("q", 512, BF16), ("k", 128, BF16), ("v", 128, BF16), ("iq", 256, BF16), ("ik", 128, BF16),
                ("iw", 128, F32), ("mq", 512, F32), ("mk", 512, F32), ("mv", 512, BF16), ("mo", 512, F32),
                ("g", 128, F32))
    idx_w_scale = (IDX_HEADS ** -0.5) * (IDX_DIM ** -0.5)
    outs = pl.pallas_call(
        functools.partial(_inproj_kernel, idx_w_scale=idx_w_scale),
        grid=(t // tm,),
        in_specs=[pl.BlockSpec((tm, d), row), pl.BlockSpec((1, d), fixed),
                  pl.BlockSpec((d, PACKED_WIDTH), fixed), pl.BlockSpec((1, LANES), fixed),
                  tab, tab, tab, tab, tab, tab],
        out_specs=[pl.BlockSpec((tm, w), row) for _, w, _ in out_defs],
        out_shape=[jax.ShapeDtypeStruct((t, w), dt) for _, w, dt in out_defs],
        compiler_params=_cparams("parallel"),
        name="in_projection",
    )(x2, norm_w.reshape(1, d), wp, gb, ca, sa1, sa2, ci, si1, si2)
    return dict(zip([n for n, _, _ in out_defs], outs))


def _dsa_kernel(q_ref, iq_ref, iw_ref, k_ref, vt_ref, ik_ref, o_ref, score_ref, acc_ref, vk_ref, jcut_ref,
                *, k_sel, index_iters):
    i = pl.program_id(1)
    n_keys = Q_BLOCK * (i + 1)
    n_steps = (n_keys + TILES_PER_STEP * KEY_TILE - 1) // (TILES_PER_STEP * KEY_TILE)

    def for_tiles(body, init, lead=None):
        def step(j, carry):
            tiles = [j * TILES_PER_STEP + u for u in range(TILES_PER_STEP)]
            opened = [lead(kt) if lead is not None else None for kt in tiles]
            for kt, head in zip(tiles, opened):
                carry = body(kt, carry) if lead is None else body(kt, head, carry)
            return carry
        return lax.fori_loop(0, n_steps, step, init)
    n_stack = ATT_HEADS * Q_BLOCK

    lane = lax.broadcasted_iota(jnp.int32, (Q_BLOCK, LANES), 1)
    iq = iq_ref[0]
    per_slab = LANES // IDX_DIM
    qi_stack = jnp.concatenate(
        [jnp.where((lane // IDX_DIM) == (h % per_slab), iq[:, (h // per_slab) * LANES:(h // per_slab + 1) * LANES], 0)
         for h in range(IDX_HEADS)], axis=0)
    q = q_ref[0]
    q_stack = jnp.concatenate(
        [jnp.where((lane // HEAD_DIM) == g, q[:, s * LANES:(s + 1) * LANES], 0)
         for s in range(Q_PER_KV) for g in range(ATT_KV_HEADS)], axis=0)
    w_t = iw_ref[0].T

    q_lane = lax.broadcasted_iota(jnp.int32, (1, Q_BLOCK), 1)
    key_limit = (2 * i + 1 + (q_lane >= CHUNK).astype(jnp.int32)) * CHUNK
    k_eff = jnp.minimum(key_limit, k_sel).astype(F32)
    key_iota = lax.broadcasted_iota(jnp.int32, (KEY_TILE, Q_BLOCK), 0)
    nt_dims = (((1,), (1,)), ((), ()))

    def idx_lead(kt):
        start = pl.multiple_of(kt * KEY_TILE, KEY_TILE)
        return lax.dot_general(ik_ref[0, pl.ds(start, KEY_TILE), :], qi_stack, nt_dims,
                               preferred_element_type=F32)

    def idx_body(kt, z, carry):
        rmin, rmax = carry
        start = pl.multiple_of(kt * KEY_TILE, KEY_TILE)
        sc = jnp.zeros((KEY_TILE, Q_BLOCK), F32)
        for h in range(IDX_HEADS):
            sc = sc + jnp.maximum(z[:, h * Q_BLOCK:(h + 1) * Q_BLOCK], 0.0) * w_t[h:h + 1, :]
        adm = (key_iota + start) < key_limit
        score_ref[pl.ds(start, KEY_TILE), :] = jnp.where(adm, sc, -jnp.inf)
        rmin = jnp.minimum(rmin, jnp.min(jnp.where(adm, sc, jnp.inf), axis=0, keepdims=True))
        rmax = jnp.maximum(rmax, jnp.max(jnp.where(adm, sc, -jnp.inf), axis=0, keepdims=True))
        return rmin, rmax

    init = (jnp.full((1, Q_BLOCK), jnp.inf, F32), jnp.full((1, Q_BLOCK), -jnp.inf, F32))
    lo, hi = for_tiles(idx_body, init, lead=idx_lead)

    def fold(x):
        return x.reshape(KEY_TILE // FOLD_ROWS, FOLD_ROWS, Q_BLOCK)

    def count(pred):
        def body(kt, acc):
            start = pl.multiple_of(kt * KEY_TILE, KEY_TILE)
            hit = pred(score_ref[pl.ds(start, KEY_TILE), :], key_iota + start).astype(F32)
            return acc + jnp.sum(fold(hit), axis=0)
        acc = for_tiles(body, jnp.zeros((FOLD_ROWS, Q_BLOCK), F32))
        return jnp.sum(acc, axis=0, keepdims=True)

    def unresolved(c_lo):
        return jnp.max(c_lo - k_eff) > 0.0

    def bisect(carry):
        it, lo, hi, c_lo = carry
        for _ in range(BISECT_GROUP):
            mid = 0.5 * (lo + hi)
            c_mid = count(lambda x, _: x >= mid)
            ok = c_mid >= k_eff
            lo, hi, c_lo = jnp.where(ok, mid, lo), jnp.where(ok, hi, mid), jnp.where(ok, c_mid, c_lo)
        return it + BISECT_GROUP, lo, hi, c_lo

    _, lo, _, c_lo = lax.while_loop(lambda c: (c[0] < BISECT_ITERS) & unresolved(c[3]), bisect,
                                    (jnp.int32(0), lo, hi, key_limit.astype(F32)))

    vk_ref[...] = jnp.broadcast_to(lo, vk_ref.shape)
    jcut_ref[...] = jnp.full(jcut_ref.shape, jnp.iinfo(jnp.int32).max, jnp.int32)

    @pl.when(unresolved(c_lo))
    def _():
        def min_body(kt, acc):
            start = pl.multiple_of(kt * KEY_TILE, KEY_TILE)
            x = score_ref[pl.ds(start, KEY_TILE), :]
            return jnp.minimum(acc, jnp.min(fold(jnp.where(x >= lo, x, jnp.inf)), axis=0))
        vk = jnp.min(for_tiles(min_body, jnp.full((FOLD_ROWS, Q_BLOCK), jnp.inf, F32)),
                     axis=0, keepdims=True)
        need = k_eff - count(lambda x, _: x > vk)

        def jbisect(_, carry):
            jlo, jhi = carry
            jmid = jnp.right_shift(jlo + jhi, 1)
            ok = count(lambda x, kidx: (x == vk) & (kidx < jmid)) >= need
            return jnp.where(ok, jlo, jmid), jnp.where(ok, jmid, jhi)

        j0 = (jnp.zeros((1, Q_BLOCK), jnp.int32), jnp.full((1, Q_BLOCK), n_keys, jnp.int32))
        _, jcut = lax.fori_loop(0, index_iters, jbisect, j0)
        vk_ref[...] = jnp.broadcast_to(vk, vk_ref.shape)
        jcut_ref[...] = jnp.broadcast_to(jcut, jcut_ref.shape)

    vk = vk_ref[0:1, :]
    jcut = jcut_ref[0:1, :]

    r_i = lax.broadcasted_iota(jnp.int32, (Q_BLOCK, LANES), 0)
    eye = jnp.where(r_i == lane, 1.0, 0.0).astype(BF16)
    q_aug = jnp.concatenate([q_stack, jnp.concatenate([eye] * ATT_HEADS, axis=0)], axis=1)
    acc_ref[...] = jnp.zeros_like(acc_ref)

    def att_lead(kt):
        start = pl.multiple_of(kt * KEY_TILE, KEY_TILE)
        x = score_ref[pl.ds(start, KEY_TILE), :]
        sel = (x > vk) | ((x == vk) & ((key_iota + start) < jcut))
        bias = jnp.where(sel, 0.0, NEG_BIG).astype(BF16)
        k_aug = jnp.concatenate([k_ref[0, pl.ds(start, KEY_TILE), :], bias], axis=1)
        return lax.dot_general(k_aug, q_aug, nt_dims, preferred_element_type=F32)

    def att_body(kt, s, m):
        m_new = jnp.maximum(m, jnp.max(s, axis=0, keepdims=True))
        p_t = jnp.exp2(s - m_new).astype(BF16)
        acc_ref[...] = jnp.exp2(m - m_new) * acc_ref[...] + jnp.dot(vt_ref[0, kt], p_t,
                                                                    preferred_element_type=F32)
        return m_new

    for_tiles(att_body, jnp.full((1, n_stack), NEG_BIG, F32), lead=att_lead)

    out_t = acc_ref[0:LANES, :] / acc_ref[LANES:LANES + 1, :]
    for s in range(Q_PER_KV):
        o0 = out_t[:, (2 * s) * Q_BLOCK:(2 * s + 1) * Q_BLOCK].T
        o1 = out_t[:, (2 * s + 1) * Q_BLOCK:(2 * s + 2) * Q_BLOCK].T
        o_ref[0, :, s * LANES:(s + 1) * LANES] = jnp.where(lane < HEAD_DIM, o0, o1).astype(o_ref.dtype)


def _dsa_attention(q, iq, iw, k, v, ik, k_sel):
    bsz, seq, _ = q.shape
    assert seq % (TILES_PER_STEP * KEY_TILE) == 0 and seq % Q_BLOCK == 0
    n_kt = seq // KEY_TILE
    vt = v.reshape(bsz, n_kt, KEY_TILE, LANES).transpose(0, 1, 3, 2)
    ones_rows = jnp.zeros((bsz, n_kt, V_ROWS - LANES, KEY_TILE), v.dtype).at[:, :, 0, :].set(1.0)
    vt = jnp.concatenate([vt, ones_rows], axis=2)
    blk = lambda w: pl.BlockSpec((1, Q_BLOCK, w), lambda b, i: (b, i, 0))
    full = pl.BlockSpec((1, seq, LANES), lambda b, i: (b, 0, 0))
    return pl.pallas_call(
        functools.partial(_dsa_kernel, k_sel=k_sel, index_iters=seq.bit_length()),
        grid=(bsz, seq // Q_BLOCK),
        in_specs=[blk(ATT_WIDTH), blk(IDX_HEADS * IDX_DIM), blk(LANES), full,
                  pl.BlockSpec((1, n_kt, V_ROWS, KEY_TILE), lambda b, i: (b, 0, 0, 0)), full],
        out_specs=blk(ATT_WIDTH),
        out_shape=jax.ShapeDtypeStruct((bsz, seq, ATT_WIDTH), BF16),
        scratch_shapes=[pltpu.VMEM((seq, Q_BLOCK), F32), pltpu.VMEM((V_ROWS, ATT_HEADS * Q_BLOCK), F32),
                        pltpu.VMEM((SUBLANES, Q_BLOCK), F32), pltpu.VMEM((SUBLANES, Q_BLOCK), jnp.int32)],
        compiler_params=_cparams("parallel", "parallel"),
        name="dsa_attention",
    )(q, iq, iw, k, vt, ik)


def _mlstm_kernel(mq_ref, mk_ref, mv_ref, mo_ref, gc_ref, gr_ref, cw_ref, cb_ref, nw_ref, o_ref,
                  c_ref, n_ref, m_ref, prev_ref):
    L = MLSTM_CHUNK
    W = MLSTM_WIDTH
    dh = MLSTM_HEAD_DIM

    @pl.when(pl.program_id(1) == 0)
    def _():
        c_ref[...] = jnp.zeros_like(c_ref)
        n_ref[...] = jnp.zeros_like(n_ref)
        m_ref[...] = jnp.zeros_like(m_ref)
        prev_ref[...] = jnp.zeros_like(prev_ref)

    row = lax.broadcasted_iota(jnp.int32, (L, W), 0)

    def conv_silu(x, prev8, w4, b):
        y = x * w4[CONV_WIDTH - 1:CONV_WIDTH, :] + b
        for d in range(1, CONV_WIDTH):
            xr = pltpu.roll(x, d, 0)
            pr = pltpu.roll(prev8, d, 0)
            pr_full = jnp.broadcast_to(pr[None], (L // SUBLANES, SUBLANES, W)).reshape(L, W)
            y = y + jnp.where(row < d, pr_full, xr) * w4[CONV_WIDTH - 1 - d:CONV_WIDTH - d, :]
        return y / (1.0 + jnp.exp(-y))

    xq, xk = mq_ref[0], mk_ref[0]
    cw, cb = cw_ref[...], cb_ref[...]
    q = conv_silu(xq, prev_ref[:, :W], cw[:, :W], cb[:, :W])
    k = conv_silu(xk, prev_ref[:, W:], cw[:, W:], cb[:, W:]) * (dh ** -0.5)
    prev_ref[:, :W] = xq[L - SUBLANES:, :]
    prev_ref[:, W:] = xk[L - SUBLANES:, :]

    g_col = gc_ref[0]
    g_row = gr_ref[0]
    r_i = lax.broadcasted_iota(jnp.int32, (L, L), 0)
    c_i = lax.broadcasted_iota(jnp.int32, (L, L), 1)
    causal = r_i >= c_i
    tri = causal.astype(F32)
    b_col_all = jnp.dot(tri, g_col, preferred_element_type=F32, precision=lax.Precision.HIGHEST)
    b_row_all = jnp.dot(g_row, (r_i <= c_i).astype(F32), preferred_element_type=F32,
                        precision=lax.Precision.HIGHEST)

    v = mv_ref[0]
    mo = mo_ref[0]
    nw = nw_ref[...]
    for h in range(MLSTM_HEADS):
        hs = slice(h * dh, (h + 1) * dh)
        i_col = _lane_col(g_col, h)
        i_row = g_row[h:h + 1, :]
        b_col = _lane_col(b_col_all, MLSTM_HEADS + h)
        b_row = b_row_all[MLSTM_HEADS + h:MLSTM_HEADS + h + 1, :]
        b_last = jnp.sum(g_row[MLSTM_HEADS + h:MLSTM_HEADS + h + 1, :], axis=1, keepdims=True)
        m_prev = m_ref[h:h + 1, 0:1]
        c_mat = c_ref[h]
        n_vec = n_ref[h:h + 1, :]

        d_log = jnp.where(causal, b_col + (i_row - b_row), -jnp.inf)
        inter_log = b_col + m_prev
        m_t = jnp.maximum(inter_log, jnp.max(d_log, axis=1, keepdims=True))
        d_w = jnp.exp(d_log - m_t)
        inter_w = jnp.exp(inter_log - m_t)
        qh, kh, vh = q[:, hs], k[:, hs], v[:, hs]
        qb, kb = qh.astype(BF16), kh.astype(BF16)
        s = lax.dot_general(qb, kb, (((1,), (1,)), ((), ())), preferred_element_type=F32) * d_w
        num = (jnp.dot(s.astype(BF16), vh, preferred_element_type=F32)
               + inter_w * jnp.dot(qb, c_mat.astype(BF16), preferred_element_type=F32))
        den = jnp.sum(s, axis=1, keepdims=True) + inter_w * jnp.sum(qh * n_vec, axis=1, keepdims=True)
        hout = num / jnp.maximum(jnp.abs(den), jnp.exp(-m_t))

        w_log = b_last - b_col + i_col
        m_new = jnp.maximum(b_last + m_prev, jnp.max(w_log, axis=0, keepdims=True))
        decay = jnp.exp(b_last + m_prev - m_new)
        kw = kh * jnp.exp(w_log - m_new)
        c_ref[h] = decay * c_mat + jnp.dot(kw.T.astype(BF16), vh, preferred_element_type=F32)
        n_ref[h:h + 1, :] = decay * n_vec + jnp.sum(kw, axis=0, keepdims=True)
        m_ref[h:h + 1, :] = jnp.broadcast_to(m_new, (1, LANES))

        hn = hout * lax.rsqrt(jnp.mean(hout * hout, axis=-1, keepdims=True) + NORM_EPS)
        gate = 1.0 / (1.0 + jnp.exp(-mo[:, hs]))
        o_ref[0, :, hs] = (hn * nw[:, hs] * gate).astype(o_ref.dtype)


def _mlstm_mixer(mq, mk, mv, mo, g, conv_w, conv_b, norm_w):
    bsz, seq, w = mq.shape
    L = MLSTM_CHUNK
    assert seq % L == 0
    g_row = g[:, :, :SUBLANES].transpose(0, 2, 1)
    blk = lambda width: pl.BlockSpec((1, L, width), lambda b, c: (b, c, 0))
    fixed = lambda shape: pl.BlockSpec(shape, lambda b, c: (0, 0))
    return pl.pallas_call(
        _mlstm_kernel,
        grid=(bsz, seq // L),
        in_specs=[blk(w), blk(w), blk(w), blk(w), blk(LANES),
                  pl.BlockSpec((1, SUBLANES, L), lambda b, c: (b, 0, c)),
                  fixed((CONV_WIDTH, 2 * w)), fixed((1, 2 * w)), fixed((1, w))],
        out_specs=blk(w),
        out_shape=jax.ShapeDtypeStruct((bsz, seq, w), BF16),
        scratch_shapes=[pltpu.VMEM((MLSTM_HEADS, MLSTM_HEAD_DIM, MLSTM_HEAD_DIM), F32),
                        pltpu.VMEM((SUBLANES, MLSTM_HEAD_DIM), F32),
                        pltpu.VMEM((SUBLANES, LANES), F32),
                        pltpu.VMEM((SUBLANES, 2 * w), F32)],
        compiler_params=_cparams("parallel", "arbitrary"),
        name="mlstm_mixer",
    )(mq, mk, mv, mo, g, g_row, conv_w, conv_b.reshape(1, 2 * w), norm_w.reshape(1, w))


def _outproj_router_kernel(att_ref, mem_ref, x_ref, wa_ref, wm_ref, nw_ref, rw_ref, rb_ref,
                           x1_ref, hn_ref, ti_ref, tg_ref):
    y = (jnp.dot(att_ref[...], wa_ref[...], preferred_element_type=F32)
         + jnp.dot(mem_ref[...], wm_ref[...], preferred_element_type=F32))
    x1 = x_ref[...] + y
    x1_ref[...] = x1
    var = jnp.mean(x1 * x1, axis=-1, keepdims=True)
    hn = (x1 * lax.rsqrt(var + NORM_EPS)) * nw_ref[...]
    hn_ref[...] = hn
    logits = jnp.dot(hn, rw_ref[...], preferred_element_type=F32, precision=lax.Precision.HIGHEST) + rb_ref[...]
    lane_i = lax.broadcasted_iota(jnp.int32, logits.shape, 1)
    lane = lane_i.astype(F32)
    cur = jnp.where(lane_i < N_EXPERTS, logits, -jnp.inf)
    vals, idxs = [], []
    for _ in range(TOP_K):
        mx = jnp.max(cur, axis=1, keepdims=True)
        am = jnp.min(jnp.where(cur == mx, lane, float(LANES)), axis=1, keepdims=True)
        vals.append(mx)
        idxs.append(am)
        cur = jnp.where(lane == am, -jnp.inf, cur)
    exps = [jnp.exp(vk - vals[0]) for vk in vals]
    denom = exps[0]
    for e in exps[1:]:
        denom = denom + e
    ti = jnp.zeros(logits.shape, F32)
    tg = jnp.zeros(logits.shape, F32)
    for kk in range(TOP_K):
        ti = jnp.where(lane_i == kk, idxs[kk], ti)
        tg = jnp.where(lane_i == kk, exps[kk] / denom, tg)
    ti_ref[...] = ti.astype(jnp.int32)
    tg_ref[...] = tg


def _outproj_router(att, mem, x2, w_att, w_mem, ffn_norm_w, router_w, router_b, tm=256):
    t, d = x2.shape
    row = lambda i: (i, 0)
    fixed = lambda i: (0, 0)
    rw = jnp.pad(router_w, ((0, 0), (0, LANES - N_EXPERTS)))
    rb = jnp.pad(router_b, (0, LANES - N_EXPERTS)).reshape(1, LANES)
    return pl.pallas_call(
        _outproj_router_kernel,
        grid=(t // tm,),
        in_specs=[pl.BlockSpec((tm, ATT_WIDTH), row), pl.BlockSpec((tm, MLSTM_WIDTH), row),
                  pl.BlockSpec((tm, d), row), pl.BlockSpec((ATT_WIDTH, d), fixed),
                  pl.BlockSpec((MLSTM_WIDTH, d), fixed), pl.BlockSpec((1, d), fixed),
                  pl.BlockSpec((d, LANES), fixed), pl.BlockSpec((1, LANES), fixed)],
        out_specs=[pl.BlockSpec((tm, d), row), pl.BlockSpec((tm, d), row),
                   pl.BlockSpec((tm, LANES), row), pl.BlockSpec((tm, LANES), row)],
        out_shape=[jax.ShapeDtypeStruct((t, d), F32), jax.ShapeDtypeStruct((t, d), F32),
                   jax.ShapeDtypeStruct((t, LANES), jnp.int32), jax.ShapeDtypeStruct((t, LANES), F32)],
        compiler_params=_cparams("parallel"),
        name="outproj_router",
    )(att, mem, x2, w_att, w_mem, ffn_norm_w.reshape(1, d), rw, rb)


DISPATCH_ROWS = 256


def _dispatch_kernel(fill_ref, nused_ref, dest_ref, hn_ref, x_hbm, zero_buf, sem, zsem, bsem):
    n_blocks = x_hbm.shape[0] // EXPERT_ROWS

    def token_row(j, kk):
        return pltpu.make_async_copy(hn_ref.at[pl.ds(j, 1), :],
                                     x_hbm.at[pl.ds(dest_ref[0, 0, j * TOP_K + kk], 1), :], sem)

    def pad_row(e, r):
        return pltpu.make_async_copy(zero_buf.at[pl.ds(0, 1), :], x_hbm.at[pl.ds(fill_ref[0, e] + r, 1), :], zsem)

    def pad_block(b):
        start = pl.multiple_of(b * EXPERT_ROWS, EXPERT_ROWS)
        return pltpu.make_async_copy(zero_buf, x_hbm.at[pl.ds(start, EXPERT_ROWS), :], bsem)

    def for_tokens(fn):
        def body(j, c):
            for kk in range(TOP_K):
                fn(token_row(j, kk))
            return c
        lax.fori_loop(0, DISPATCH_ROWS, body, 0)

    def for_padding(fn):
        def per_expert(e, c):
            lax.fori_loop(0, fill_ref[1, e], lambda r, cc: (fn(pad_row(e, r)), cc)[1], 0)
            return c
        lax.fori_loop(0, N_EXPERTS, per_expert, 0)
        lax.fori_loop(nused_ref[0], n_blocks, lambda b, cc: (fn(pad_block(b)), cc)[1], 0)

    for_tokens(lambda cp: cp.start())

    @pl.when(pl.program_id(0) == 0)
    def _():
        zero_buf[...] = jnp.zeros_like(zero_buf)
        for_padding(lambda cp: cp.start())
        for_padding(lambda cp: cp.wait())

    for_tokens(lambda cp: cp.wait())


def _dispatch_rows(hn, dest, fill, n_blocks_used, n_rows):
    t, d = hn.shape
    tm = DISPATCH_ROWS
    nb = t // tm
    return pl.pallas_call(
        _dispatch_kernel,
        grid_spec=pltpu.PrefetchScalarGridSpec(
            num_scalar_prefetch=2,
            grid=(nb,),
            in_specs=[pl.BlockSpec((1, 1, tm * TOP_K), lambda i, f, n: (i, 0, 0), memory_space=pltpu.SMEM),
                      pl.BlockSpec((tm, d), lambda i, f, n: (i, 0))],
            out_specs=pl.BlockSpec(memory_space=pl.ANY),
            scratch_shapes=[pltpu.VMEM((EXPERT_ROWS, d), hn.dtype), pltpu.SemaphoreType.DMA(()),
                            pltpu.SemaphoreType.DMA(()), pltpu.SemaphoreType.DMA(())],
        ),
        out_shape=jax.ShapeDtypeStruct((n_rows, d), hn.dtype),
        compiler_params=_cparams("arbitrary"),
        name="moe_dispatch",
    )(fill, n_blocks_used, dest.reshape(nb, 1, tm * TOP_K), hn)


def _expert_kernel(be_ref, nu_ref, x_ref, wgu_ref, bgu_ref, wdn_ref, bdn_ref, o_ref, wgu_s, wdn_s):
    b = pl.program_id(0)
    e = be_ref[b]
    e_prev = be_ref[jnp.maximum(b - 1, 0)]
    d_e = wdn_s.shape[0]

    @pl.when((b == 0) | (e != e_prev))
    def _():
        wgu_s[...] = wgu_ref[0].astype(BF16)
        wdn_s[...] = wdn_ref[0].astype(BF16)

    @pl.when(b < nu_ref[0])
    def _():
        xb = x_ref[...].astype(BF16)
        bgu = bgu_ref[0]
        glu = jnp.dot(xb, wgu_s[:, :d_e], preferred_element_type=F32) + bgu[:, :d_e]
        lin = jnp.dot(xb, wgu_s[:, d_e:], preferred_element_type=F32) + bgu[:, d_e:]
        glu = jnp.minimum(glu, SWIGLU_LIMIT)
        lin = jnp.clip(lin, -SWIGLU_LIMIT, SWIGLU_LIMIT)
        act = glu * (1.0 / (1.0 + jnp.exp(-SWIGLU_ALPHA * glu))) * (lin + 1.0)
        o_ref[...] = jnp.dot(act.astype(BF16), wdn_s[...], preferred_element_type=F32) + bdn_ref[0]

    @pl.when(b >= nu_ref[0])
    def _():
        o_ref[...] = jnp.zeros_like(o_ref)


def _expert_ffn(x_buf, blk_expert, n_blocks_used, w_gu, b_gu, w_dn, b_dn):
    n_rows, d = x_buf.shape
    ne, _, d2 = w_gu.shape
    d_e = w_dn.shape[1]
    nb = n_rows // EXPERT_ROWS
    last_used = lambda b, be, nu: (jnp.minimum(b, jnp.maximum(nu[0] - 1, 0)), 0)
    return pl.pallas_call(
        _expert_kernel,
        grid_spec=pltpu.PrefetchScalarGridSpec(
            num_scalar_prefetch=2,
            grid=(nb,),
            in_specs=[pl.BlockSpec((EXPERT_ROWS, d), last_used),
                      pl.BlockSpec((1, d, d2), lambda b, be, nu: (be[b], 0, 0)),
                      pl.BlockSpec((1, 1, d2), lambda b, be, nu: (be[b], 0, 0)),
                      pl.BlockSpec((1, d_e, d), lambda b, be, nu: (be[b], 0, 0)),
                      pl.BlockSpec((1, 1, d), lambda b, be, nu: (be[b], 0, 0))],
            out_specs=pl.BlockSpec((EXPERT_ROWS, d), lambda b, be, nu: (b, 0)),
            scratch_shapes=[pltpu.VMEM((d, d2), BF16), pltpu.VMEM((d_e, d), BF16)],
        ),
        out_shape=jax.ShapeDtypeStruct((n_rows, d), F32),
        compiler_params=_cparams("arbitrary"),
        name="moe_experts",
    )(blk_expert, n_blocks_used, x_buf, w_gu, b_gu.reshape(ne, 1, d2), w_dn, b_dn.reshape(ne, 1, d))


COMBINE_ROWS = 256


def _combine_kernel(pos_ref, g_ref, x1_ref, nw_ref, y_hbm, o_ref, buf, sem, *, apply_norm):
    tm = COMBINE_ROWS

    def issue(j, c):
        for kk in range(TOP_K):
            pltpu.make_async_copy(y_hbm.at[pl.ds(pos_ref[0, 0, j * TOP_K + kk], 1), :],
                                  buf.at[kk, pl.ds(j, 1), :], sem).start()
        return c
    lax.fori_loop(0, tm, issue, 0)

    def drain(j, c):
        for kk in range(TOP_K):
            pltpu.make_async_copy(y_hbm.at[pl.ds(0, 1), :], buf.at[kk, pl.ds(j, 1), :], sem).wait()
        return c
    lax.fori_loop(0, tm, drain, 0)

    g = g_ref[...]
    acc = x1_ref[...]
    for kk in range(TOP_K):
        acc = acc + buf[kk] * _lane_col(g, kk)
    if apply_norm:
        var = jnp.mean(acc * acc, axis=-1, keepdims=True)
        acc = (acc * lax.rsqrt(var + NORM_EPS)) * nw_ref[...]
    o_ref[...] = acc


def _combine(y_buf, pos, gates, x1, final_norm_w, apply_norm):
    t, d = x1.shape
    tm = COMBINE_ROWS
    nb = t // tm
    row = lambda i: (i, 0)
    return pl.pallas_call(
        functools.partial(_combine_kernel, apply_norm=apply_norm),
        grid=(nb,),
        in_specs=[pl.BlockSpec((1, 1, tm * TOP_K), lambda i: (i, 0, 0), memory_space=pltpu.SMEM),
                  pl.BlockSpec((tm, LANES), row), pl.BlockSpec((tm, d), row),
                  pl.BlockSpec((1, d), lambda i: (0, 0)), pl.BlockSpec(memory_space=pl.ANY)],
        out_specs=pl.BlockSpec((tm, d), row),
        out_shape=jax.ShapeDtypeStruct((t, d), F32),
        scratch_shapes=[pltpu.VMEM((TOP_K, tm, d), F32), pltpu.SemaphoreType.DMA(())],
        compiler_params=_cparams("arbitrary"),
        name="moe_combine",
    )(pos.reshape(nb, 1, tm * TOP_K), gates, x1, final_norm_w.reshape(1, d), y_buf)


def _routing_tables(top_idx, n_tokens):
    onehot = (top_idx[:, :, None] == jnp.arange(N_EXPERTS, dtype=jnp.int32)[None, None, :]).astype(jnp.int32)
    cnt_tok = jnp.sum(onehot, axis=1)
    excl = jnp.cumsum(cnt_tok, axis=0) - cnt_tok
    rank = jnp.take_along_axis(excl, top_idx, axis=1)
    counts = jnp.sum(cnt_tok, axis=0)
    padded = (counts + EXPERT_ROWS - 1) // EXPERT_ROWS * EXPERT_ROWS
    pad_end = jnp.cumsum(padded)
    pad_start = pad_end - padded
    dest = (pad_start[top_idx] + rank).astype(jnp.int32)
    n_rows = n_tokens * TOP_K + N_EXPERTS * EXPERT_ROWS
    n_blocks = n_rows // EXPERT_ROWS
    blk_start = jnp.arange(n_blocks, dtype=jnp.int32) * EXPERT_ROWS
    blk_expert = jnp.minimum(jnp.sum((pad_end[None, :] <= blk_start[:, None]).astype(jnp.int32), axis=1),
                             N_EXPERTS - 1).astype(jnp.int32)
    n_blocks_used = (pad_end[-1:] // EXPERT_ROWS).astype(jnp.int32)
    fill = jnp.stack([pad_start + counts, padded - counts]).astype(jnp.int32)
    return dest, fill, blk_expert, n_blocks_used, n_rows


def kernel(x, mix_norm_w, w_in, conv_w, conv_b, igate_b, fgate_b, mlstm_norm_w, w_out, ffn_norm_w,
           router_w, router_b, w_gate_up, b_gate_up, w_down, b_down, final_norm_w):
    bsz, seq, d = x.shape
    depth = w_in.shape[0]
    t = bsz * seq
    k_sel = min(TOPK_MAX, seq // 4)
    head_order = [h for s in range(Q_PER_KV) for h in (s, Q_PER_KV + s)]
    x2 = x.reshape(t, d)
    for layer in range(depth):
        p = _in_projection(x2, mix_norm_w[layer], w_in[layer], igate_b[layer], fgate_b[layer], seq)
        r3 = lambda a: a.reshape(bsz, seq, a.shape[-1])
        att = _dsa_attention(r3(p["q"]), r3(p["iq"]), r3(p["iw"]), r3(p["k"]), r3(p["v"]), r3(p["ik"]), k_sel)
        mem = _mlstm_mixer(r3(p["mq"]), r3(p["mk"]), r3(p["mv"]), r3(p["mo"]), r3(p["g"]),
                           conv_w[layer], conv_b[layer], mlstm_norm_w[layer])
        wo = w_out[layer]
        w_att = wo[:ATT_WIDTH].reshape(ATT_HEADS, HEAD_DIM, d)[jnp.asarray(head_order)].reshape(ATT_WIDTH, d)
        x1, hn, top_i, top_g = _outproj_router(
            att.reshape(t, ATT_WIDTH), mem.reshape(t, MLSTM_WIDTH), x2, w_att.astype(BF16),
            wo[ATT_WIDTH:].astype(BF16), ffn_norm_w[layer], router_w[layer], router_b[layer])
        dest, fill, blk_expert, n_blocks_used, n_rows = _routing_tables(top_i[:, :TOP_K], t)
        x_buf = _dispatch_rows(hn, dest, fill, n_blocks_used, n_rows)
        y_buf = _expert_ffn(x_buf, blk_expert, n_blocks_used, w_gate_up[layer], b_gate_up[layer],
                            w_down[layer], b_down[layer])
        x2 = _combine(y_buf, dest, top_g, x1, final_norm_w, apply_norm=(layer == depth - 1))
    return x2.reshape(bsz, seq, d)
```

```python
import functools

import jax
import jax.numpy as jnp
import numpy as np
from jax import lax
from jax.experimental import pallas as pl
from jax.experimental.pallas import tpu as pltpu

F32 = jnp.float32
BF16 = jnp.bfloat16

LANES = 128
SUBLANES = 8
VMEM_LIMIT_BYTES = 56 * 1024 * 1024

NORM_EPS = 1e-6
ROPE_THETA = 500000.0
CHUNK = 64

HEAD_DIM = 64
ATT_HEADS = 8
ATT_KV_HEADS = 2
Q_PER_KV = ATT_HEADS // ATT_KV_HEADS
ATT_ROT_DIM = HEAD_DIM // 4
IDX_HEADS = 8
IDX_DIM = 32
IDX_ROT_DIM = IDX_DIM // 4
TOPK_MAX = 256
ATT_WIDTH = ATT_HEADS * HEAD_DIM

MLSTM_HEAD_DIM = 128
MLSTM_HEADS = 4
MLSTM_WIDTH = MLSTM_HEADS * MLSTM_HEAD_DIM
CONV_WIDTH = 4

N_EXPERTS = 32
TOP_K = 4
SWIGLU_ALPHA = 1.702
SWIGLU_LIMIT = 7.0

Q_BLOCK = 128
KEY_TILE = 256
BISECT_ITERS = 32
BISECT_GROUP = 4
FOLD_ROWS = 32
TILES_PER_STEP = 2
V_ROWS = LANES + 16
MLSTM_CHUNK = 256
EXPERT_ROWS = 256
NEG_BIG = -1e30


def _cparams(*sem):
    return pltpu.CompilerParams(dimension_semantics=sem, vmem_limit_bytes=VMEM_LIMIT_BYTES)


_SEG_WIDTHS = (("q", 512), ("k", 128), ("v", 128), ("iq", 256), ("ik", 128), ("iw", 128),
               ("mq", 512), ("mk", 512), ("mv", 512), ("mo", 512), ("g", 128))
_SEG = {}
_off = 0
for _name, _w in _SEG_WIDTHS:
    _SEG[_name] = (_off, _w)
    _off += _w
PACKED_WIDTH = _off


def _lane_col(x, idx):
    lane = lax.broadcasted_iota(jnp.int32, x.shape, 1)
    return jnp.sum(jnp.where(lane == idx, x, 0.0), axis=1, keepdims=True)


def _inproj_kernel(x_ref, nw_ref, w_ref, gb_ref, ca_ref, sa1_ref, sa2_ref, ci_ref, si1_ref, si2_ref,
                   q_ref, k_ref, v_ref, iq_ref, ik_ref, iw_ref, mq_ref, mk_ref, mv_ref, mo_ref, g_ref,
                   *, idx_w_scale):
    x = x_ref[...]
    var = jnp.mean(x * x, axis=-1, keepdims=True)
    h = (x * lax.rsqrt(var + NORM_EPS)) * nw_ref[...]
    hb = h.astype(BF16)

    def proj(name):
        lo, width = _SEG[name]
        return jnp.dot(hb, w_ref[:, lo:lo + width], preferred_element_type=F32)

    def rope(p, c_ref, s1_ref, s2_ref, half):
        c, s1, s2 = c_ref[...], s1_ref[...], s2_ref[...]
        outs = []
        for j in range(p.shape[1] // LANES):
            xs = p[:, j * LANES:(j + 1) * LANES]
            outs.append(xs * c + pltpu.roll(xs, LANES - half, 1) * s1 + pltpu.roll(xs, half, 1) * s2)
        return outs[0] if len(outs) == 1 else jnp.concatenate(outs, axis=1)

    att_scale = float(HEAD_DIM ** -0.5 * np.log2(np.e))
    q_ref[...] = (rope(proj("q"), ca_ref, sa1_ref, sa2_ref, ATT_ROT_DIM // 2) * att_scale).astype(BF16)
    k_ref[...] = rope(proj("k"), ca_ref, sa1_ref, sa2_ref, ATT_ROT_DIM // 2).astype(BF16)
    v_ref[...] = proj("v").astype(BF16)
    iq_ref[...] = rope(proj("iq"), ci_ref, si1_ref, si2_ref, IDX_ROT_DIM // 2).astype(BF16)
    ik_ref[...] = rope(proj("ik"), ci_ref, si1_ref, si2_ref, IDX_ROT_DIM // 2).astype(BF16)
    iw_ref[...] = proj("iw") * idx_w_scale
    mq_ref[...] = proj("mq")
    mk_ref[...] = proj("mk")
    mv_ref[...] = proj("mv").astype(BF16)
    mo_ref[...] = proj("mo")
    g = proj("g") + gb_ref[...]
    lane = lax.broadcasted_iota(jnp.int32, g.shape, 1)
    log_f = jnp.minimum(g, 0.0) - jnp.log(1.0 + jnp.exp(-jnp.abs(g)))
    g_ref[...] = jnp.where(lane < MLSTM_HEADS, g, log_f)


def _rope_tables(seq, rot_dim, head_dim):
    pos = jnp.arange(seq, dtype=F32)
    inv_freq = ROPE_THETA ** (-jnp.arange(0, rot_dim, 2, dtype=F32) / rot_dim)
    ang = pos[:, None] * inv_freq[None, :]
    cos, sin = jnp.cos(ang), jnp.sin(ang)
    half = rot_dim // 2
    lane = np.arange(LANES) % head_dim
    fidx = lane % half
    first = lane < half
    second = (lane >= half) & (lane < rot_dim)
    cos_l, sin_l = cos[:, fidx], sin[:, fidx]
    c = jnp.where(jnp.asarray(first | second)[None, :], cos_l, 1.0)
    s1 = jnp.where(jnp.asarray(first)[None, :], -sin_l, 0.0)
    s2 = jnp.where(jnp.asarray(second)[None, :], sin_l, 0.0)
    return c, s1, s2


def _pack_w_in(w_in):
    sizes = (ATT_WIDTH, ATT_KV_HEADS * HEAD_DIM, ATT_KV_HEADS * HEAD_DIM, IDX_HEADS * IDX_DIM, IDX_DIM,
             IDX_HEADS, MLSTM_WIDTH, MLSTM_WIDTH, MLSTM_WIDTH, MLSTM_HEADS, MLSTM_HEADS, MLSTM_WIDTH)
    offs = np.concatenate([[0], np.cumsum(sizes)])
    aq, ak, av, iq, ik, iw, mq, mk, mv, mi, mf, mo = (w_in[:, offs[i]:offs[i + 1]] for i in range(12))
    d = w_in.shape[0]
    head_order = [h for s in range(Q_PER_KV) for h in (s, Q_PER_KV + s)]
    aq = aq.reshape(d, ATT_HEADS, HEAD_DIM)[:, head_order, :].reshape(d, ATT_WIDTH)
    ik_rep = jnp.tile(ik, (1, LANES // IDX_DIM))
    iw_pad = jnp.pad(iw, ((0, 0), (0, LANES - IDX_HEADS)))
    g_pad = jnp.pad(jnp.concatenate([mi, mf], axis=1), ((0, 0), (0, LANES - 2 * MLSTM_HEADS)))
    packed = jnp.concatenate([aq, ak, av, iq, ik_rep, iw_pad, mq, mk, mv, mo, g_pad], axis=1)
    assert packed.shape[1] == PACKED_WIDTH
    return packed.astype(BF16)


def _in_projection(x2, norm_w, w_in, igate_b, fgate_b, seq, tm=512):
    t, d = x2.shape
    assert t % tm == 0 and seq % tm == 0
    wp = _pack_w_in(w_in)
    gb = jnp.pad(jnp.concatenate([igate_b, fgate_b]), (0, LANES - 2 * MLSTM_HEADS)).reshape(1, LANES).astype(F32)
    ca, sa1, sa2 = _rope_tables(seq, ATT_ROT_DIM, HEAD_DIM)
    ci, si1, si2 = _rope_tables(seq, IDX_ROT_DIM, IDX_DIM)
    n_pos_blocks = seq // tm
    row = lambda i: (i, 0)
    fixed = lambda i: (0, 0)
    pos = lambda i: (i % n_pos_blocks, 0)
    tab = pl.BlockSpec((tm, LANES), pos)
    out_defs = (("q", 512, BF16), ("k", 128, BF16), ("v", 128, BF16), ("iq", 256, BF16), ("ik", 128, BF16),
                ("iw", 128, F32), ("mq", 512, F32), ("mk", 512, F32), ("mv", 512, BF16), ("mo", 512, F32),
                ("g", 128, F32))
    idx_w_scale = (IDX_HEADS ** -0.5) * (IDX_DIM ** -0.5)
    outs = pl.pallas_call(
        functools.partial(_inproj_kernel, idx_w_scale=idx_w_scale),
        grid=(t // tm,),
        in_specs=[pl.BlockSpec((tm, d), row), pl.BlockSpec((1, d), fixed),
                  pl.BlockSpec((d, PACKED_WIDTH), fixed), pl.BlockSpec((1, LANES), fixed),
                  tab, tab, tab, tab, tab, tab],
        out_specs=[pl.BlockSpec((tm, w), row) for _, w, _ in out_defs],
        out_shape=[jax.ShapeDtypeStruct((t, w), dt) for _, w, dt in out_defs],
        compiler_params=_cparams("parallel"),
        name="in_projection",
    )(x2, norm_w.reshape(1, d), wp, gb, ca, sa1, sa2, ci, si1, si2)
    return dict(zip([n for n, _, _ in out_defs], outs))


def _dsa_kernel(q_ref, iq_ref, iw_ref, k_ref, vt_ref, ik_ref, o_ref, score_ref, acc_ref, vk_ref, jcut_ref, p_ref,
                *, k_sel, index_iters):
    i = pl.program_id(1)
    n_keys = Q_BLOCK * (i + 1)
    n_steps = (n_keys + TILES_PER_STEP * KEY_TILE - 1) // (TILES_PER_STEP * KEY_TILE)

    def for_tiles(body, init, lead=None):
        def step(j, carry):
            tiles = [j * TILES_PER_STEP + u for u in range(TILES_PER_STEP)]
            opened = [lead(kt) if lead is not None else None for kt in tiles]
            for kt, head in zip(tiles, opened):
                carry = body(kt, carry) if lead is None else body(kt, head, carry)
            return carry
        return lax.fori_loop(0, n_steps, step, init)
    n_stack = ATT_HEADS * Q_BLOCK

    lane = lax.broadcasted_iota(jnp.int32, (Q_BLOCK, LANES), 1)
    iq = iq_ref[0]
    per_slab = LANES // IDX_DIM
    qi_stack = jnp.concatenate(
        [jnp.where((lane // IDX_DIM) == (h % per_slab), iq[:, (h // per_slab) * LANES:(h // per_slab + 1) * LANES], 0)
         for h in range(IDX_HEADS)], axis=0)
    q = q_ref[0]
    q_stack = jnp.concatenate(
        [jnp.where((lane // HEAD_DIM) == g, q[:, s * LANES:(s + 1) * LANES], 0)
         for s in range(Q_PER_KV) for g in range(ATT_KV_HEADS)], axis=0)
    w_t = iw_ref[0].T

    q_lane = lax.broadcasted_iota(jnp.int32, (1, Q_BLOCK), 1)
    key_limit = (2 * i + 1 + (q_lane >= CHUNK).astype(jnp.int32)) * CHUNK
    k_eff = jnp.minimum(key_limit, k_sel).astype(F32)
    key_iota = lax.broadcasted_iota(jnp.int32, (KEY_TILE, Q_BLOCK), 0)
    nt_dims = (((1,), (1,)), ((), ()))

    def idx_lead(kt):
        start = pl.multiple_of(kt * KEY_TILE, KEY_TILE)
        return lax.dot_general(ik_ref[0, pl.ds(start, KEY_TILE), :], qi_stack, nt_dims,
                               preferred_element_type=F32)

    def idx_body(kt, z, carry):
        rmin, rmax = carry
        start = pl.multiple_of(kt * KEY_TILE, KEY_TILE)
        sc = jnp.zeros((KEY_TILE, Q_BLOCK), F32)
        for h in range(IDX_HEADS):
            sc = sc + jnp.maximum(z[:, h * Q_BLOCK:(h + 1) * Q_BLOCK], 0.0) * w_t[h:h + 1, :]
        adm = (key_iota + start) < key_limit
        score_ref[pl.ds(start, KEY_TILE), :] = jnp.where(adm, sc, -jnp.inf)
        rmin = jnp.minimum(rmin, jnp.min(jnp.where(adm, sc, jnp.inf), axis=0, keepdims=True))
        rmax = jnp.maximum(rmax, jnp.max(jnp.where(adm, sc, -jnp.inf), axis=0, keepdims=True))
        return rmin, rmax

    init = (jnp.full((1, Q_BLOCK), jnp.inf, F32), jnp.full((1, Q_BLOCK), -jnp.inf, F32))
    lo, hi = for_tiles(idx_body, init, lead=idx_lead)

    def fold(x):
        return x.reshape(KEY_TILE // FOLD_ROWS, FOLD_ROWS, Q_BLOCK)

    def count(pred):
        def body(kt, acc):
            start = pl.multiple_of(kt * KEY_TILE, KEY_TILE)
            hit = pred(score_ref[pl.ds(start, KEY_TILE), :], key_iota + start).astype(F32)
            return acc + jnp.sum(fold(hit), axis=0)
        acc = for_tiles(body, jnp.zeros((FOLD_ROWS, Q_BLOCK), F32))
        return jnp.sum(acc, axis=0, keepdims=True)

    def unresolved(c_lo):
        return jnp.max(c_lo - k_eff) > 0.0

    def bisect(carry):
        it, lo, hi, c_lo = carry
        for _ in range(BISECT_GROUP):
            mid = 0.5 * (lo + hi)
            c_mid = count(lambda x, _: x >= mid)
            ok = c_mid >= k_eff
            lo, hi, c_lo = jnp.where(ok, mid, lo), jnp.where(ok, hi, mid), jnp.where(ok, c_mid, c_lo)
        return it + BISECT_GROUP, lo, hi, c_lo

    _, lo, _, c_lo = lax.while_loop(lambda c: (c[0] < BISECT_ITERS) & unresolved(c[3]), bisect,
                                    (jnp.int32(0), lo, hi, key_limit.astype(F32)))

    vk_ref[...] = jnp.broadcast_to(lo, vk_ref.shape)
    jcut_ref[...] = jnp.full(jcut_ref.shape, jnp.iinfo(jnp.int32).max, jnp.int32)

    @pl.when(unresolved(c_lo))
    def _():
        def min_body(kt, acc):
            start = pl.multiple_of(kt * KEY_TILE, KEY_TILE)
            x = score_ref[pl.ds(start, KEY_TILE), :]
            return jnp.minimum(acc, jnp.min(fold(jnp.where(x >= lo, x, jnp.inf)), axis=0))
        vk = jnp.min(for_tiles(min_body, jnp.full((FOLD_ROWS, Q_BLOCK), jnp.inf, F32)),
                     axis=0, keepdims=True)
        need = k_eff - count(lambda x, _: x > vk)

        def jbisect(_, carry):
            jlo, jhi = carry
            jmid = jnp.right_shift(jlo + jhi, 1)
            ok = count(lambda x, kidx: (x == vk) & (kidx < jmid)) >= need
            return jnp.where(ok, jlo, jmid), jnp.where(ok, jmid, jhi)

        j0 = (jnp.zeros((1, Q_BLOCK), jnp.int32), jnp.full((1, Q_BLOCK), n_keys, jnp.int32))
        _, jcut = lax.fori_loop(0, index_iters, jbisect, j0)
        vk_ref[...] = jnp.broadcast_to(vk, vk_ref.shape)
        jcut_ref[...] = jnp.broadcast_to(jcut, jcut_ref.shape)

    vk = vk_ref[0:1, :]
    jcut = jcut_ref[0:1, :]

    r_i = lax.broadcasted_iota(jnp.int32, (Q_BLOCK, LANES), 0)
    eye = jnp.where(r_i == lane, 1.0, 0.0).astype(BF16)
    q_aug = jnp.concatenate([q_stack, jnp.concatenate([eye] * ATT_HEADS, axis=0)], axis=1)
    acc_ref[...] = jnp.zeros_like(acc_ref)

    def att_lead(kt):
        start = pl.multiple_of(kt * KEY_TILE, KEY_TILE)
        x = score_ref[pl.ds(start, KEY_TILE), :]
        sel = (x > vk) | ((x == vk) & ((key_iota + start) < jcut))
        bias = jnp.where(sel, 0.0, NEG_BIG).astype(BF16)
        k_aug = jnp.concatenate([k_ref[0, pl.ds(start, KEY_TILE), :], bias], axis=1)
        return lax.dot_general(k_aug, q_aug, nt_dims, preferred_element_type=F32)

    def accumulate(kt, alpha):
        acc_ref[...] = alpha * acc_ref[...] + jnp.dot(vt_ref[0, kt], p_ref[...], preferred_element_type=F32)

    def att_body(kt, s, carry):
        m, alpha = carry
        accumulate(jnp.maximum(kt - 1, 0), alpha)
        m_new = jnp.maximum(m, jnp.max(s, axis=0, keepdims=True))
        p_ref[...] = jnp.exp2(s - m_new).astype(BF16)
        return m_new, jnp.exp2(m - m_new)

    p_ref[...] = jnp.zeros_like(p_ref)
    init = (jnp.full((1, n_stack), NEG_BIG, F32), jnp.ones((1, n_stack), F32))
    _, alpha = for_tiles(att_body, init, lead=att_lead)
    accumulate(TILES_PER_STEP * n_steps - 1, alpha)

    out_t = acc_ref[0:LANES, :] / acc_ref[LANES:LANES + 1, :]
    for s in range(Q_PER_KV):
        o0 = out_t[:, (2 * s) * Q_BLOCK:(2 * s + 1) * Q_BLOCK].T
        o1 = out_t[:, (2 * s + 1) * Q_BLOCK:(2 * s + 2) * Q_BLOCK].T
        o_ref[0, :, s * LANES:(s + 1) * LANES] = jnp.where(lane < HEAD_DIM, o0, o1).astype(o_ref.dtype)


def _dsa_attention(q, iq, iw, k, v, ik, k_sel):
    bsz, seq, _ = q.shape
    assert seq % (TILES_PER_STEP * KEY_TILE) == 0 and seq % Q_BLOCK == 0
    n_kt = seq // KEY_TILE
    vt = v.reshape(bsz, n_kt, KEY_TILE, LANES).transpose(0, 1, 3, 2)
    ones_rows = jnp.zeros((bsz, n_kt, V_ROWS - LANES, KEY_TILE), v.dtype).at[:, :, 0, :].set(1.0)
    vt = jnp.concatenate([vt, ones_rows], axis=2)
    blk = lambda w: pl.BlockSpec((1, Q_BLOCK, w), lambda b, i: (b, i, 0))
    full = pl.BlockSpec((1, seq, LANES), lambda b, i: (b, 0, 0))
    return pl.pallas_call(
        functools.partial(_dsa_kernel, k_sel=k_sel, index_iters=seq.bit_length()),
        grid=(bsz, seq // Q_BLOCK),
        in_specs=[blk(ATT_WIDTH), blk(IDX_HEADS * IDX_DIM), blk(LANES), full,
                  pl.BlockSpec((1, n_kt, V_ROWS, KEY_TILE), lambda b, i: (b, 0, 0, 0)), full],
        out_specs=blk(ATT_WIDTH),
        out_shape=jax.ShapeDtypeStruct((bsz, seq, ATT_WIDTH), BF16),
        scratch_shapes=[pltpu.VMEM((seq, Q_BLOCK), F32), pltpu.VMEM((V_ROWS, ATT_HEADS * Q_BLOCK), F32),
                        pltpu.VMEM((SUBLANES, Q_BLOCK), F32), pltpu.VMEM((SUBLANES, Q_BLOCK), jnp.int32),
                        pltpu.VMEM((KEY_TILE, ATT_HEADS * Q_BLOCK), BF16)],
        compiler_params=_cparams("parallel", "parallel"),
        name="dsa_attention",
    )(q, iq, iw, k, vt, ik)


def _mlstm_kernel(mq_ref, mk_ref, mv_ref, mo_ref, gc_ref, gr_ref, cw_ref, cb_ref, nw_ref, o_ref,
                  c_ref, n_ref, m_ref, prev_ref):
    L = MLSTM_CHUNK
    W = MLSTM_WIDTH
    dh = MLSTM_HEAD_DIM

    @pl.when(pl.program_id(1) == 0)
    def _():
        c_ref[...] = jnp.zeros_like(c_ref)
        n_ref[...] = jnp.zeros_like(n_ref)
        m_ref[...] = jnp.zeros_like(m_ref)
        prev_ref[...] = jnp.zeros_like(prev_ref)

    row = lax.broadcasted_iota(jnp.int32, (L, W), 0)

    def conv_silu(x, prev8, w4, b):
        y = x * w4[CONV_WIDTH - 1:CONV_WIDTH, :] + b
        for d in range(1, CONV_WIDTH):
            xr = pltpu.roll(x, d, 0)
            pr = pltpu.roll(prev8, d, 0)
            pr_full = jnp.broadcast_to(pr[None], (L // SUBLANES, SUBLANES, W)).reshape(L, W)
            y = y + jnp.where(row < d, pr_full, xr) * w4[CONV_WIDTH - 1 - d:CONV_WIDTH - d, :]
        return y / (1.0 + jnp.exp(-y))

    xq, xk = mq_ref[0], mk_ref[0]
    cw, cb = cw_ref[...], cb_ref[...]
    q = conv_silu(xq, prev_ref[:, :W], cw[:, :W], cb[:, :W])
    k = conv_silu(xk, prev_ref[:, W:], cw[:, W:], cb[:, W:]) * (dh ** -0.5)
    prev_ref[:, :W] = xq[L - SUBLANES:, :]
    prev_ref[:, W:] = xk[L - SUBLANES:, :]

    g_col = gc_ref[0]
    g_row = gr_ref[0]
    r_i = lax.broadcasted_iota(jnp.int32, (L, L), 0)
    c_i = lax.broadcasted_iota(jnp.int32, (L, L), 1)
    causal = r_i >= c_i
    tri = causal.astype(F32)
    b_col_all = jnp.dot(tri, g_col, preferred_element_type=F32, precision=lax.Precision.HIGHEST)
    b_row_all = jnp.dot(g_row, (r_i <= c_i).astype(F32), preferred_element_type=F32,
                        precision=lax.Precision.HIGHEST)

    v = mv_ref[0]
    mo = mo_ref[0]
    nw = nw_ref[...]
    for h in range(MLSTM_HEADS):
        hs = slice(h * dh, (h + 1) * dh)
        i_col = _lane_col(g_col, h)
        i_row = g_row[h:h + 1, :]
        b_col = _lane_col(b_col_all, MLSTM_HEADS + h)
        b_row = b_row_all[MLSTM_HEADS + h:MLSTM_HEADS + h + 1, :]
        b_last = jnp.sum(g_row[MLSTM_HEADS + h:MLSTM_HEADS + h + 1, :], axis=1, keepdims=True)
        m_prev = m_ref[h:h + 1, 0:1]
        c_mat = c_ref[h]
        n_vec = n_ref[h:h + 1, :]

        d_log = jnp.where(causal, b_col + (i_row - b_row), -jnp.inf)
        inter_log = b_col + m_prev
        m_t = jnp.maximum(inter_log, jnp.max(d_log, axis=1, keepdims=True))
        d_w = jnp.exp(d_log - m_t)
        inter_w = jnp.exp(inter_log - m_t)
        qh, kh, vh = q[:, hs], k[:, hs], v[:, hs]
        qb, kb = qh.astype(BF16), kh.astype(BF16)
        s = lax.dot_general(qb, kb, (((1,), (1,)), ((), ())), preferred_element_type=F32) * d_w
        num = (jnp.dot(s.astype(BF16), vh, preferred_element_type=F32)
               + inter_w * jnp.dot(qb, c_mat.astype(BF16), preferred_element_type=F32))
        den = jnp.sum(s, axis=1, keepdims=True) + inter_w * jnp.sum(qh * n_vec, axis=1, keepdims=True)
        hout = num / jnp.maximum(jnp.abs(den), jnp.exp(-m_t))

        w_log = b_last - b_col + i_col
        m_new = jnp.maximum(b_last + m_prev, jnp.max(w_log, axis=0, keepdims=True))
        decay = jnp.exp(b_last + m_prev - m_new)
        kw = kh * jnp.exp(w_log - m_new)
        c_ref[h] = decay * c_mat + jnp.dot(kw.T.astype(BF16), vh, preferred_element_type=F32)
        n_ref[h:h + 1, :] = decay * n_vec + jnp.sum(kw, axis=0, keepdims=True)
        m_ref[h:h + 1, :] = jnp.broadcast_to(m_new, (1, LANES))

        hn = hout * lax.rsqrt(jnp.mean(hout * hout, axis=-1, keepdims=True) + NORM_EPS)
        gate = 1.0 / (1.0 + jnp.exp(-mo[:, hs]))
        o_ref[0, :, hs] = (hn * nw[:, hs] * gate).astype(o_ref.dtype)


def _mlstm_mixer(mq, mk, mv, mo, g, conv_w, conv_b, norm_w):
    bsz, seq, w = mq.shape
    L = MLSTM_CHUNK
    assert seq % L == 0
    g_row = g[:, :, :SUBLANES].transpose(0, 2, 1)
    blk = lambda width: pl.BlockSpec((1, L, width), lambda b, c: (b, c, 0))
    fixed = lambda shape: pl.BlockSpec(shape, lambda b, c: (0, 0))
    return pl.pallas_call(
        _mlstm_kernel,
        grid=(bsz, seq // L),
        in_specs=[blk(w), blk(w), blk(w), blk(w), blk(LANES),
                  pl.BlockSpec((1, SUBLANES, L), lambda b, c: (b, 0, c)),
                  fixed((CONV_WIDTH, 2 * w)), fixed((1, 2 * w)), fixed((1, w))],
        out_specs=blk(w),
        out_shape=jax.ShapeDtypeStruct((bsz, seq, w), BF16),
        scratch_shapes=[pltpu.VMEM((MLSTM_HEADS, MLSTM_HEAD_DIM, MLSTM_HEAD_DIM), F32),
                        pltpu.VMEM((SUBLANES, MLSTM_HEAD_DIM), F32),
                        pltpu.VMEM((SUBLANES, LANES), F32),
                        pltpu.VMEM((SUBLANES, 2 * w), F32)],
        compiler_params=_cparams("parallel", "arbitrary"),
        name="mlstm_mixer",
    )(mq, mk, mv, mo, g, g_row, conv_w, conv_b.reshape(1, 2 * w), norm_w.reshape(1, w))


def _outproj_router_kernel(att_ref, mem_ref, x_ref, wa_ref, wm_ref, nw_ref, rw_ref, rb_ref,
                           x1_ref, hn_ref, ti_ref, tg_ref):
    y = (jnp.dot(att_ref[...], wa_ref[...], preferred_element_type=F32)
         + jnp.dot(mem_ref[...], wm_ref[...], preferred_element_type=F32))
    x1 = x_ref[...] + y
    x1_ref[...] = x1
    var = jnp.mean(x1 * x1, axis=-1, keepdims=True)
    hn = (x1 * lax.rsqrt(var + NORM_EPS)) * nw_ref[...]
    hn_ref[...] = hn
    hn_hi = hn.astype(BF16)
    hn_lo = (hn - hn_hi.astype(F32)).astype(BF16)
    rw = rw_ref[...]
    rw_hi = rw.astype(BF16)
    rw_lo = (rw - rw_hi.astype(F32)).astype(BF16)
    part = jnp.dot(hn_hi, jnp.concatenate([rw_hi, rw_lo], axis=1), preferred_element_type=F32)
    logits = (part[:, :LANES] + part[:, LANES:]
              + jnp.dot(hn_lo, rw_hi, preferred_element_type=F32) + rb_ref[...])
    lane_i = lax.broadcasted_iota(jnp.int32, logits.shape, 1)
    lane = lane_i.astype(F32)
    cur = jnp.where(lane_i < N_EXPERTS, logits, -jnp.inf)
    vals, idxs = [], []
    for _ in range(TOP_K):
        mx = jnp.max(cur, axis=1, keepdims=True)
        am = jnp.min(jnp.where(cur == mx, lane, float(LANES)), axis=1, keepdims=True)
        vals.append(mx)
        idxs.append(am)
        cur = jnp.where(lane == am, -jnp.inf, cur)
    exps = [jnp.exp(vk - vals[0]) for vk in vals]
    denom = exps[0]
    for e in exps[1:]:
        denom = denom + e
    ti = jnp.zeros(logits.shape, F32)
    tg = jnp.zeros(logits.shape, F32)
    for kk in range(TOP_K):
        ti = jnp.where(lane_i == kk, idxs[kk], ti)
        tg = jnp.where(lane_i == kk, exps[kk] / denom, tg)
    ti_ref[...] = ti.astype(jnp.int32)
    tg_ref[...] = tg


def _outproj_router(att, mem, x2, w_att, w_mem, ffn_norm_w, router_w, router_b, tm=256):
    t, d = x2.shape
    row = lambda i: (i, 0)
    fixed = lambda i: (0, 0)
    rw = jnp.pad(router_w, ((0, 0), (0, LANES - N_EXPERTS)))
    rb = jnp.pad(router_b, (0, LANES - N_EXPERTS)).reshape(1, LANES)
    return pl.pallas_call(
        _outproj_router_kernel,
        grid=(t // tm,),
        in_specs=[pl.BlockSpec((tm, ATT_WIDTH), row), pl.BlockSpec((tm, MLSTM_WIDTH), row),
                  pl.BlockSpec((tm, d), row), pl.BlockSpec((ATT_WIDTH, d), fixed),
                  pl.BlockSpec((MLSTM_WIDTH, d), fixed), pl.BlockSpec((1, d), fixed),
                  pl.BlockSpec((d, LANES), fixed), pl.BlockSpec((1, LANES), fixed)],
        out_specs=[pl.BlockSpec((tm, d), row), pl.BlockSpec((tm, d), row),
                   pl.BlockSpec((tm, LANES), row), pl.BlockSpec((tm, LANES), row)],
        out_shape=[jax.ShapeDtypeStruct((t, d), F32), jax.ShapeDtypeStruct((t, d), F32),
                   jax.ShapeDtypeStruct((t, LANES), jnp.int32), jax.ShapeDtypeStruct((t, LANES), F32)],
        compiler_params=_cparams("parallel"),
        name="outproj_router",
    )(att, mem, x2, w_att, w_mem, ffn_norm_w.reshape(1, d), rw, rb)


DISPATCH_ROWS = 256


def _dispatch_kernel(fill_ref, nused_ref, dest_ref, hn_ref, x_hbm, stage, zero_buf, sem, zsem, bsem):
    n_blocks = x_hbm.shape[0] // EXPERT_ROWS
    step = pl.program_id(0)
    slot = step % 2
    stage[slot] = hn_ref[...]

    def token_row(j, kk):
        return pltpu.make_async_copy(stage.at[slot, pl.ds(j, 1), :],
                                     x_hbm.at[pl.ds(dest_ref[0, 0, j * TOP_K + kk], 1), :], sem.at[slot])

    def wait_tokens(which):
        all_rows = x_hbm.at[pl.ds(0, TOP_K * DISPATCH_ROWS), :]
        pltpu.make_async_copy(all_rows, all_rows, sem.at[which]).wait()

    def pad_row(e, r):
        return pltpu.make_async_copy(zero_buf.at[pl.ds(0, 1), :], x_hbm.at[pl.ds(fill_ref[0, e] + r, 1), :], zsem)

    def pad_block(b):
        start = pl.multiple_of(b * EXPERT_ROWS, EXPERT_ROWS)
        return pltpu.make_async_copy(zero_buf, x_hbm.at[pl.ds(start, EXPERT_ROWS), :], bsem)

    def start_tokens(j, c):
        for kk in range(TOP_K):
            token_row(j, kk).start()
        return c

    def for_padding(fn):
        def per_expert(e, c):
            lax.fori_loop(0, fill_ref[1, e], lambda r, cc: (fn(pad_row(e, r)), cc)[1], 0)
            return c
        lax.fori_loop(0, N_EXPERTS, per_expert, 0)
        lax.fori_loop(nused_ref[0], n_blocks, lambda b, cc: (fn(pad_block(b)), cc)[1], 0)

    lax.fori_loop(0, DISPATCH_ROWS, start_tokens, 0)

    @pl.when(pl.program_id(0) == 0)
    def _():
        zero_buf[...] = jnp.zeros_like(zero_buf)
        for_padding(lambda cp: cp.start())
        for_padding(lambda cp: cp.wait())

    @pl.when(step > 0)
    def _():
        wait_tokens(1 - slot)

    @pl.when(step == pl.num_programs(0) - 1)
    def _():
        wait_tokens(slot)


def _dispatch_rows(hn, dest, fill, n_blocks_used, n_rows):
    t, d = hn.shape
    tm = DISPATCH_ROWS
    nb = t // tm
    return pl.pallas_call(
        _dispatch_kernel,
        grid_spec=pltpu.PrefetchScalarGridSpec(
            num_scalar_prefetch=2,
            grid=(nb,),
            in_specs=[pl.BlockSpec((1, 1, tm * TOP_K), lambda i, f, n: (i, 0, 0), memory_space=pltpu.SMEM),
                      pl.BlockSpec((tm, d), lambda i, f, n: (i, 0))],
            out_specs=pl.BlockSpec(memory_space=pl.ANY),
            scratch_shapes=[pltpu.VMEM((2, tm, d), hn.dtype), pltpu.VMEM((EXPERT_ROWS, d), hn.dtype),
                            pltpu.SemaphoreType.DMA((2,)), pltpu.SemaphoreType.DMA(()),
                            pltpu.SemaphoreType.DMA(())],
        ),
        out_shape=jax.ShapeDtypeStruct((n_rows, d), hn.dtype),
        compiler_params=_cparams("arbitrary"),
        name="moe_dispatch",
    )(fill, n_blocks_used, dest.reshape(nb, 1, tm * TOP_K), hn)


def _expert_kernel(be_ref, nu_ref, nxt_ref, par_ref, x_ref, bgu_ref, bdn_ref, wgu_hbm, wdn_hbm, o_ref,
                   wgu_f, wdn_f, wgu_s, wdn_s, sem):
    b = pl.program_id(0)
    e = be_ref[b]
    e_prev = be_ref[jnp.maximum(b - 1, 0)]
    slot = par_ref[b]
    used = b < nu_ref[0]
    d_e = wdn_s.shape[0]

    def fetch(expert, into):
        return (pltpu.make_async_copy(wgu_hbm.at[expert], wgu_f.at[into], sem.at[0, into]),
                pltpu.make_async_copy(wdn_hbm.at[expert], wdn_f.at[into], sem.at[1, into]))

    @pl.when(b == 0)
    def _():
        for cp in fetch(e, slot):
            cp.start()

    @pl.when(used & ((b == 0) | (e != e_prev)))
    def _():
        for cp in fetch(e, slot):
            cp.wait()

        @pl.when(nxt_ref[b] >= 0)
        def _():
            for cp in fetch(nxt_ref[b], 1 - slot):
                cp.start()

        wgu_s[...] = wgu_f[slot].astype(BF16)
        wdn_s[...] = wdn_f[slot].astype(BF16)

    @pl.when(used)
    def _():
        xb = x_ref[...].astype(BF16)
        bgu = bgu_ref[0]
        glu = jnp.dot(xb, wgu_s[:, :d_e], preferred_element_type=F32) + bgu[:, :d_e]
        lin = jnp.dot(xb, wgu_s[:, d_e:], preferred_element_type=F32) + bgu[:, d_e:]
        glu = jnp.minimum(glu, SWIGLU_LIMIT)
        lin = jnp.clip(lin, -SWIGLU_LIMIT, SWIGLU_LIMIT)
        act = glu * (1.0 / (1.0 + jnp.exp(-SWIGLU_ALPHA * glu))) * (lin + 1.0)
        o_ref[...] = jnp.dot(act.astype(BF16), wdn_s[...], preferred_element_type=F32) + bdn_ref[0]

    @pl.when(jnp.logical_not(used))
    def _():
        o_ref[...] = jnp.zeros_like(o_ref)


def _expert_ffn(x_buf, blk_expert, n_blocks_used, next_expert, run_parity, w_gu, b_gu, w_dn, b_dn):
    n_rows, d = x_buf.shape
    ne, _, d2 = w_gu.shape
    d_e = w_dn.shape[1]
    nb = n_rows // EXPERT_ROWS
    last_used = lambda b, be, nu, nx, pr: (jnp.minimum(b, jnp.maximum(nu[0] - 1, 0)), 0)
    by_expert = lambda b, be, nu, nx, pr: (be[b], 0, 0)
    return pl.pallas_call(
        _expert_kernel,
        grid_spec=pltpu.PrefetchScalarGridSpec(
            num_scalar_prefetch=4,
            grid=(nb,),
            in_specs=[pl.BlockSpec((EXPERT_ROWS, d), last_used),
                      pl.BlockSpec((1, 1, d2), by_expert),
                      pl.BlockSpec((1, 1, d), by_expert),
                      pl.BlockSpec(memory_space=pl.ANY),
                      pl.BlockSpec(memory_space=pl.ANY)],
            out_specs=pl.BlockSpec((EXPERT_ROWS, d), lambda b, be, nu, nx, pr: (b, 0)),
            scratch_shapes=[pltpu.VMEM((2, d, d2), w_gu.dtype), pltpu.VMEM((2, d_e, d), w_dn.dtype),
                            pltpu.VMEM((d, d2), BF16), pltpu.VMEM((d_e, d), BF16),
                            pltpu.SemaphoreType.DMA((2, 2))],
        ),
        out_shape=jax.ShapeDtypeStruct((n_rows, d), F32),
        compiler_params=_cparams("arbitrary"),
        name="moe_experts",
    )(blk_expert, n_blocks_used, next_expert, run_parity, x_buf, b_gu.reshape(ne, 1, d2), b_dn.reshape(ne, 1, d),
      w_gu, w_dn)


COMBINE_ROWS = 256


def _combine_kernel(pos_ref, pos_next_ref, g_ref, x1_ref, nw_ref, y_hbm, o_ref, buf, sem, *, apply_norm):
    tm = COMBINE_ROWS
    step = pl.program_id(0)
    slot = step % 2

    def gather(p_ref, into):
        def issue(j, c):
            for kk in range(TOP_K):
                pltpu.make_async_copy(y_hbm.at[pl.ds(p_ref[0, 0, j * TOP_K + kk], 1), :],
                                      buf.at[into, pl.ds(kk * tm + j, 1), :], sem.at[into]).start()
            return c
        lax.fori_loop(0, tm, issue, 0)

    @pl.when(step == 0)
    def _():
        gather(pos_ref, 0)

    @pl.when(step + 1 < pl.num_programs(0))
    def _():
        gather(pos_next_ref, 1 - slot)

    pltpu.make_async_copy(y_hbm.at[pl.ds(0, TOP_K * tm), :], buf.at[slot], sem.at[slot]).wait()

    g = g_ref[...]
    acc = x1_ref[...]
    for kk in range(TOP_K):
        acc = acc + buf[slot, pl.ds(kk * tm, tm), :] * _lane_col(g, kk)
    if apply_norm:
        var = jnp.mean(acc * acc, axis=-1, keepdims=True)
        acc = (acc * lax.rsqrt(var + NORM_EPS)) * nw_ref[...]
    o_ref[...] = acc


def _combine(y_buf, pos, gates, x1, final_norm_w, apply_norm):
    t, d = x1.shape
    tm = COMBINE_ROWS
    nb = t // tm
    row = lambda i: (i, 0)
    pos3 = pos.reshape(nb, 1, tm * TOP_K)
    return pl.pallas_call(
        functools.partial(_combine_kernel, apply_norm=apply_norm),
        grid=(nb,),
        in_specs=[pl.BlockSpec((1, 1, tm * TOP_K), lambda i: (i, 0, 0), memory_space=pltpu.SMEM),
                  pl.BlockSpec((1, 1, tm * TOP_K), lambda i: (jnp.minimum(i + 1, nb - 1), 0, 0),
                               memory_space=pltpu.SMEM),
                  pl.BlockSpec((tm, LANES), row), pl.BlockSpec((tm, d), row),
                  pl.BlockSpec((1, d), lambda i: (0, 0)), pl.BlockSpec(memory_space=pl.ANY)],
        out_specs=pl.BlockSpec((tm, d), row),
        out_shape=jax.ShapeDtypeStruct((t, d), F32),
        scratch_shapes=[pltpu.VMEM((2, TOP_K * tm, d), F32), pltpu.SemaphoreType.DMA((2,))],
        compiler_params=_cparams("arbitrary"),
        name="moe_combine",
    )(pos3, pos3, gates, x1, final_norm_w.reshape(1, d), y_buf)


def _routing_tables(top_idx, n_tokens):
    onehot = (top_idx[:, :, None] == jnp.arange(N_EXPERTS, dtype=jnp.int32)[None, None, :]).astype(jnp.int32)
    cnt_tok = jnp.sum(onehot, axis=1)
    excl = jnp.cumsum(cnt_tok, axis=0) - cnt_tok
    rank = jnp.take_along_axis(excl, top_idx, axis=1)
    counts = jnp.sum(cnt_tok, axis=0)
    padded = (counts + EXPERT_ROWS - 1) // EXPERT_ROWS * EXPERT_ROWS
    pad_end = jnp.cumsum(padded)
    pad_start = pad_end - padded
    dest = (pad_start[top_idx] + rank).astype(jnp.int32)
    n_rows = n_tokens * TOP_K + N_EXPERTS * EXPERT_ROWS
    n_blocks = n_rows // EXPERT_ROWS
    blk_start = jnp.arange(n_blocks, dtype=jnp.int32) * EXPERT_ROWS
    blk_expert = jnp.minimum(jnp.sum((pad_end[None, :] <= blk_start[:, None]).astype(jnp.int32), axis=1),
                             N_EXPERTS - 1).astype(jnp.int32)
    n_blocks_used = (pad_end[-1:] // EXPERT_ROWS).astype(jnp.int32)
    fill = jnp.stack([pad_start + counts, padded - counts]).astype(jnp.int32)
    first = jnp.concatenate([jnp.ones((1,), jnp.bool_), blk_expert[1:] != blk_expert[:-1]])
    run_parity = ((jnp.cumsum(first.astype(jnp.int32)) - 1) % 2).astype(jnp.int32)
    next_run = (pad_end[blk_expert] // EXPERT_ROWS).astype(jnp.int32)
    next_expert = jnp.where(next_run < n_blocks_used[0],
                            blk_expert[jnp.minimum(next_run, n_blocks - 1)], -1).astype(jnp.int32)
    return dest, fill, blk_expert, n_blocks_used, next_expert, run_parity, n_rows


def kernel(x, mix_norm_w, w_in, conv_w, conv_b, igate_b, fgate_b, mlstm_norm_w, w_out, ffn_norm_w,
           router_w, router_b, w_gate_up, b_gate_up, w_down, b_down, final_norm_w):
    bsz, seq, d = x.shape
    depth = w_in.shape[0]
    t = bsz * seq
    k_sel = min(TOPK_MAX, seq // 4)
    head_order = [h for s in range(Q_PER_KV) for h in (s, Q_PER_KV + s)]
    x2 = x.reshape(t, d)
    for layer in range(depth):
        p = _in_projection(x2, mix_norm_w[layer], w_in[layer], igate_b[layer], fgate_b[layer], seq)
        r3 = lambda a: a.reshape(bsz, seq, a.shape[-1])
        att = _dsa_attention(r3(p["q"]), r3(p["iq"]), r3(p["iw"]), r3(p["k"]), r3(p["v"]), r3(p["ik"]), k_sel)
        mem = _mlstm_mixer(r3(p["mq"]), r3(p["mk"]), r3(p["mv"]), r3(p["mo"]), r3(p["g"]),
                           conv_w[layer], conv_b[layer], mlstm_norm_w[layer])
        wo = w_out[layer]
        w_att = wo[:ATT_WIDTH].reshape(ATT_HEADS, HEAD_DIM, d)[jnp.asarray(head_order)].reshape(ATT_WIDTH, d)
        x1, hn, top_i, top_g = _outproj_router(
            att.reshape(t, ATT_WIDTH), mem.reshape(t, MLSTM_WIDTH), x2, w_att.astype(BF16),
            wo[ATT_WIDTH:].astype(BF16), ffn_norm_w[layer], router_w[layer], router_b[layer])
        dest, fill, blk_expert, n_blocks_used, next_expert, run_parity, n_rows = _routing_tables(
            top_i[:, :TOP_K], t)
        x_buf = _dispatch_rows(hn, dest, fill, n_blocks_used, n_rows)
        y_buf = _expert_ffn(x_buf, blk_expert, n_blocks_used, next_expert, run_parity, w_gate_up[layer],
                            b_gate_up[layer], w_down[layer], b_down[layer])
        x2 = _combine(y_buf, dest, top_g, x1, final_norm_w, apply_norm=(layer == depth - 1))
    return x2.reshape(bsz, seq, d)
```

```python
import functools

import jax
import jax.numpy as jnp
import numpy as np
from jax import lax
from jax.experimental import pallas as pl
from jax.experimental.pallas import tpu as pltpu

F32 = jnp.float32
BF16 = jnp.bfloat16

LANES = 128
SUBLANES = 8
VMEM_LIMIT_BYTES = 56 * 1024 * 1024

NORM_EPS = 1e-6
ROPE_THETA = 500000.0
CHUNK = 64

HEAD_DIM = 64
ATT_HEADS = 8
ATT_KV_HEADS = 2
Q_PER_KV = ATT_HEADS // ATT_KV_HEADS
ATT_ROT_DIM = HEAD_DIM // 4
IDX_HEADS = 8
IDX_DIM = 32
IDX_ROT_DIM = IDX_DIM // 4
TOPK_MAX = 256
ATT_WIDTH = ATT_HEADS * HEAD_DIM

MLSTM_HEAD_DIM = 128
MLSTM_HEADS = 4
MLSTM_WIDTH = MLSTM_HEADS * MLSTM_HEAD_DIM
CONV_WIDTH = 4

N_EXPERTS = 32
TOP_K = 4
SWIGLU_ALPHA = 1.702
SWIGLU_LIMIT = 7.0

Q_BLOCK = 128
KEY_TILE = 256
BISECT_ITERS = 32
BISECT_GROUP = 4
TIE_CHECK_FROM = 20
FOLD_ROWS = 32
TILES_PER_STEP = 2
V_ROWS = LANES + 16
MLSTM_CHUNK = 256
EXPERT_ROWS = 256
NEG_BIG = -1e30


def _cparams(*sem):
    return pltpu.CompilerParams(dimension_semantics=sem, vmem_limit_bytes=VMEM_LIMIT_BYTES)


_SEG_WIDTHS = (("q", 512), ("k", 128), ("v", 128), ("iq", 256), ("ik", 128), ("iw", 128),
               ("mq", 512), ("mk", 512), ("mv", 512), ("mo", 512), ("g", 128))
_SEG = {}
_off = 0
for _name, _w in _SEG_WIDTHS:
    _SEG[_name] = (_off, _w)
    _off += _w
PACKED_WIDTH = _off


def _lane_col(x, idx):
    lane = lax.broadcasted_iota(jnp.int32, x.shape, 1)
    return jnp.sum(jnp.where(lane == idx, x, 0.0), axis=1, keepdims=True)


def _inproj_kernel(x_ref, nw_ref, w_ref, gb_ref, ca_ref, sa1_ref, sa2_ref, ci_ref, si1_ref, si2_ref,
                   q_ref, k_ref, v_ref, iq_ref, ik_ref, iw_ref, mq_ref, mk_ref, mv_ref, mo_ref, g_ref,
                   *, idx_w_scale):
    x = x_ref[...]
    var = jnp.mean(x * x, axis=-1, keepdims=True)
    h = (x * lax.rsqrt(var + NORM_EPS)) * nw_ref[...]
    hb = h.astype(BF16)

    def proj(name):
        lo, width = _SEG[name]
        return jnp.dot(hb, w_ref[:, lo:lo + width], preferred_element_type=F32)

    def rope(p, c_ref, s1_ref, s2_ref, half):
        c, s1, s2 = c_ref[...], s1_ref[...], s2_ref[...]
        outs = []
        for j in range(p.shape[1] // LANES):
            xs = p[:, j * LANES:(j + 1) * LANES]
            outs.append(xs * c + pltpu.roll(xs, LANES - half, 1) * s1 + pltpu.roll(xs, half, 1) * s2)
        return outs[0] if len(outs) == 1 else jnp.concatenate(outs, axis=1)

    att_scale = float(HEAD_DIM ** -0.5 * np.log2(np.e))
    q_ref[...] = (rope(proj("q"), ca_ref, sa1_ref, sa2_ref, ATT_ROT_DIM // 2) * att_scale).astype(BF16)
    k_ref[...] = rope(proj("k"), ca_ref, sa1_ref, sa2_ref, ATT_ROT_DIM // 2).astype(BF16)
    v_ref[...] = proj("v").astype(BF16)
    iq_ref[...] = rope(proj("iq"), ci_ref, si1_ref, si2_ref, IDX_ROT_DIM // 2).astype(BF16)
    ik_ref[...] = rope(proj("ik"), ci_ref, si1_ref, si2_ref, IDX_ROT_DIM // 2).astype(BF16)
    iw_ref[...] = proj("iw") * idx_w_scale
    mq_ref[...] = proj("mq")
    mk_ref[...] = proj("mk")
    mv_ref[...] = proj("mv").astype(BF16)
    mo_ref[...] = proj("mo")
    g = proj("g") + gb_ref[...]
    lane = lax.broadcasted_iota(jnp.int32, g.shape, 1)
    log_f = jnp.minimum(g, 0.0) - jnp.log(1.0 + jnp.exp(-jnp.abs(g)))
    g_ref[...] = jnp.where(lane < MLSTM_HEADS, g, log_f)


def _rope_tables(seq, rot_dim, head_dim):
    pos = jnp.arange(seq, dtype=F32)
    inv_freq = ROPE_THETA ** (-jnp.arange(0, rot_dim, 2, dtype=F32) / rot_dim)
    ang = pos[:, None] * inv_freq[None, :]
    cos, sin = jnp.cos(ang), jnp.sin(ang)
    half = rot_dim // 2
    lane = np.arange(LANES) % head_dim
    fidx = lane % half
    first = lane < half
    second = (lane >= half) & (lane < rot_dim)
    cos_l, sin_l = cos[:, fidx], sin[:, fidx]
    c = jnp.where(jnp.asarray(first | second)[None, :], cos_l, 1.0)
    s1 = jnp.where(jnp.asarray(first)[None, :], -sin_l, 0.0)
    s2 = jnp.where(jnp.asarray(second)[None, :], sin_l, 0.0)
    return c, s1, s2


def _pack_w_in(w_in):
    sizes = (ATT_WIDTH, ATT_KV_HEADS * HEAD_DIM, ATT_KV_HEADS * HEAD_DIM, IDX_HEADS * IDX_DIM, IDX_DIM,
             IDX_HEADS, MLSTM_WIDTH, MLSTM_WIDTH, MLSTM_WIDTH, MLSTM_HEADS, MLSTM_HEADS, MLSTM_WIDTH)
    offs = np.concatenate([[0], np.cumsum(sizes)])
    aq, ak, av, iq, ik, iw, mq, mk, mv, mi, mf, mo = (w_in[:, offs[i]:offs[i + 1]] for i in range(12))
    d = w_in.shape[0]
    head_order = [h for s in range(Q_PER_KV) for h in (s, Q_PER_KV + s)]
    aq = aq.reshape(d, ATT_HEADS, HEAD_DIM)[:, head_order, :].reshape(d, ATT_WIDTH)
    ik_rep = jnp.tile(ik, (1, LANES // IDX_DIM))
    iw_pad = jnp.pad(iw, ((0, 0), (0, LANES - IDX_HEADS)))
    g_pad = jnp.pad(jnp.concatenate([mi, mf], axis=1), ((0, 0), (0, LANES - 2 * MLSTM_HEADS)))
    packed = jnp.concatenate([aq, ak, av, iq, ik_rep, iw_pad, mq, mk, mv, mo, g_pad], axis=1)
    assert packed.shape[1] == PACKED_WIDTH
    return packed.astype(BF16)


def _in_projection(x2, norm_w, w_in, igate_b, fgate_b, seq, tm=512):
    t, d = x2.shape
    assert t % tm == 0 and seq % tm == 0
    wp = _pack_w_in(w_in)
    gb = jnp.pad(jnp.concatenate([igate_b, fgate_b]), (0, LANES - 2 * MLSTM_HEADS)).reshape(1, LANES).astype(F32)
    ca, sa1, sa2 = _rope_tables(seq, ATT_ROT_DIM, HEAD_DIM)
    ci, si1, si2 = _rope_tables(seq, IDX_ROT_DIM, IDX_DIM)
    n_pos_blocks = seq // tm
    row = lambda i: (i, 0)
    fixed = lambda i: (0, 0)
    pos = lambda i: (i % n_pos_blocks, 0)
    tab = pl.BlockSpec((tm, LANES), pos)
    out_defs = (("q", 512, BF16), ("k", 128, BF16), ("v", 128, BF16), ("iq", 256, BF16), ("ik", 128, BF16),
                ("iw", 128, F32), ("mq", 512, F32), ("mk", 512, F32), ("mv", 512, BF16), ("mo", 512, F32),
                ("g", 128, F32))
    idx_w_scale = (IDX_HEADS ** -0.5) * (IDX_DIM ** -0.5)
    outs = pl.pallas_call(
        functools.partial(_inproj_kernel, idx_w_scale=idx_w_scale),
        grid=(t // tm,),
        in_specs=[pl.BlockSpec((tm, d), row), pl.BlockSpec((1, d), fixed),
                  pl.BlockSpec((d, PACKED_WIDTH), fixed), pl.BlockSpec((1, LANES), fixed),
                  tab, tab, tab, tab, tab, tab],
        out_specs=[pl.BlockSpec((tm, w), row) for _, w, _ in out_defs],
        out_shape=[jax.ShapeDtypeStruct((t, w), dt) for _, w, dt in out_defs],
        compiler_params=_cparams("parallel"),
        name="in_projection",
    )(x2, norm_w.reshape(1, d), wp, gb, ca, sa1, sa2, ci, si1, si2)
    return dict(zip([n for n, _, _ in out_defs], outs))


def _dsa_kernel(q_ref, iq_ref, iw_ref, k_ref, vt_ref, ik_ref, o_ref, score_ref, acc_ref, vk_ref, jcut_ref, p_ref,
                *, k_sel, index_iters):
    i = pl.program_id(1)
    n_keys = Q_BLOCK * (i + 1)
    n_steps = (n_keys + TILES_PER_STEP * KEY_TILE - 1) // (TILES_PER_STEP * KEY_TILE)

    def for_tiles(body, init, lead=None):
        def step(j, carry):
            tiles = [j * TILES_PER_STEP + u for u in range(TILES_PER_STEP)]
            opened = [lead(kt) if lead is not None else None for kt in tiles]
            for kt, head in zip(tiles, opened):
                carry = body(kt, carry) if lead is None else body(kt, head, carry)
            return carry
        return lax.fori_loop(0, n_steps, step, init)
    n_stack = ATT_HEADS * Q_BLOCK

    lane = lax.broadcasted_iota(jnp.int32, (Q_BLOCK, LANES), 1)
    iq = iq_ref[0]
    per_slab = LANES // IDX_DIM
    qi_stack = jnp.concatenate(
        [jnp.where((lane // IDX_DIM) == (h % per_slab), iq[:, (h // per_slab) * LANES:(h // per_slab + 1) * LANES], 0)
         for h in range(IDX_HEADS)], axis=0)
    q = q_ref[0]
    q_stack = jnp.concatenate(
        [jnp.where((lane // HEAD_DIM) == g, q[:, s * LANES:(s + 1) * LANES], 0)
         for s in range(Q_PER_KV) for g in range(ATT_KV_HEADS)], axis=0)
    w_t = iw_ref[0].T

    q_lane = lax.broadcasted_iota(jnp.int32, (1, Q_BLOCK), 1)
    key_limit = (2 * i + 1 + (q_lane >= CHUNK).astype(jnp.int32)) * CHUNK
    k_eff = jnp.minimum(key_limit, k_sel).astype(F32)
    key_iota = lax.broadcasted_iota(jnp.int32, (KEY_TILE, Q_BLOCK), 0)
    nt_dims = (((1,), (1,)), ((), ()))

    def idx_lead(kt):
        start = pl.multiple_of(kt * KEY_TILE, KEY_TILE)
        return lax.dot_general(ik_ref[0, pl.ds(start, KEY_TILE), :], qi_stack, nt_dims,
                               preferred_element_type=F32)

    def idx_body(kt, z, carry):
        rmin, rmax = carry
        start = pl.multiple_of(kt * KEY_TILE, KEY_TILE)
        sc = jnp.zeros((KEY_TILE, Q_BLOCK), F32)
        for h in range(IDX_HEADS):
            sc = sc + jnp.maximum(z[:, h * Q_BLOCK:(h + 1) * Q_BLOCK], 0.0) * w_t[h:h + 1, :]
        adm = (key_iota + start) < key_limit
        score_ref[pl.ds(start, KEY_TILE), :] = jnp.where(adm, sc, -jnp.inf)
        rmin = jnp.minimum(rmin, jnp.min(jnp.where(adm, sc, jnp.inf), axis=0, keepdims=True))
        rmax = jnp.maximum(rmax, jnp.max(jnp.where(adm, sc, -jnp.inf), axis=0, keepdims=True))
        return rmin, rmax

    init = (jnp.full((1, Q_BLOCK), jnp.inf, F32), jnp.full((1, Q_BLOCK), -jnp.inf, F32))
    lo, hi = for_tiles(idx_body, init, lead=idx_lead)

    def fold(x):
        return x.reshape(KEY_TILE // FOLD_ROWS, FOLD_ROWS, Q_BLOCK)

    def count(pred):
        def body(kt, acc):
            start = pl.multiple_of(kt * KEY_TILE, KEY_TILE)
            hit = pred(score_ref[pl.ds(start, KEY_TILE), :], key_iota + start).astype(F32)
            return acc + jnp.sum(fold(hit), axis=0)
        acc = for_tiles(body, jnp.zeros((FOLD_ROWS, Q_BLOCK), F32))
        return jnp.sum(acc, axis=0, keepdims=True)

    def unresolved(c_lo):
        return jnp.max(c_lo - k_eff) > 0.0

    def smallest_at_least(lo):
        def min_body(kt, acc):
            start = pl.multiple_of(kt * KEY_TILE, KEY_TILE)
            x = score_ref[pl.ds(start, KEY_TILE), :]
            return jnp.minimum(acc, jnp.min(fold(jnp.where(x >= lo, x, jnp.inf)), axis=0))
        return jnp.min(for_tiles(min_body, jnp.full((FOLD_ROWS, Q_BLOCK), jnp.inf, F32)), axis=0, keepdims=True)

    def bisect(carry):
        it, lo, hi, c_lo, _ = carry
        for _ in range(BISECT_GROUP):
            mid = 0.5 * (lo + hi)
            c_mid = count(lambda x, _: x >= mid)
            ok = c_mid >= k_eff
            lo, hi, c_lo = jnp.where(ok, mid, lo), jnp.where(ok, hi, mid), jnp.where(ok, c_mid, c_lo)
        it = it + BISECT_GROUP

        def kth_isolated(_):
            above = count(lambda x, _: x > smallest_at_least(lo))
            still_open = (c_lo > k_eff) & (above >= k_eff)
            return (jnp.max(still_open.astype(F32)) == 0.0).astype(jnp.int32)

        settled = lax.cond((it >= TIE_CHECK_FROM) & unresolved(c_lo), kth_isolated, lambda _: jnp.int32(0), 0)
        return it, lo, hi, c_lo, settled

    _, lo, _, c_lo, _ = lax.while_loop(
        lambda c: (c[0] < BISECT_ITERS) & unresolved(c[3]) & (c[4] == 0), bisect,
        (jnp.int32(0), lo, hi, key_limit.astype(F32), jnp.int32(0)))

    vk_ref[...] = jnp.broadcast_to(lo, vk_ref.shape)
    jcut_ref[...] = jnp.full(jcut_ref.shape, jnp.iinfo(jnp.int32).max, jnp.int32)

    @pl.when(unresolved(c_lo))
    def _():
        vk = smallest_at_least(lo)
        need = k_eff - count(lambda x, _: x > vk)

        def jbisect(_, carry):
            jlo, jhi = carry
            jmid = jnp.right_shift(jlo + jhi, 1)
            ok = count(lambda x, kidx: (x == vk) & (kidx < jmid)) >= need
            return jnp.where(ok, jlo, jmid), jnp.where(ok, jmid, jhi)

        j0 = (jnp.zeros((1, Q_BLOCK), jnp.int32), jnp.full((1, Q_BLOCK), n_keys, jnp.int32))
        _, jcut = lax.fori_loop(0, index_iters, jbisect, j0)
        vk_ref[...] = jnp.broadcast_to(vk, vk_ref.shape)
        jcut_ref[...] = jnp.broadcast_to(jcut, jcut_ref.shape)

    vk = vk_ref[0:1, :]
    jcut = jcut_ref[0:1, :]

    r_i = lax.broadcasted_iota(jnp.int32, (Q_BLOCK, LANES), 0)
    eye = jnp.where(r_i == lane, 1.0, 0.0).astype(BF16)
    q_aug = jnp.concatenate([q_stack, jnp.concatenate([eye] * ATT_HEADS, axis=0)], axis=1)
    acc_ref[...] = jnp.zeros_like(acc_ref)

    def att_lead(kt):
        start = pl.multiple_of(kt * KEY_TILE, KEY_TILE)
        x = score_ref[pl.ds(start, KEY_TILE), :]
        sel = (x > vk) | ((x == vk) & ((key_iota + start) < jcut))
        bias = jnp.where(sel, 0.0, NEG_BIG).astype(BF16)
        k_aug = jnp.concatenate([k_ref[0, pl.ds(start, KEY_TILE), :], bias], axis=1)
        return lax.dot_general(k_aug, q_aug, nt_dims, preferred_element_type=F32)

    def accumulate(kt, alpha):
        acc_ref[...] = alpha * acc_ref[...] + jnp.dot(vt_ref[0, kt], p_ref[...], preferred_element_type=F32)

    def att_body(kt, s, carry):
        m, alpha = carry
        accumulate(jnp.maximum(kt - 1, 0), alpha)
        m_new = jnp.maximum(m, jnp.max(s, axis=0, keepdims=True))
        p_ref[...] = jnp.exp2(s - m_new).astype(BF16)
        return m_new, jnp.exp2(m - m_new)

    p_ref[...] = jnp.zeros_like(p_ref)
    init = (jnp.full((1, n_stack), NEG_BIG, F32), jnp.ones((1, n_stack), F32))
    _, alpha = for_tiles(att_body, init, lead=att_lead)
    accumulate(TILES_PER_STEP * n_steps - 1, alpha)

    out_t = acc_ref[0:LANES, :] / acc_ref[LANES:LANES + 1, :]
    for s in range(Q_PER_KV):
        o0 = out_t[:, (2 * s) * Q_BLOCK:(2 * s + 1) * Q_BLOCK].T
        o1 = out_t[:, (2 * s + 1) * Q_BLOCK:(2 * s + 2) * Q_BLOCK].T
        o_ref[0, :, s * LANES:(s + 1) * LANES] = jnp.where(lane < HEAD_DIM, o0, o1).astype(o_ref.dtype)


def _dsa_attention(q, iq, iw, k, v, ik, k_sel):
    bsz, seq, _ = q.shape
    assert seq % (TILES_PER_STEP * KEY_TILE) == 0 and seq % Q_BLOCK == 0
    n_kt = seq // KEY_TILE
    vt = v.reshape(bsz, n_kt, KEY_TILE, LANES).transpose(0, 1, 3, 2)
    ones_rows = jnp.zeros((bsz, n_kt, V_ROWS - LANES, KEY_TILE), v.dtype).at[:, :, 0, :].set(1.0)
    vt = jnp.concatenate([vt, ones_rows], axis=2)
    blk = lambda w: pl.BlockSpec((1, Q_BLOCK, w), lambda b, i: (b, i, 0))
    full = pl.BlockSpec((1, seq, LANES), lambda b, i: (b, 0, 0))
    return pl.pallas_call(
        functools.partial(_dsa_kernel, k_sel=k_sel, index_iters=seq.bit_length()),
        grid=(bsz, seq // Q_BLOCK),
        in_specs=[blk(ATT_WIDTH), blk(IDX_HEADS * IDX_DIM), blk(LANES), full,
                  pl.BlockSpec((1, n_kt, V_ROWS, KEY_TILE), lambda b, i: (b, 0, 0, 0)), full],
        out_specs=blk(ATT_WIDTH),
        out_shape=jax.ShapeDtypeStruct((bsz, seq, ATT_WIDTH), BF16),
        scratch_shapes=[pltpu.VMEM((seq, Q_BLOCK), F32), pltpu.VMEM((V_ROWS, ATT_HEADS * Q_BLOCK), F32),
                        pltpu.VMEM((SUBLANES, Q_BLOCK), F32), pltpu.VMEM((SUBLANES, Q_BLOCK), jnp.int32),
                        pltpu.VMEM((KEY_TILE, ATT_HEADS * Q_BLOCK), BF16)],
        compiler_params=_cparams("parallel", "parallel"),
        name="dsa_attention",
    )(q, iq, iw, k, vt, ik)


def _mlstm_kernel(mq_ref, mk_ref, mv_ref, mo_ref, gc_ref, gr_ref, cw_ref, cb_ref, nw_ref, o_ref,
                  c_ref, n_ref, m_ref, prev_ref):
    L = MLSTM_CHUNK
    W = MLSTM_WIDTH
    dh = MLSTM_HEAD_DIM

    @pl.when(pl.program_id(1) == 0)
    def _():
        c_ref[...] = jnp.zeros_like(c_ref)
        n_ref[...] = jnp.zeros_like(n_ref)
        m_ref[...] = jnp.zeros_like(m_ref)
        prev_ref[...] = jnp.zeros_like(prev_ref)

    row = lax.broadcasted_iota(jnp.int32, (SUBLANES, W), 0)

    def conv_silu(x, prev8, w4, b):
        y = x * w4[CONV_WIDTH - 1:CONV_WIDTH, :] + b
        for d in range(1, CONV_WIDTH):
            xr = pltpu.roll(x, d, 0)
            head = jnp.where(row < d, pltpu.roll(prev8, d, 0), xr[:SUBLANES])
            xs = jnp.concatenate([head, xr[SUBLANES:]], axis=0)
            y = y + xs * w4[CONV_WIDTH - 1 - d:CONV_WIDTH - d, :]
        return y / (1.0 + jnp.exp(-y))

    xq, xk = mq_ref[0], mk_ref[0]
    cw, cb = cw_ref[...], cb_ref[...]
    q = conv_silu(xq, prev_ref[:, :W], cw[:, :W], cb[:, :W])
    k = conv_silu(xk, prev_ref[:, W:], cw[:, W:], cb[:, W:]) * (dh ** -0.5)
    prev_ref[:, :W] = xq[L - SUBLANES:, :]
    prev_ref[:, W:] = xk[L - SUBLANES:, :]

    g_col = gc_ref[0]
    g_row = gr_ref[0]
    r_i = lax.broadcasted_iota(jnp.int32, (L, L), 0)
    c_i = lax.broadcasted_iota(jnp.int32, (L, L), 1)
    causal = r_i >= c_i
    tri = causal.astype(F32)
    b_col_all = jnp.dot(tri, g_col, preferred_element_type=F32, precision=lax.Precision.HIGHEST)
    b_row_all = jnp.dot(g_row, (r_i <= c_i).astype(F32), preferred_element_type=F32,
                        precision=lax.Precision.HIGHEST)

    v = mv_ref[0]
    mo = mo_ref[0]
    nw = nw_ref[...]
    for h in range(MLSTM_HEADS):
        hs = slice(h * dh, (h + 1) * dh)
        i_col = _lane_col(g_col, h)
        i_row = g_row[h:h + 1, :]
        b_col = _lane_col(b_col_all, MLSTM_HEADS + h)
        b_row = b_row_all[MLSTM_HEADS + h:MLSTM_HEADS + h + 1, :]
        b_last = jnp.sum(g_row[MLSTM_HEADS + h:MLSTM_HEADS + h + 1, :], axis=1, keepdims=True)
        m_prev = m_ref[h:h + 1, 0:1]
        c_mat = c_ref[h]
        n_vec = n_ref[h:h + 1, :]

        d_log = jnp.where(causal, b_col + (i_row - b_row), -jnp.inf)
        inter_log = b_col + m_prev
        m_t = jnp.maximum(inter_log, jnp.max(d_log, axis=1, keepdims=True))
        d_w = jnp.exp(d_log - m_t)
        inter_w = jnp.exp(inter_log - m_t)
        qh, kh, vh = q[:, hs], k[:, hs], v[:, hs]
        qb, kb = qh.astype(BF16), kh.astype(BF16)
        s = lax.dot_general(qb, kb, (((1,), (1,)), ((), ())), preferred_element_type=F32) * d_w
        num = (jnp.dot(s.astype(BF16), vh, preferred_element_type=F32)
               + inter_w * jnp.dot(qb, c_mat.astype(BF16), preferred_element_type=F32))
        den = jnp.sum(s, axis=1, keepdims=True) + inter_w * jnp.sum(qh * n_vec, axis=1, keepdims=True)
        hout = num / jnp.maximum(jnp.abs(den), jnp.exp(-m_t))

        w_log = b_last - b_col + i_col
        m_new = jnp.maximum(b_last + m_prev, jnp.max(w_log, axis=0, keepdims=True))
        decay = jnp.exp(b_last + m_prev - m_new)
        kw = kh * jnp.exp(w_log - m_new)
        c_ref[h] = decay * c_mat + jnp.dot(kw.T.astype(BF16), vh, preferred_element_type=F32)
        n_ref[h:h + 1, :] = decay * n_vec + jnp.sum(kw, axis=0, keepdims=True)
        m_ref[h:h + 1, :] = jnp.broadcast_to(m_new, (1, LANES))

        hn = hout * lax.rsqrt(jnp.mean(hout * hout, axis=-1, keepdims=True) + NORM_EPS)
        gate = 1.0 / (1.0 + jnp.exp(-mo[:, hs]))
        o_ref[0, :, hs] = (hn * nw[:, hs] * gate).astype(o_ref.dtype)


def _mlstm_mixer(mq, mk, mv, mo, g, conv_w, conv_b, norm_w):
    bsz, seq, w = mq.shape
    L = MLSTM_CHUNK
    assert seq % L == 0
    g_row = g[:, :, :SUBLANES].transpose(0, 2, 1)
    blk = lambda width: pl.BlockSpec((1, L, width), lambda b, c: (b, c, 0))
    fixed = lambda shape: pl.BlockSpec(shape, lambda b, c: (0, 0))
    return pl.pallas_call(
        _mlstm_kernel,
        grid=(bsz, seq // L),
        in_specs=[blk(w), blk(w), blk(w), blk(w), blk(LANES),
                  pl.BlockSpec((1, SUBLANES, L), lambda b, c: (b, 0, c)),
                  fixed((CONV_WIDTH, 2 * w)), fixed((1, 2 * w)), fixed((1, w))],
        out_specs=blk(w),
        out_shape=jax.ShapeDtypeStruct((bsz, seq, w), BF16),
        scratch_shapes=[pltpu.VMEM((MLSTM_HEADS, MLSTM_HEAD_DIM, MLSTM_HEAD_DIM), F32),
                        pltpu.VMEM((SUBLANES, MLSTM_HEAD_DIM), F32),
                        pltpu.VMEM((SUBLANES, LANES), F32),
                        pltpu.VMEM((SUBLANES, 2 * w), F32)],
        compiler_params=_cparams("parallel", "arbitrary"),
        name="mlstm_mixer",
    )(mq, mk, mv, mo, g, g_row, conv_w, conv_b.reshape(1, 2 * w), norm_w.reshape(1, w))


def _outproj_router_kernel(att_ref, mem_ref, x_ref, wa_ref, wm_ref, nw_ref, rw_ref, rb_ref,
                           x1_ref, hn_ref, ti_ref, tg_ref, cnt_ref):
    y = (jnp.dot(att_ref[...], wa_ref[...], preferred_element_type=F32)
         + jnp.dot(mem_ref[...], wm_ref[...], preferred_element_type=F32))
    x1 = x_ref[...] + y
    x1_ref[...] = x1
    var = jnp.mean(x1 * x1, axis=-1, keepdims=True)
    hn = (x1 * lax.rsqrt(var + NORM_EPS)) * nw_ref[...]
    hn_ref[...] = hn
    hn_hi = hn.astype(BF16)
    hn_lo = (hn - hn_hi.astype(F32)).astype(BF16)
    rw = rw_ref[...]
    rw_hi = rw.astype(BF16)
    rw_lo = (rw - rw_hi.astype(F32)).astype(BF16)
    part = jnp.dot(hn_hi, jnp.concatenate([rw_hi, rw_lo], axis=1), preferred_element_type=F32)
    logits = (part[:, :LANES] + part[:, LANES:]
              + jnp.dot(hn_lo, rw_hi, preferred_element_type=F32) + rb_ref[...])
    lane_i = lax.broadcasted_iota(jnp.int32, logits.shape, 1)
    lane = lane_i.astype(F32)
    cur = jnp.where(lane_i < N_EXPERTS, logits, -jnp.inf)
    vals, idxs = [], []
    for _ in range(TOP_K):
        mx = jnp.max(cur, axis=1, keepdims=True)
        am = jnp.min(jnp.where(cur == mx, lane, float(LANES)), axis=1, keepdims=True)
        vals.append(mx)
        idxs.append(am)
        cur = jnp.where(lane == am, -jnp.inf, cur)
    exps = [jnp.exp(vk - vals[0]) for vk in vals]
    denom = exps[0]
    for e in exps[1:]:
        denom = denom + e
    @pl.when(pl.program_id(0) == 0)
    def _():
        cnt_ref[...] = jnp.zeros_like(cnt_ref)

    tm = logits.shape[0]
    picks = [lane == am for am in idxs]
    picked = jnp.zeros(logits.shape, F32)
    for pk in picks:
        picked = picked + jnp.where(pk, 1.0, 0.0)
    r_i = lax.broadcasted_iota(jnp.int32, (tm, tm), 0)
    c_i = lax.broadcasted_iota(jnp.int32, (tm, tm), 1)
    before = jnp.dot(jnp.where(c_i < r_i, 1.0, 0.0).astype(BF16), picked.astype(BF16), preferred_element_type=F32)
    base = before + cnt_ref[0:1, :]
    cnt_ref[...] = cnt_ref[...] + jnp.sum(picked, axis=0, keepdims=True)

    ti = jnp.zeros(logits.shape, F32)
    tg = jnp.zeros(logits.shape, F32)
    for kk in range(TOP_K):
        rank = jnp.sum(jnp.where(picks[kk], base, 0.0), axis=1, keepdims=True)
        ti = jnp.where(lane_i == kk, idxs[kk], ti)
        ti = jnp.where(lane_i == TOP_K + kk, rank, ti)
        tg = jnp.where(lane_i == kk, exps[kk] / denom, tg)
    ti_ref[...] = ti.astype(jnp.int32)
    tg_ref[...] = tg


def _outproj_router(att, mem, x2, w_att, w_mem, ffn_norm_w, router_w, router_b, tm=256):
    t, d = x2.shape
    row = lambda i: (i, 0)
    fixed = lambda i: (0, 0)
    rw = jnp.pad(router_w, ((0, 0), (0, LANES - N_EXPERTS)))
    rb = jnp.pad(router_b, (0, LANES - N_EXPERTS)).reshape(1, LANES)
    return pl.pallas_call(
        _outproj_router_kernel,
        grid=(t // tm,),
        in_specs=[pl.BlockSpec((tm, ATT_WIDTH), row), pl.BlockSpec((tm, MLSTM_WIDTH), row),
                  pl.BlockSpec((tm, d), row), pl.BlockSpec((ATT_WIDTH, d), fixed),
                  pl.BlockSpec((MLSTM_WIDTH, d), fixed), pl.BlockSpec((1, d), fixed),
                  pl.BlockSpec((d, LANES), fixed), pl.BlockSpec((1, LANES), fixed)],
        out_specs=[pl.BlockSpec((tm, d), row), pl.BlockSpec((tm, d), row),
                   pl.BlockSpec((tm, LANES), row), pl.BlockSpec((tm, LANES), row),
                   pl.BlockSpec((SUBLANES, LANES), fixed)],
        out_shape=[jax.ShapeDtypeStruct((t, d), F32), jax.ShapeDtypeStruct((t, d), F32),
                   jax.ShapeDtypeStruct((t, LANES), jnp.int32), jax.ShapeDtypeStruct((t, LANES), F32),
                   jax.ShapeDtypeStruct((SUBLANES, LANES), F32)],
        compiler_params=_cparams("arbitrary"),
        name="outproj_router",
    )(att, mem, x2, w_att, w_mem, ffn_norm_w.reshape(1, d), rw, rb)


DISPATCH_ROWS = 256


def _dispatch_kernel(fill_ref, nused_ref, dest_ref, hn_ref, x_hbm, stage, zero_buf, sem, zsem, bsem):
    n_blocks = x_hbm.shape[0] // EXPERT_ROWS
    step = pl.program_id(0)
    slot = step % 2
    stage[slot] = hn_ref[...]

    def token_row(j, kk):
        return pltpu.make_async_copy(stage.at[slot, pl.ds(j, 1), :],
                                     x_hbm.at[pl.ds(dest_ref[0, 0, j * TOP_K + kk], 1), :], sem.at[slot])

    def wait_tokens(which):
        all_rows = x_hbm.at[pl.ds(0, TOP_K * DISPATCH_ROWS), :]
        pltpu.make_async_copy(all_rows, all_rows, sem.at[which]).wait()

    def pad_row(e, r):
        return pltpu.make_async_copy(zero_buf.at[pl.ds(0, 1), :], x_hbm.at[pl.ds(fill_ref[0, e] + r, 1), :], zsem)

    def pad_block(b):
        start = pl.multiple_of(b * EXPERT_ROWS, EXPERT_ROWS)
        return pltpu.make_async_copy(zero_buf, x_hbm.at[pl.ds(start, EXPERT_ROWS), :], bsem)

    def start_tokens(j, c):
        for kk in range(TOP_K):
            token_row(j, kk).start()
        return c

    def for_padding(fn):
        def per_expert(e, c):
            lax.fori_loop(0, fill_ref[1, e], lambda r, cc: (fn(pad_row(e, r)), cc)[1], 0)
            return c
        lax.fori_loop(0, N_EXPERTS, per_expert, 0)
        lax.fori_loop(nused_ref[0], n_blocks, lambda b, cc: (fn(pad_block(b)), cc)[1], 0)

    lax.fori_loop(0, DISPATCH_ROWS, start_tokens, 0)

    @pl.when(pl.program_id(0) == 0)
    def _():
        zero_buf[...] = jnp.zeros_like(zero_buf)
        for_padding(lambda cp: cp.start())
        for_padding(lambda cp: cp.wait())

    @pl.when(step > 0)
    def _():
        wait_tokens(1 - slot)

    @pl.when(step == pl.num_programs(0) - 1)
    def _():
        wait_tokens(slot)


def _dispatch_rows(hn, dest, fill, n_blocks_used, n_rows):
    t, d = hn.shape
    tm = DISPATCH_ROWS
    nb = t // tm
    return pl.pallas_call(
        _dispatch_kernel,
        grid_spec=pltpu.PrefetchScalarGridSpec(
            num_scalar_prefetch=2,
            grid=(nb,),
            in_specs=[pl.BlockSpec((1, 1, tm * TOP_K), lambda i, f, n: (i, 0, 0), memory_space=pltpu.SMEM),
                      pl.BlockSpec((tm, d), lambda i, f, n: (i, 0))],
            out_specs=pl.BlockSpec(memory_space=pl.ANY),
            scratch_shapes=[pltpu.VMEM((2, tm, d), hn.dtype), pltpu.VMEM((EXPERT_ROWS, d), hn.dtype),
                            pltpu.SemaphoreType.DMA((2,)), pltpu.SemaphoreType.DMA(()),
                            pltpu.SemaphoreType.DMA(())],
        ),
        out_shape=jax.ShapeDtypeStruct((n_rows, d), hn.dtype),
        compiler_params=_cparams("arbitrary"),
        name="moe_dispatch",
    )(fill, n_blocks_used, dest.reshape(nb, 1, tm * TOP_K), hn)


def _expert_kernel(be_ref, nu_ref, nxt_ref, par_ref, x_ref, bgu_ref, bdn_ref, wgu_hbm, wdn_hbm, o_ref,
                   wgu_f, wdn_f, wgu_s, wdn_s, sem):
    b = pl.program_id(0)
    e = be_ref[b]
    e_prev = be_ref[jnp.maximum(b - 1, 0)]
    slot = par_ref[b]
    used = b < nu_ref[0]
    d_e = wdn_s.shape[0]

    def fetch(expert, into):
        return (pltpu.make_async_copy(wgu_hbm.at[expert], wgu_f.at[into], sem.at[0, into]),
                pltpu.make_async_copy(wdn_hbm.at[expert], wdn_f.at[into], sem.at[1, into]))

    @pl.when(b == 0)
    def _():
        for cp in fetch(e, slot):
            cp.start()

    @pl.when(used & ((b == 0) | (e != e_prev)))
    def _():
        for cp in fetch(e, slot):
            cp.wait()

        @pl.when(nxt_ref[b] >= 0)
        def _():
            for cp in fetch(nxt_ref[b], 1 - slot):
                cp.start()

        wgu_s[...] = wgu_f[slot].astype(BF16)
        wdn_s[...] = wdn_f[slot].astype(BF16)

    @pl.when(used)
    def _():
        xb = x_ref[...].astype(BF16)
        bgu = bgu_ref[0]
        glu = jnp.dot(xb, wgu_s[:, :d_e], preferred_element_type=F32) + bgu[:, :d_e]
        lin = jnp.dot(xb, wgu_s[:, d_e:], preferred_element_type=F32) + bgu[:, d_e:]
        glu = jnp.minimum(glu, SWIGLU_LIMIT)
        lin = jnp.clip(lin, -SWIGLU_LIMIT, SWIGLU_LIMIT)
        act = glu * (1.0 / (1.0 + jnp.exp(-SWIGLU_ALPHA * glu))) * (lin + 1.0)
        o_ref[...] = jnp.dot(act.astype(BF16), wdn_s[...], preferred_element_type=F32) + bdn_ref[0]

    @pl.when(jnp.logical_not(used))
    def _():
        o_ref[...] = jnp.zeros_like(o_ref)


def _expert_ffn(x_buf, blk_expert, n_blocks_used, next_expert, run_parity, w_gu, b_gu, w_dn, b_dn):
    n_rows, d = x_buf.shape
    ne, _, d2 = w_gu.shape
    d_e = w_dn.shape[1]
    nb = n_rows // EXPERT_ROWS
    last_used = lambda b, be, nu, nx, pr: (jnp.minimum(b, jnp.maximum(nu[0] - 1, 0)), 0)
    by_expert = lambda b, be, nu, nx, pr: (be[b], 0, 0)
    return pl.pallas_call(
        _expert_kernel,
        grid_spec=pltpu.PrefetchScalarGridSpec(
            num_scalar_prefetch=4,
            grid=(nb,),
            in_specs=[pl.BlockSpec((EXPERT_ROWS, d), last_used),
                      pl.BlockSpec((1, 1, d2), by_expert),
                      pl.BlockSpec((1, 1, d), by_expert),
                      pl.BlockSpec(memory_space=pl.ANY),
                      pl.BlockSpec(memory_space=pl.ANY)],
            out_specs=pl.BlockSpec((EXPERT_ROWS, d), lambda b, be, nu, nx, pr: (b, 0)),
            scratch_shapes=[pltpu.VMEM((2, d, d2), w_gu.dtype), pltpu.VMEM((2, d_e, d), w_dn.dtype),
                            pltpu.VMEM((d, d2), BF16), pltpu.VMEM((d_e, d), BF16),
                            pltpu.SemaphoreType.DMA((2, 2))],
        ),
        out_shape=jax.ShapeDtypeStruct((n_rows, d), F32),
        compiler_params=_cparams("arbitrary"),
        name="moe_experts",
    )(blk_expert, n_blocks_used, next_expert, run_parity, x_buf, b_gu.reshape(ne, 1, d2), b_dn.reshape(ne, 1, d),
      w_gu, w_dn)


COMBINE_ROWS = 256


def _combine_kernel(pos_ref, pos_next_ref, g_ref, x1_ref, nw_ref, y_hbm, o_ref, buf, sem, *, apply_norm):
    tm = COMBINE_ROWS
    step = pl.program_id(0)
    slot = step % 2

    def gather(p_ref, into):
        def issue(j, c):
            for kk in range(TOP_K):
                pltpu.make_async_copy(y_hbm.at[pl.ds(p_ref[0, 0, j * TOP_K + kk], 1), :],
                                      buf.at[into, pl.ds(kk * tm + j, 1), :], sem.at[into]).start()
            return c
        lax.fori_loop(0, tm, issue, 0)

    @pl.when(step == 0)
    def _():
        gather(pos_ref, 0)

    @pl.when(step + 1 < pl.num_programs(0))
    def _():
        gather(pos_next_ref, 1 - slot)

    pltpu.make_async_copy(y_hbm.at[pl.ds(0, TOP_K * tm), :], buf.at[slot], sem.at[slot]).wait()

    g = g_ref[...]
    acc = x1_ref[...]
    for kk in range(TOP_K):
        acc = acc + buf[slot, pl.ds(kk * tm, tm), :] * _lane_col(g, kk)
    if apply_norm:
        var = jnp.mean(acc * acc, axis=-1, keepdims=True)
        acc = (acc * lax.rsqrt(var + NORM_EPS)) * nw_ref[...]
    o_ref[...] = acc


def _combine(y_buf, pos, gates, x1, final_norm_w, apply_norm):
    t, d = x1.shape
    tm = COMBINE_ROWS
    nb = t // tm
    row = lambda i: (i, 0)
    pos3 = pos.reshape(nb, 1, tm * TOP_K)
    return pl.pallas_call(
        functools.partial(_combine_kernel, apply_norm=apply_norm),
        grid=(nb,),
        in_specs=[pl.BlockSpec((1, 1, tm * TOP_K), lambda i: (i, 0, 0), memory_space=pltpu.SMEM),
                  pl.BlockSpec((1, 1, tm * TOP_K), lambda i: (jnp.minimum(i + 1, nb - 1), 0, 0),
                               memory_space=pltpu.SMEM),
                  pl.BlockSpec((tm, LANES), row), pl.BlockSpec((tm, d), row),
                  pl.BlockSpec((1, d), lambda i: (0, 0)), pl.BlockSpec(memory_space=pl.ANY)],
        out_specs=pl.BlockSpec((tm, d), row),
        out_shape=jax.ShapeDtypeStruct((t, d), F32),
        scratch_shapes=[pltpu.VMEM((2, TOP_K * tm, d), F32), pltpu.SemaphoreType.DMA((2,))],
        compiler_params=_cparams("arbitrary"),
        name="moe_combine",
    )(pos3, pos3, gates, x1, final_norm_w.reshape(1, d), y_buf)


def _routing_tables(top_idx, rank, counts, n_tokens):
    padded = (counts + EXPERT_ROWS - 1) // EXPERT_ROWS * EXPERT_ROWS
    pad_end = jnp.cumsum(padded)
    pad_start = pad_end - padded
    dest = (pad_start[top_idx] + rank).astype(jnp.int32)
    n_rows = n_tokens * TOP_K + N_EXPERTS * EXPERT_ROWS
    n_blocks = n_rows // EXPERT_ROWS
    blk_start = jnp.arange(n_blocks, dtype=jnp.int32) * EXPERT_ROWS
    blk_expert = jnp.minimum(jnp.sum((pad_end[None, :] <= blk_start[:, None]).astype(jnp.int32), axis=1),
                             N_EXPERTS - 1).astype(jnp.int32)
    n_blocks_used = (pad_end[-1:] // EXPERT_ROWS).astype(jnp.int32)
    fill = jnp.stack([pad_start + counts, padded - counts]).astype(jnp.int32)
    first = jnp.concatenate([jnp.ones((1,), jnp.bool_), blk_expert[1:] != blk_expert[:-1]])
    run_parity = ((jnp.cumsum(first.astype(jnp.int32)) - 1) % 2).astype(jnp.int32)
    next_run = (pad_end[blk_expert] // EXPERT_ROWS).astype(jnp.int32)
    next_expert = jnp.where(next_run < n_blocks_used[0],
                            blk_expert[jnp.minimum(next_run, n_blocks - 1)], -1).astype(jnp.int32)
    return dest, fill, blk_expert, n_blocks_used, next_expert, run_parity, n_rows


def kernel(x, mix_norm_w, w_in, conv_w, conv_b, igate_b, fgate_b, mlstm_norm_w, w_out, ffn_norm_w,
           router_w, router_b, w_gate_up, b_gate_up, w_down, b_down, final_norm_w):
    bsz, seq, d = x.shape
    depth = w_in.shape[0]
    t = bsz * seq
    k_sel = min(TOPK_MAX, seq // 4)
    head_order = [h for s in range(Q_PER_KV) for h in (s, Q_PER_KV + s)]
    x2 = x.reshape(t, d)
    for layer in range(depth):
        p = _in_projection(x2, mix_norm_w[layer], w_in[layer], igate_b[layer], fgate_b[layer], seq)
        r3 = lambda a: a.reshape(bsz, seq, a.shape[-1])
        att = _dsa_attention(r3(p["q"]), r3(p["iq"]), r3(p["iw"]), r3(p["k"]), r3(p["v"]), r3(p["ik"]), k_sel)
        mem = _mlstm_mixer(r3(p["mq"]), r3(p["mk"]), r3(p["mv"]), r3(p["mo"]), r3(p["g"]),
                           conv_w[layer], conv_b[layer], mlstm_norm_w[layer])
        wo = w_out[layer]
        w_att = wo[:ATT_WIDTH].reshape(ATT_HEADS, HEAD_DIM, d)[jnp.asarray(head_order)].reshape(ATT_WIDTH, d)
        x1, hn, top_i, top_g, counts = _outproj_router(
            att.reshape(t, ATT_WIDTH), mem.reshape(t, MLSTM_WIDTH), x2, w_att.astype(BF16),
            wo[ATT_WIDTH:].astype(BF16), ffn_norm_w[layer], router_w[layer], router_b[layer])
        dest, fill, blk_expert, n_blocks_used, next_expert, run_parity, n_rows = _routing_tables(
            top_i[:, :TOP_K], top_i[:, TOP_K:2 * TOP_K], counts[0, :N_EXPERTS].astype(jnp.int32), t)
        x_buf = _dispatch_rows(hn, dest, fill, n_blocks_used, n_rows)
        y_buf = _expert_ffn(x_buf, blk_expert, n_blocks_used, next_expert, run_parity, w_gate_up[layer],
                            b_gate_up[layer], w_down[layer], b_down[layer])
        x2 = _combine(y_buf, dest, top_g, x1, final_norm_w, apply_norm=(layer == depth - 1))
    return x2.reshape(bsz, seq, d)
```

```python
import functools

import jax
import jax.numpy as jnp
import numpy as np
from jax import lax
from jax.experimental import pallas as pl
from jax.experimental.pallas import tpu as pltpu

F32 = jnp.float32
BF16 = jnp.bfloat16

LANES = 128
SUBLANES = 8
VMEM_LIMIT_BYTES = 56 * 1024 * 1024

NORM_EPS = 1e-6
ROPE_THETA = 500000.0
CHUNK = 64

HEAD_DIM = 64
ATT_HEADS = 8
ATT_KV_HEADS = 2
Q_PER_KV = ATT_HEADS // ATT_KV_HEADS
ATT_ROT_DIM = HEAD_DIM // 4
IDX_HEADS = 8
IDX_DIM = 32
IDX_ROT_DIM = IDX_DIM // 4
TOPK_MAX = 256
ATT_WIDTH = ATT_HEADS * HEAD_DIM

MLSTM_HEAD_DIM = 128
MLSTM_HEADS = 4
MLSTM_WIDTH = MLSTM_HEADS * MLSTM_HEAD_DIM
CONV_WIDTH = 4

N_EXPERTS = 32
TOP_K = 4
SWIGLU_ALPHA = 1.702
SWIGLU_LIMIT = 7.0

Q_BLOCK = 128
KEY_TILE = 256
BISECT_ITERS = 32
BISECT_GROUP = 4
TIE_CHECK_FROM = 20
FOLD_ROWS = 32
TILES_PER_STEP = 2
V_ROWS = LANES + 16
MLSTM_CHUNK = 256
EXPERT_ROWS = 256
NEG_BIG = -1e30


def _cparams(*sem):
    return pltpu.CompilerParams(dimension_semantics=sem, vmem_limit_bytes=VMEM_LIMIT_BYTES)


_SEG_WIDTHS = (("q", 512), ("k", 128), ("v", 128), ("iq", 256), ("ik", 128), ("iw", 128),
               ("mq", 512), ("mk", 512), ("mv", 512), ("mo", 512), ("g", 128))
_SEG = {}
_off = 0
for _name, _w in _SEG_WIDTHS:
    _SEG[_name] = (_off, _w)
    _off += _w
PACKED_WIDTH = _off


def _lane_col(x, idx):
    lane = lax.broadcasted_iota(jnp.int32, x.shape, 1)
    return jnp.sum(jnp.where(lane == idx, x, 0.0), axis=1, keepdims=True)


def _inproj_kernel(x_ref, nw_ref, w_ref, gb_ref, ca_ref, sa1_ref, sa2_ref, ci_ref, si1_ref, si2_ref,
                   q_ref, k_ref, v_ref, iq_ref, ik_ref, iw_ref, mq_ref, mk_ref, mv_ref, mo_ref, g_ref,
                   *, idx_w_scale):
    x = x_ref[...]
    var = jnp.mean(x * x, axis=-1, keepdims=True)
    h = (x * lax.rsqrt(var + NORM_EPS)) * nw_ref[...]
    hb = h.astype(BF16)

    def proj(name):
        lo, width = _SEG[name]
        return jnp.dot(hb, w_ref[:, lo:lo + width], preferred_element_type=F32)

    def rope(p, c_ref, s1_ref, s2_ref, half):
        c, s1, s2 = c_ref[...], s1_ref[...], s2_ref[...]
        outs = []
        for j in range(p.shape[1] // LANES):
            xs = p[:, j * LANES:(j + 1) * LANES]
            outs.append(xs * c + pltpu.roll(xs, LANES - half, 1) * s1 + pltpu.roll(xs, half, 1) * s2)
        return outs[0] if len(outs) == 1 else jnp.concatenate(outs, axis=1)

    att_scale = float(HEAD_DIM ** -0.5 * np.log2(np.e))
    q_ref[...] = (rope(proj("q"), ca_ref, sa1_ref, sa2_ref, ATT_ROT_DIM // 2) * att_scale).astype(BF16)
    k_ref[...] = rope(proj("k"), ca_ref, sa1_ref, sa2_ref, ATT_ROT_DIM // 2).astype(BF16)
    v_ref[...] = proj("v").astype(BF16)
    iq_ref[...] = rope(proj("iq"), ci_ref, si1_ref, si2_ref, IDX_ROT_DIM // 2).astype(BF16)
    ik_ref[...] = rope(proj("ik"), ci_ref, si1_ref, si2_ref, IDX_ROT_DIM // 2).astype(BF16)
    iw_ref[...] = proj("iw") * idx_w_scale
    mq_ref[...] = proj("mq")
    mk_ref[...] = proj("mk")
    mv_ref[...] = proj("mv").astype(BF16)
    mo_ref[...] = proj("mo")
    g = proj("g") + gb_ref[...]
    lane = lax.broadcasted_iota(jnp.int32, g.shape, 1)
    log_f = jnp.minimum(g, 0.0) - jnp.log(1.0 + jnp.exp(-jnp.abs(g)))
    g_ref[...] = jnp.where(lane < MLSTM_HEADS, g, log_f)


def _rope_tables(seq, rot_dim, head_dim):
    pos = jnp.arange(seq, dtype=F32)
    inv_freq = ROPE_THETA ** (-jnp.arange(0, rot_dim, 2, dtype=F32) / rot_dim)
    ang = pos[:, None] * inv_freq[None, :]
    cos, sin = jnp.cos(ang), jnp.sin(ang)
    half = rot_dim // 2
    lane = np.arange(LANES) % head_dim
    fidx = lane % half
    first = lane < half
    second = (lane >= half) & (lane < rot_dim)
    cos_l, sin_l = cos[:, fidx], sin[:, fidx]
    c = jnp.where(jnp.asarray(first | second)[None, :], cos_l, 1.0)
    s1 = jnp.where(jnp.asarray(first)[None, :], -sin_l, 0.0)
    s2 = jnp.where(jnp.asarray(second)[None, :], sin_l, 0.0)
    return c, s1, s2


def _pack_w_in(w_in):
    sizes = (ATT_WIDTH, ATT_KV_HEADS * HEAD_DIM, ATT_KV_HEADS * HEAD_DIM, IDX_HEADS * IDX_DIM, IDX_DIM,
             IDX_HEADS, MLSTM_WIDTH, MLSTM_WIDTH, MLSTM_WIDTH, MLSTM_HEADS, MLSTM_HEADS, MLSTM_WIDTH)
    offs = np.concatenate([[0], np.cumsum(sizes)])
    aq, ak, av, iq, ik, iw, mq, mk, mv, mi, mf, mo = (w_in[:, offs[i]:offs[i + 1]] for i in range(12))
    d = w_in.shape[0]
    head_order = [h for s in range(Q_PER_KV) for h in (s, Q_PER_KV + s)]
    aq = aq.reshape(d, ATT_HEADS, HEAD_DIM)[:, head_order, :].reshape(d, ATT_WIDTH)
    ik_rep = jnp.tile(ik, (1, LANES // IDX_DIM))
    iw_pad = jnp.pad(iw, ((0, 0), (0, LANES - IDX_HEADS)))
    g_pad = jnp.pad(jnp.concatenate([mi, mf], axis=1), ((0, 0), (0, LANES - 2 * MLSTM_HEADS)))
    packed = jnp.concatenate([aq, ak, av, iq, ik_rep, iw_pad, mq, mk, mv, mo, g_pad], axis=1)
    assert packed.shape[1] == PACKED_WIDTH
    return packed.astype(BF16)


def _in_projection(x2, norm_w, w_in, igate_b, fgate_b, seq, tm=512):
    t, d = x2.shape
    assert t % tm == 0 and seq % tm == 0
    wp = _pack_w_in(w_in)
    gb = jnp.pad(jnp.concatenate([igate_b, fgate_b]), (0, LANES - 2 * MLSTM_HEADS)).reshape(1, LANES).astype(F32)
    ca, sa1, sa2 = _rope_tables(seq, ATT_ROT_DIM, HEAD_DIM)
    ci, si1, si2 = _rope_tables(seq, IDX_ROT_DIM, IDX_DIM)
    n_pos_blocks = seq // tm
    row = lambda i: (i, 0)
    fixed = lambda i: (0, 0)
    pos = lambda i: (i % n_pos_blocks, 0)
    tab = pl.BlockSpec((tm, LANES), pos)
    out_defs = (("q", 512, BF16), ("k", 128, BF16), ("v", 128, BF16), ("iq", 256, BF16), ("ik", 128, BF16),
                ("iw", 128, F32), ("mq", 512, F32), ("mk", 512, F32), ("mv", 512, BF16), ("mo", 512, F32),
                ("g", 128, F32))
    idx_w_scale = (IDX_HEADS ** -0.5) * (IDX_DIM ** -0.5)
    outs = pl.pallas_call(
        functools.partial(_inproj_kernel, idx_w_scale=idx_w_scale),
        grid=(t // tm,),
        in_specs=[pl.BlockSpec((tm, d), row), pl.BlockSpec((1, d), fixed),
                  pl.BlockSpec((d, PACKED_WIDTH), fixed), pl.BlockSpec((1, LANES), fixed),
                  tab, tab, tab, tab, tab, tab],
        out_specs=[pl.BlockSpec((tm, w), row) for _, w, _ in out_defs],
        out_shape=[jax.ShapeDtypeStruct((t, w), dt) for _, w, dt in out_defs],
        compiler_params=_cparams("parallel"),
        name="in_projection",
    )(x2, norm_w.reshape(1, d), wp, gb, ca, sa1, sa2, ci, si1, si2)
    return dict(zip([n for n, _, _ in out_defs], outs))


def _dsa_kernel(q_ref, iq_ref, iw_ref, k_ref, vt_ref, ik_ref, o_ref, score_ref, acc_ref, vk_ref, jcut_ref, p_ref,
                *, k_sel, index_iters):
    i = pl.program_id(1)
    n_keys = Q_BLOCK * (i + 1)
    n_steps = (n_keys + TILES_PER_STEP * KEY_TILE - 1) // (TILES_PER_STEP * KEY_TILE)

    def for_tiles(body, init, lead=None):
        def step(j, carry):
            tiles = [j * TILES_PER_STEP + u for u in range(TILES_PER_STEP)]
            opened = [lead(kt) if lead is not None else None for kt in tiles]
            for kt, head in zip(tiles, opened):
                carry = body(kt, carry) if lead is None else body(kt, head, carry)
            return carry
        return lax.fori_loop(0, n_steps, step, init)
    n_stack = ATT_HEADS * Q_BLOCK

    lane = lax.broadcasted_iota(jnp.int32, (Q_BLOCK, LANES), 1)
    iq = iq_ref[0]
    per_slab = LANES // IDX_DIM
    qi_stack = jnp.concatenate(
        [jnp.where((lane // IDX_DIM) == (h % per_slab), iq[:, (h // per_slab) * LANES:(h // per_slab + 1) * LANES], 0)
         for h in range(IDX_HEADS)], axis=0)
    q = q_ref[0]
    q_stack = jnp.concatenate(
        [jnp.where((lane // HEAD_DIM) == g, q[:, s * LANES:(s + 1) * LANES], 0)
         for s in range(Q_PER_KV) for g in range(ATT_KV_HEADS)], axis=0)
    w_t = iw_ref[0].T

    q_lane = lax.broadcasted_iota(jnp.int32, (1, Q_BLOCK), 1)
    key_limit = (2 * i + 1 + (q_lane >= CHUNK).astype(jnp.int32)) * CHUNK
    k_eff = jnp.minimum(key_limit, k_sel).astype(F32)
    key_iota = lax.broadcasted_iota(jnp.int32, (KEY_TILE, Q_BLOCK), 0)
    nt_dims = (((1,), (1,)), ((), ()))

    def idx_lead(kt):
        start = pl.multiple_of(kt * KEY_TILE, KEY_TILE)
        return lax.dot_general(ik_ref[0, pl.ds(start, KEY_TILE), :], qi_stack, nt_dims,
                               preferred_element_type=F32)

    def idx_body(kt, z, carry):
        rmin, rmax = carry
        start = pl.multiple_of(kt * KEY_TILE, KEY_TILE)
        sc = jnp.zeros((KEY_TILE, Q_BLOCK), F32)
        for h in range(IDX_HEADS):
            sc = sc + jnp.maximum(z[:, h * Q_BLOCK:(h + 1) * Q_BLOCK], 0.0) * w_t[h:h + 1, :]
        adm = (key_iota + start) < key_limit
        score_ref[pl.ds(start, KEY_TILE), :] = jnp.where(adm, sc, -jnp.inf)
        rmin = jnp.minimum(rmin, jnp.min(jnp.where(adm, sc, jnp.inf), axis=0, keepdims=True))
        rmax = jnp.maximum(rmax, jnp.max(jnp.where(adm, sc, -jnp.inf), axis=0, keepdims=True))
        return rmin, rmax

    init = (jnp.full((1, Q_BLOCK), jnp.inf, F32), jnp.full((1, Q_BLOCK), -jnp.inf, F32))
    lo, hi = for_tiles(idx_body, init, lead=idx_lead)

    def fold(x):
        return x.reshape(KEY_TILE // FOLD_ROWS, FOLD_ROWS, Q_BLOCK)

    def count(pred):
        def body(kt, acc):
            start = pl.multiple_of(kt * KEY_TILE, KEY_TILE)
            hit = pred(score_ref[pl.ds(start, KEY_TILE), :], key_iota + start).astype(F32)
            return acc + jnp.sum(fold(hit), axis=0)
        acc = for_tiles(body, jnp.zeros((FOLD_ROWS, Q_BLOCK), F32))
        return jnp.sum(acc, axis=0, keepdims=True)

    def unresolved(c_lo):
        return jnp.max(c_lo - k_eff) > 0.0

    def smallest_at_least(lo):
        def min_body(kt, acc):
            start = pl.multiple_of(kt * KEY_TILE, KEY_TILE)
            x = score_ref[pl.ds(start, KEY_TILE), :]
            return jnp.minimum(acc, jnp.min(fold(jnp.where(x >= lo, x, jnp.inf)), axis=0))
        return jnp.min(for_tiles(min_body, jnp.full((FOLD_ROWS, Q_BLOCK), jnp.inf, F32)), axis=0, keepdims=True)

    def bisect(carry):
        it, lo, hi, c_lo, _ = carry
        for _ in range(BISECT_GROUP):
            mid = 0.5 * (lo + hi)
            c_mid = count(lambda x, _: x >= mid)
            ok = c_mid >= k_eff
            lo, hi, c_lo = jnp.where(ok, mid, lo), jnp.where(ok, hi, mid), jnp.where(ok, c_mid, c_lo)
        it = it + BISECT_GROUP

        def kth_isolated(_):
            kth = smallest_at_least(lo)
            above = count(lambda x, _: x > kth)
            still_open = (c_lo > k_eff) & (above >= k_eff)
            return (jnp.max(still_open.astype(F32)) == 0.0).astype(jnp.int32)

        settled = lax.cond((it >= TIE_CHECK_FROM) & unresolved(c_lo), kth_isolated, lambda _: jnp.int32(0), 0)
        return it, lo, hi, c_lo, settled

    _, lo, _, c_lo, _ = lax.while_loop(
        lambda c: (c[0] < BISECT_ITERS) & unresolved(c[3]) & (c[4] == 0), bisect,
        (jnp.int32(0), lo, hi, key_limit.astype(F32), jnp.int32(0)))

    vk_ref[...] = jnp.broadcast_to(lo, vk_ref.shape)
    jcut_ref[...] = jnp.full(jcut_ref.shape, jnp.iinfo(jnp.int32).max, jnp.int32)

    @pl.when(unresolved(c_lo))
    def _():
        vk = smallest_at_least(lo)
        need = k_eff - count(lambda x, _: x > vk)

        def jbisect(_, carry):
            jlo, jhi = carry
            jmid = jnp.right_shift(jlo + jhi, 1)
            ok = count(lambda x, kidx: (x == vk) & (kidx < jmid)) >= need
            return jnp.where(ok, jlo, jmid), jnp.where(ok, jmid, jhi)

        j0 = (jnp.zeros((1, Q_BLOCK), jnp.int32), jnp.full((1, Q_BLOCK), n_keys, jnp.int32))
        _, jcut = lax.fori_loop(0, index_iters, jbisect, j0)
        vk_ref[...] = jnp.broadcast_to(vk, vk_ref.shape)
        jcut_ref[...] = jnp.broadcast_to(jcut, jcut_ref.shape)

    vk = vk_ref[0:1, :]
    jcut = jcut_ref[0:1, :]

    r_i = lax.broadcasted_iota(jnp.int32, (Q_BLOCK, LANES), 0)
    eye = jnp.where(r_i == lane, 1.0, 0.0).astype(BF16)
    q_aug = jnp.concatenate([q_stack, jnp.concatenate([eye] * ATT_HEADS, axis=0)], axis=1)
    acc_ref[...] = jnp.zeros_like(acc_ref)

    def att_lead(kt):
        start = pl.multiple_of(kt * KEY_TILE, KEY_TILE)
        x = score_ref[pl.ds(start, KEY_TILE), :]
        sel = (x > vk) | ((x == vk) & ((key_iota + start) < jcut))
        bias = jnp.where(sel, 0.0, NEG_BIG).astype(BF16)
        k_aug = jnp.concatenate([k_ref[0, pl.ds(start, KEY_TILE), :], bias], axis=1)
        return lax.dot_general(k_aug, q_aug, nt_dims, preferred_element_type=F32)

    def accumulate(kt, alpha):
        acc_ref[...] = alpha * acc_ref[...] + jnp.dot(vt_ref[0, kt], p_ref[...], preferred_element_type=F32)

    def att_body(kt, s, carry):
        m, alpha = carry
        accumulate(jnp.maximum(kt - 1, 0), alpha)
        m_new = jnp.maximum(m, jnp.max(s, axis=0, keepdims=True))
        p_ref[...] = jnp.exp2(s - m_new).astype(BF16)
        return m_new, jnp.exp2(m - m_new)

    p_ref[...] = jnp.zeros_like(p_ref)
    init = (jnp.full((1, n_stack), NEG_BIG, F32), jnp.ones((1, n_stack), F32))
    _, alpha = for_tiles(att_body, init, lead=att_lead)
    accumulate(TILES_PER_STEP * n_steps - 1, alpha)

    out_t = acc_ref[0:LANES, :] / acc_ref[LANES:LANES + 1, :]
    for s in range(Q_PER_KV):
        o0 = out_t[:, (2 * s) * Q_BLOCK:(2 * s + 1) * Q_BLOCK].T
        o1 = out_t[:, (2 * s + 1) * Q_BLOCK:(2 * s + 2) * Q_BLOCK].T
        o_ref[0, :, s * LANES:(s + 1) * LANES] = jnp.where(lane < HEAD_DIM, o0, o1).astype(o_ref.dtype)


def _dsa_attention(q, iq, iw, k, v, ik, k_sel):
    bsz, seq, _ = q.shape
    assert seq % (TILES_PER_STEP * KEY_TILE) == 0 and seq % Q_BLOCK == 0
    n_kt = seq // KEY_TILE
    vt = v.reshape(bsz, n_kt, KEY_TILE, LANES).transpose(0, 1, 3, 2)
    ones_rows = jnp.zeros((bsz, n_kt, V_ROWS - LANES, KEY_TILE), v.dtype).at[:, :, 0, :].set(1.0)
    vt = jnp.concatenate([vt, ones_rows], axis=2)
    blk = lambda w: pl.BlockSpec((1, Q_BLOCK, w), lambda b, i: (b, i, 0))
    full = pl.BlockSpec((1, seq, LANES), lambda b, i: (b, 0, 0))
    return pl.pallas_call(
        functools.partial(_dsa_kernel, k_sel=k_sel, index_iters=seq.bit_length()),
        grid=(bsz, seq // Q_BLOCK),
        in_specs=[blk(ATT_WIDTH), blk(IDX_HEADS * IDX_DIM), blk(LANES), full,
                  pl.BlockSpec((1, n_kt, V_ROWS, KEY_TILE), lambda b, i: (b, 0, 0, 0)), full],
        out_specs=blk(ATT_WIDTH),
        out_shape=jax.ShapeDtypeStruct((bsz, seq, ATT_WIDTH), BF16),
        scratch_shapes=[pltpu.VMEM((seq, Q_BLOCK), F32), pltpu.VMEM((V_ROWS, ATT_HEADS * Q_BLOCK), F32),
                        pltpu.VMEM((SUBLANES, Q_BLOCK), F32), pltpu.VMEM((SUBLANES, Q_BLOCK), jnp.int32),
                        pltpu.VMEM((KEY_TILE, ATT_HEADS * Q_BLOCK), BF16)],
        compiler_params=_cparams("parallel", "parallel"),
        name="dsa_attention",
    )(q, iq, iw, k, vt, ik)


def _mlstm_kernel(mq_ref, mk_ref, mv_ref, mo_ref, gc_ref, gr_ref, cw_ref, cb_ref, nw_ref, o_ref,
                  c_ref, n_ref, m_ref, prev_ref):
    L = MLSTM_CHUNK
    W = MLSTM_WIDTH
    dh = MLSTM_HEAD_DIM

    @pl.when(pl.program_id(1) == 0)
    def _():
        c_ref[...] = jnp.zeros_like(c_ref)
        n_ref[...] = jnp.zeros_like(n_ref)
        m_ref[...] = jnp.zeros_like(m_ref)
        prev_ref[...] = jnp.zeros_like(prev_ref)

    row = lax.broadcasted_iota(jnp.int32, (SUBLANES, W), 0)

    def conv_silu(x, prev8, w4, b):
        y = x * w4[CONV_WIDTH - 1:CONV_WIDTH, :] + b
        for d in range(1, CONV_WIDTH):
            xr = pltpu.roll(x, d, 0)
            head = jnp.where(row < d, pltpu.roll(prev8, d, 0), xr[:SUBLANES])
            xs = jnp.concatenate([head, xr[SUBLANES:]], axis=0)
            y = y + xs * w4[CONV_WIDTH - 1 - d:CONV_WIDTH - d, :]
        return y / (1.0 + jnp.exp(-y))

    xq, xk = mq_ref[0], mk_ref[0]
    cw, cb = cw_ref[...], cb_ref[...]
    q = conv_silu(xq, prev_ref[:, :W], cw[:, :W], cb[:, :W])
    k = conv_silu(xk, prev_ref[:, W:], cw[:, W:], cb[:, W:]) * (dh ** -0.5)
    prev_ref[:, :W] = xq[L - SUBLANES:, :]
    prev_ref[:, W:] = xk[L - SUBLANES:, :]

    g_col = gc_ref[0]
    g_row = gr_ref[0]
    r_i = lax.broadcasted_iota(jnp.int32, (L, L), 0)
    c_i = lax.broadcasted_iota(jnp.int32, (L, L), 1)
    causal = r_i >= c_i
    tri = causal.astype(F32)
    b_col_all = jnp.dot(tri, g_col, preferred_element_type=F32, precision=lax.Precision.HIGHEST)
    b_row_all = jnp.dot(g_row, (r_i <= c_i).astype(F32), preferred_element_type=F32,
                        precision=lax.Precision.HIGHEST)

    v = mv_ref[0]
    mo = mo_ref[0]
    nw = nw_ref[...]
    for h in range(MLSTM_HEADS):
        hs = slice(h * dh, (h + 1) * dh)
        i_col = _lane_col(g_col, h)
        i_row = g_row[h:h + 1, :]
        b_col = _lane_col(b_col_all, MLSTM_HEADS + h)
        b_row = b_row_all[MLSTM_HEADS + h:MLSTM_HEADS + h + 1, :]
        b_last = jnp.sum(g_row[MLSTM_HEADS + h:MLSTM_HEADS + h + 1, :], axis=1, keepdims=True)
        m_prev = m_ref[h:h + 1, 0:1]
        c_mat = c_ref[h]
        n_vec = n_ref[h:h + 1, :]

        d_log = jnp.where(causal, b_col + (i_row - b_row), -jnp.inf)
        inter_log = b_col + m_prev
        m_t = jnp.maximum(inter_log, jnp.max(d_log, axis=1, keepdims=True))
        d_w = jnp.exp(d_log - m_t)
        inter_w = jnp.exp(inter_log - m_t)
        qh, kh, vh = q[:, hs], k[:, hs], v[:, hs]
        qb, kb = qh.astype(BF16), kh.astype(BF16)
        s = lax.dot_general(qb, kb, (((1,), (1,)), ((), ())), preferred_element_type=F32) * d_w
        num = (jnp.dot(s.astype(BF16), vh, preferred_element_type=F32)
               + inter_w * jnp.dot(qb, c_mat.astype(BF16), preferred_element_type=F32))
        den = jnp.sum(s, axis=1, keepdims=True) + inter_w * jnp.sum(qh * n_vec, axis=1, keepdims=True)
        hout = num / jnp.maximum(jnp.abs(den), jnp.exp(-m_t))

        w_log = b_last - b_col + i_col
        m_new = jnp.maximum(b_last + m_prev, jnp.max(w_log, axis=0, keepdims=True))
        decay = jnp.exp(b_last + m_prev - m_new)
        kw = kh * jnp.exp(w_log - m_new)
        c_ref[h] = decay * c_mat + jnp.dot(kw.T.astype(BF16), vh, preferred_element_type=F32)
        n_ref[h:h + 1, :] = decay * n_vec + jnp.sum(kw, axis=0, keepdims=True)
        m_ref[h:h + 1, :] = jnp.broadcast_to(m_new, (1, LANES))

        hn = hout * lax.rsqrt(jnp.mean(hout * hout, axis=-1, keepdims=True) + NORM_EPS)
        gate = 1.0 / (1.0 + jnp.exp(-mo[:, hs]))
        o_ref[0, :, hs] = (hn * nw[:, hs] * gate).astype(o_ref.dtype)


def _mlstm_mixer(mq, mk, mv, mo, g, conv_w, conv_b, norm_w):
    bsz, seq, w = mq.shape
    L = MLSTM_CHUNK
    assert seq % L == 0
    g_row = g[:, :, :SUBLANES].transpose(0, 2, 1)
    blk = lambda width: pl.BlockSpec((1, L, width), lambda b, c: (b, c, 0))
    fixed = lambda shape: pl.BlockSpec(shape, lambda b, c: (0, 0))
    return pl.pallas_call(
        _mlstm_kernel,
        grid=(bsz, seq // L),
        in_specs=[blk(w), blk(w), blk(w), blk(w), blk(LANES),
                  pl.BlockSpec((1, SUBLANES, L), lambda b, c: (b, 0, c)),
                  fixed((CONV_WIDTH, 2 * w)), fixed((1, 2 * w)), fixed((1, w))],
        out_specs=blk(w),
        out_shape=jax.ShapeDtypeStruct((bsz, seq, w), BF16),
        scratch_shapes=[pltpu.VMEM((MLSTM_HEADS, MLSTM_HEAD_DIM, MLSTM_HEAD_DIM), F32),
                        pltpu.VMEM((SUBLANES, MLSTM_HEAD_DIM), F32),
                        pltpu.VMEM((SUBLANES, LANES), F32),
                        pltpu.VMEM((SUBLANES, 2 * w), F32)],
        compiler_params=_cparams("parallel", "arbitrary"),
        name="mlstm_mixer",
    )(mq, mk, mv, mo, g, g_row, conv_w, conv_b.reshape(1, 2 * w), norm_w.reshape(1, w))


def _outproj_router_kernel(att_ref, mem_ref, x_ref, wa_ref, wm_ref, nw_ref, rw_ref, rb_ref,
                           x1_ref, hn_ref, ti_ref, tg_ref, cnt_ref):
    y = (jnp.dot(att_ref[...], wa_ref[...], preferred_element_type=F32)
         + jnp.dot(mem_ref[...], wm_ref[...], preferred_element_type=F32))
    x1 = x_ref[...] + y
    x1_ref[...] = x1
    var = jnp.mean(x1 * x1, axis=-1, keepdims=True)
    hn = (x1 * lax.rsqrt(var + NORM_EPS)) * nw_ref[...]
    hn_ref[...] = hn
    hn_hi = hn.astype(BF16)
    hn_lo = (hn - hn_hi.astype(F32)).astype(BF16)
    rw = rw_ref[...]
    rw_hi = rw.astype(BF16)
    rw_lo = (rw - rw_hi.astype(F32)).astype(BF16)
    part = jnp.dot(hn_hi, jnp.concatenate([rw_hi, rw_lo], axis=1), preferred_element_type=F32)
    logits = (part[:, :LANES] + part[:, LANES:]
              + jnp.dot(hn_lo, rw_hi, preferred_element_type=F32) + rb_ref[...])
    lane_i = lax.broadcasted_iota(jnp.int32, logits.shape, 1)
    lane = lane_i.astype(F32)
    cur = jnp.where(lane_i < N_EXPERTS, logits, -jnp.inf)
    vals, idxs = [], []
    for _ in range(TOP_K):
        mx = jnp.max(cur, axis=1, keepdims=True)
        am = jnp.min(jnp.where(cur == mx, lane, float(LANES)), axis=1, keepdims=True)
        vals.append(mx)
        idxs.append(am)
        cur = jnp.where(lane == am, -jnp.inf, cur)
    exps = [jnp.exp(vk - vals[0]) for vk in vals]
    denom = exps[0]
    for e in exps[1:]:
        denom = denom + e
    @pl.when(pl.program_id(0) == 0)
    def _():
        cnt_ref[...] = jnp.zeros_like(cnt_ref)

    tm = logits.shape[0]
    picks = [lane == am for am in idxs]
    picked = jnp.zeros(logits.shape, F32)
    for pk in picks:
        picked = picked + jnp.where(pk, 1.0, 0.0)
    r_i = lax.broadcasted_iota(jnp.int32, (tm, tm), 0)
    c_i = lax.broadcasted_iota(jnp.int32, (tm, tm), 1)
    before = jnp.dot(jnp.where(c_i < r_i, 1.0, 0.0).astype(BF16), picked.astype(BF16), preferred_element_type=F32)
    base = before + cnt_ref[0:1, :]
    cnt_ref[...] = cnt_ref[...] + jnp.sum(picked, axis=0, keepdims=True)

    ti = jnp.zeros(logits.shape, F32)
    tg = jnp.zeros(logits.shape, F32)
    for kk in range(TOP_K):
        rank = jnp.sum(jnp.where(picks[kk], base, 0.0), axis=1, keepdims=True)
        ti = jnp.where(lane_i == kk, idxs[kk], ti)
        ti = jnp.where(lane_i == TOP_K + kk, rank, ti)
        tg = jnp.where(lane_i == kk, exps[kk] / denom, tg)
    ti_ref[...] = ti.astype(jnp.int32)
    tg_ref[...] = tg


def _outproj_router(att, mem, x2, w_att, w_mem, ffn_norm_w, router_w, router_b, tm=256):
    t, d = x2.shape
    row = lambda i: (i, 0)
    fixed = lambda i: (0, 0)
    rw = jnp.pad(router_w, ((0, 0), (0, LANES - N_EXPERTS)))
    rb = jnp.pad(router_b, (0, LANES - N_EXPERTS)).reshape(1, LANES)
    return pl.pallas_call(
        _outproj_router_kernel,
        grid=(t // tm,),
        in_specs=[pl.BlockSpec((tm, ATT_WIDTH), row), pl.BlockSpec((tm, MLSTM_WIDTH), row),
                  pl.BlockSpec((tm, d), row), pl.BlockSpec((ATT_WIDTH, d), fixed),
                  pl.BlockSpec((MLSTM_WIDTH, d), fixed), pl.BlockSpec((1, d), fixed),
                  pl.BlockSpec((d, LANES), fixed), pl.BlockSpec((1, LANES), fixed)],
        out_specs=[pl.BlockSpec((tm, d), row), pl.BlockSpec((tm, d), row),
                   pl.BlockSpec((tm, LANES), row), pl.BlockSpec((tm, LANES), row),
                   pl.BlockSpec((SUBLANES, LANES), fixed)],
        out_shape=[jax.ShapeDtypeStruct((t, d), F32), jax.ShapeDtypeStruct((t, d), F32),
                   jax.ShapeDtypeStruct((t, LANES), jnp.int32), jax.ShapeDtypeStruct((t, LANES), F32),
                   jax.ShapeDtypeStruct((SUBLANES, LANES), F32)],
        compiler_params=_cparams("arbitrary"),
        name="outproj_router",
    )(att, mem, x2, w_att, w_mem, ffn_norm_w.reshape(1, d), rw, rb)


DISPATCH_ROWS = 256


def _dispatch_kernel(fill_ref, nused_ref, dest_ref, hn_ref, x_hbm, stage, zero_buf, sem, zsem, bsem):
    n_blocks = x_hbm.shape[0] // EXPERT_ROWS
    step = pl.program_id(0)
    slot = step % 2
    stage[slot] = hn_ref[...]

    def token_row(j, kk):
        return pltpu.make_async_copy(stage.at[slot, pl.ds(j, 1), :],
                                     x_hbm.at[pl.ds(dest_ref[0, 0, j * TOP_K + kk], 1), :], sem.at[slot])

    def wait_tokens(which):
        all_rows = x_hbm.at[pl.ds(0, TOP_K * DISPATCH_ROWS), :]
        pltpu.make_async_copy(all_rows, all_rows, sem.at[which]).wait()

    def pad_row(e, r):
        return pltpu.make_async_copy(zero_buf.at[pl.ds(0, 1), :], x_hbm.at[pl.ds(fill_ref[0, e] + r, 1), :], zsem)

    def pad_block(b):
        start = pl.multiple_of(b * EXPERT_ROWS, EXPERT_ROWS)
        return pltpu.make_async_copy(zero_buf, x_hbm.at[pl.ds(start, EXPERT_ROWS), :], bsem)

    def start_tokens(j, c):
        for kk in range(TOP_K):
            token_row(j, kk).start()
        return c

    def for_padding(fn):
        def per_expert(e, c):
            lax.fori_loop(0, fill_ref[1, e], lambda r, cc: (fn(pad_row(e, r)), cc)[1], 0)
            return c
        lax.fori_loop(0, N_EXPERTS, per_expert, 0)
        lax.fori_loop(nused_ref[0], n_blocks, lambda b, cc: (fn(pad_block(b)), cc)[1], 0)

    lax.fori_loop(0, DISPATCH_ROWS, start_tokens, 0)

    @pl.when(pl.program_id(0) == 0)
    def _():
        zero_buf[...] = jnp.zeros_like(zero_buf)
        for_padding(lambda cp: cp.start())
        for_padding(lambda cp: cp.wait())

    @pl.when(step > 0)
    def _():
        wait_tokens(1 - slot)

    @pl.when(step == pl.num_programs(0) - 1)
    def _():
        wait_tokens(slot)


def _dispatch_rows(hn, dest, fill, n_blocks_used, n_rows):
    t, d = hn.shape
    tm = DISPATCH_ROWS
    nb = t // tm
    return pl.pallas_call(
        _dispatch_kernel,
        grid_spec=pltpu.PrefetchScalarGridSpec(
            num_scalar_prefetch=2,
            grid=(nb,),
            in_specs=[pl.BlockSpec((1, 1, tm * TOP_K), lambda i, f, n: (i, 0, 0), memory_space=pltpu.SMEM),
                      pl.BlockSpec((tm, d), lambda i, f, n: (i, 0))],
            out_specs=pl.BlockSpec(memory_space=pl.ANY),
            scratch_shapes=[pltpu.VMEM((2, tm, d), hn.dtype), pltpu.VMEM((EXPERT_ROWS, d), hn.dtype),
                            pltpu.SemaphoreType.DMA((2,)), pltpu.SemaphoreType.DMA(()),
                            pltpu.SemaphoreType.DMA(())],
        ),
        out_shape=jax.ShapeDtypeStruct((n_rows, d), hn.dtype),
        compiler_params=_cparams("arbitrary"),
        name="moe_dispatch",
    )(fill, n_blocks_used, dest.reshape(nb, 1, tm * TOP_K), hn)


def _expert_kernel(be_ref, nu_ref, nxt_ref, par_ref, x_ref, bgu_ref, bdn_ref, wgu_hbm, wdn_hbm, o_ref,
                   wgu_f, wdn_f, wgu_s, wdn_s, sem):
    b = pl.program_id(0)
    e = be_ref[b]
    e_prev = be_ref[jnp.maximum(b - 1, 0)]
    slot = par_ref[b]
    used = b < nu_ref[0]
    d_e = wdn_s.shape[0]

    def fetch(expert, into):
        return (pltpu.make_async_copy(wgu_hbm.at[expert], wgu_f.at[into], sem.at[0, into]),
                pltpu.make_async_copy(wdn_hbm.at[expert], wdn_f.at[into], sem.at[1, into]))

    @pl.when(b == 0)
    def _():
        for cp in fetch(e, slot):
            cp.start()

    @pl.when(used & ((b == 0) | (e != e_prev)))
    def _():
        for cp in fetch(e, slot):
            cp.wait()

        @pl.when(nxt_ref[b] >= 0)
        def _():
            for cp in fetch(nxt_ref[b], 1 - slot):
                cp.start()

        wgu_s[...] = wgu_f[slot].astype(BF16)
        wdn_s[...] = wdn_f[slot].astype(BF16)

    @pl.when(used)
    def _():
        xb = x_ref[...].astype(BF16)
        bgu = bgu_ref[0]
        glu = jnp.dot(xb, wgu_s[:, :d_e], preferred_element_type=F32) + bgu[:, :d_e]
        lin = jnp.dot(xb, wgu_s[:, d_e:], preferred_element_type=F32) + bgu[:, d_e:]
        glu = jnp.minimum(glu, SWIGLU_LIMIT)
        lin = jnp.clip(lin, -SWIGLU_LIMIT, SWIGLU_LIMIT)
        act = glu * (1.0 / (1.0 + jnp.exp(-SWIGLU_ALPHA * glu))) * (lin + 1.0)
        o_ref[...] = jnp.dot(act.astype(BF16), wdn_s[...], preferred_element_type=F32) + bdn_ref[0]

    @pl.when(jnp.logical_not(used))
    def _():
        o_ref[...] = jnp.zeros_like(o_ref)


def _expert_ffn(x_buf, blk_expert, n_blocks_used, next_expert, run_parity, w_gu, b_gu, w_dn, b_dn):
    n_rows, d = x_buf.shape
    ne, _, d2 = w_gu.shape
    d_e = w_dn.shape[1]
    nb = n_rows // EXPERT_ROWS
    last_used = lambda b, be, nu, nx, pr: (jnp.minimum(b, jnp.maximum(nu[0] - 1, 0)), 0)
    by_expert = lambda b, be, nu, nx, pr: (be[b], 0, 0)
    return pl.pallas_call(
        _expert_kernel,
        grid_spec=pltpu.PrefetchScalarGridSpec(
            num_scalar_prefetch=4,
            grid=(nb,),
            in_specs=[pl.BlockSpec((EXPERT_ROWS, d), last_used),
                      pl.BlockSpec((1, 1, d2), by_expert),
                      pl.BlockSpec((1, 1, d), by_expert),
                      pl.BlockSpec(memory_space=pl.ANY),
                      pl.BlockSpec(memory_space=pl.ANY)],
            out_specs=pl.BlockSpec((EXPERT_ROWS, d), lambda b, be, nu, nx, pr: (b, 0)),
            scratch_shapes=[pltpu.VMEM((2, d, d2), w_gu.dtype), pltpu.VMEM((2, d_e, d), w_dn.dtype),
                            pltpu.VMEM((d, d2), BF16), pltpu.VMEM((d_e, d), BF16),
                            pltpu.SemaphoreType.DMA((2, 2))],
        ),
        out_shape=jax.ShapeDtypeStruct((n_rows, d), F32),
        compiler_params=_cparams("arbitrary"),
        name="moe_experts",
    )(blk_expert, n_blocks_used, next_expert, run_parity, x_buf, b_gu.reshape(ne, 1, d2), b_dn.reshape(ne, 1, d),
      w_gu, w_dn)


COMBINE_ROWS = 256


def _combine_kernel(pos_ref, pos_next_ref, g_ref, x1_ref, nw_ref, y_hbm, o_ref, buf, sem, *, apply_norm):
    tm = COMBINE_ROWS
    step = pl.program_id(0)
    slot = step % 2

    def gather(p_ref, into):
        def issue(j, c):
            for kk in range(TOP_K):
                pltpu.make_async_copy(y_hbm.at[pl.ds(p_ref[0, 0, j * TOP_K + kk], 1), :],
                                      buf.at[into, pl.ds(kk * tm + j, 1), :], sem.at[into]).start()
            return c
        lax.fori_loop(0, tm, issue, 0)

    @pl.when(step == 0)
    def _():
        gather(pos_ref, 0)

    @pl.when(step + 1 < pl.num_programs(0))
    def _():
        gather(pos_next_ref, 1 - slot)

    pltpu.make_async_copy(y_hbm.at[pl.ds(0, TOP_K * tm), :], buf.at[slot], sem.at[slot]).wait()

    g = g_ref[...]
    acc = x1_ref[...]
    for kk in range(TOP_K):
        acc = acc + buf[slot, pl.ds(kk * tm, tm), :] * _lane_col(g, kk)
    if apply_norm:
        var = jnp.mean(acc * acc, axis=-1, keepdims=True)
        acc = (acc * lax.rsqrt(var + NORM_EPS)) * nw_ref[...]
    o_ref[...] = acc


def _combine(y_buf, pos, gates, x1, final_norm_w, apply_norm):
    t, d = x1.shape
    tm = COMBINE_ROWS
    nb = t // tm
    row = lambda i: (i, 0)
    pos3 = pos.reshape(nb, 1, tm * TOP_K)
    return pl.pallas_call(
        functools.partial(_combine_kernel, apply_norm=apply_norm),
        grid=(nb,),
        in_specs=[pl.BlockSpec((1, 1, tm * TOP_K), lambda i: (i, 0, 0), memory_space=pltpu.SMEM),
                  pl.BlockSpec((1, 1, tm * TOP_K), lambda i: (jnp.minimum(i + 1, nb - 1), 0, 0),
                               memory_space=pltpu.SMEM),
                  pl.BlockSpec((tm, LANES), row), pl.BlockSpec((tm, d), row),
                  pl.BlockSpec((1, d), lambda i: (0, 0)), pl.BlockSpec(memory_space=pl.ANY)],
        out_specs=pl.BlockSpec((tm, d), row),
        out_shape=jax.ShapeDtypeStruct((t, d), F32),
        scratch_shapes=[pltpu.VMEM((2, TOP_K * tm, d), F32), pltpu.SemaphoreType.DMA((2,))],
        compiler_params=_cparams("arbitrary"),
        name="moe_combine",
    )(pos3, pos3, gates, x1, final_norm_w.reshape(1, d), y_buf)


def _routing_tables(top_idx, rank, counts, n_tokens):
    padded = (counts + EXPERT_ROWS - 1) // EXPERT_ROWS * EXPERT_ROWS
    pad_end = jnp.cumsum(padded)
    pad_start = pad_end - padded
    dest = (pad_start[top_idx.reshape(-1)] + rank.reshape(-1)).astype(jnp.int32)
    n_rows = n_tokens * TOP_K + N_EXPERTS * EXPERT_ROWS
    n_blocks = n_rows // EXPERT_ROWS
    blk_start = jnp.arange(n_blocks, dtype=jnp.int32) * EXPERT_ROWS
    blk_expert = jnp.minimum(jnp.sum((pad_end[None, :] <= blk_start[:, None]).astype(jnp.int32), axis=1),
                             N_EXPERTS - 1).astype(jnp.int32)
    n_blocks_used = (pad_end[-1:] // EXPERT_ROWS).astype(jnp.int32)
    fill = jnp.stack([pad_start + counts, padded - counts]).astype(jnp.int32)
    first = jnp.concatenate([jnp.ones((1,), jnp.bool_), blk_expert[1:] != blk_expert[:-1]])
    run_parity = ((jnp.cumsum(first.astype(jnp.int32)) - 1) % 2).astype(jnp.int32)
    next_run = (pad_end[blk_expert] // EXPERT_ROWS).astype(jnp.int32)
    next_expert = jnp.where(next_run < n_blocks_used[0],
                            blk_expert[jnp.minimum(next_run, n_blocks - 1)], -1).astype(jnp.int32)
    return dest, fill, blk_expert, n_blocks_used, next_expert, run_parity, n_rows


def kernel(x, mix_norm_w, w_in, conv_w, conv_b, igate_b, fgate_b, mlstm_norm_w, w_out, ffn_norm_w,
           router_w, router_b, w_gate_up, b_gate_up, w_down, b_down, final_norm_w):
    bsz, seq, d = x.shape
    depth = w_in.shape[0]
    t = bsz * seq
    k_sel = min(TOPK_MAX, seq // 4)
    head_order = [h for s in range(Q_PER_KV) for h in (s, Q_PER_KV + s)]
    x2 = x.reshape(t, d)
    for layer in range(depth):
        p = _in_projection(x2, mix_norm_w[layer], w_in[layer], igate_b[layer], fgate_b[layer], seq)
        r3 = lambda a: a.reshape(bsz, seq, a.shape[-1])
        att = _dsa_attention(r3(p["q"]), r3(p["iq"]), r3(p["iw"]), r3(p["k"]), r3(p["v"]), r3(p["ik"]), k_sel)
        mem = _mlstm_mixer(r3(p["mq"]), r3(p["mk"]), r3(p["mv"]), r3(p["mo"]), r3(p["g"]),
                           conv_w[layer], conv_b[layer], mlstm_norm_w[layer])
        wo = w_out[layer]
        w_att = wo[:ATT_WIDTH].reshape(ATT_HEADS, HEAD_DIM, d)[jnp.asarray(head_order)].reshape(ATT_WIDTH, d)
        x1, hn, top_i, top_g, counts = _outproj_router(
            att.reshape(t, ATT_WIDTH), mem.reshape(t, MLSTM_WIDTH), x2, w_att.astype(BF16),
            wo[ATT_WIDTH:].astype(BF16), ffn_norm_w[layer], router_w[layer], router_b[layer])
        dest, fill, blk_expert, n_blocks_used, next_expert, run_parity, n_rows = _routing_tables(
            top_i[:, :TOP_K], top_i[:, TOP_K:2 * TOP_K], counts[0, :N_EXPERTS].astype(jnp.int32), t)
        x_buf = _dispatch_rows(hn, dest, fill, n_blocks_used, n_rows)
        y_buf = _expert_ffn(x_buf, blk_expert, n_blocks_used, next_expert, run_parity, w_gate_up[layer],
                            b_gate_up[layer], w_down[layer], b_down[layer])
        x2 = _combine(y_buf, dest, top_g, x1, final_norm_w, apply_norm=(layer == depth - 1))
    return x2.reshape(bsz, seq, d)
```

```python
import functools

import jax
import jax.numpy as jnp
import numpy as np
from jax import lax
from jax.experimental import pallas as pl
from jax.experimental.pallas import tpu as pltpu

F32 = jnp.float32
BF16 = jnp.bfloat16

LANES = 128
SUBLANES = 8
VMEM_LIMIT_BYTES = 56 * 1024 * 1024

NORM_EPS = 1e-6
ROPE_THETA = 500000.0
CHUNK = 64

HEAD_DIM = 64
ATT_HEADS = 8
ATT_KV_HEADS = 2
Q_PER_KV = ATT_HEADS // ATT_KV_HEADS
ATT_ROT_DIM = HEAD_DIM // 4
IDX_HEADS = 8
IDX_DIM = 32
IDX_ROT_DIM = IDX_DIM // 4
TOPK_MAX = 256
ATT_WIDTH = ATT_HEADS * HEAD_DIM

MLSTM_HEAD_DIM = 128
MLSTM_HEADS = 4
MLSTM_WIDTH = MLSTM_HEADS * MLSTM_HEAD_DIM
CONV_WIDTH = 4

N_EXPERTS = 32
TOP_K = 4
SWIGLU_ALPHA = 1.702
SWIGLU_LIMIT = 7.0

Q_BLOCK = 128
KEY_TILE = 256
BISECT_ITERS = 32
BISECT_GROUP = 4
FOLD_ROWS = 32
TILES_PER_STEP = 2
V_ROWS = LANES + 16
MLSTM_CHUNK = 256
EXPERT_ROWS = 512
NEG_BIG = -1e30


def _cparams(*sem):
    return pltpu.CompilerParams(dimension_semantics=sem, vmem_limit_bytes=VMEM_LIMIT_BYTES)


_SEG_WIDTHS = (("q", 512), ("k", 128), ("v", 128), ("iq", 256), ("ik", 128), ("iw", 128),
               ("mq", 512), ("mk", 512), ("mv", 512), ("mo", 512), ("g", 128))
_SEG = {}
_off = 0
for _name, _w in _SEG_WIDTHS:
    _SEG[_name] = (_off, _w)
    _off += _w
PACKED_WIDTH = _off


def _lane_col(x, idx):
    lane = lax.broadcasted_iota(jnp.int32, x.shape, 1)
    return jnp.sum(jnp.where(lane == idx, x, 0.0), axis=1, keepdims=True)


def _inproj_kernel(x_ref, nw_ref, w_ref, gb_ref, ca_ref, sa1_ref, sa2_ref, ci_ref, si1_ref, si2_ref,
                   q_ref, k_ref, v_ref, iq_ref, ik_ref, iw_ref, mq_ref, mk_ref, mv_ref, mo_ref, g_ref,
                   *, idx_w_scale):
    x = x_ref[...]
    var = jnp.mean(x * x, axis=-1, keepdims=True)
    h = (x * lax.rsqrt(var + NORM_EPS)) * nw_ref[...]
    hb = h.astype(BF16)

    def proj(name):
        lo, width = _SEG[name]
        return jnp.dot(hb, w_ref[:, lo:lo + width], preferred_element_type=F32)

    def rope(p, c_ref, s1_ref, s2_ref, half):
        c, s1, s2 = c_ref[...], s1_ref[...], s2_ref[...]
        outs = []
        for j in range(p.shape[1] // LANES):
            xs = p[:, j * LANES:(j + 1) * LANES]
            outs.append(xs * c + pltpu.roll(xs, LANES - half, 1) * s1 + pltpu.roll(xs, half, 1) * s2)
        return outs[0] if len(outs) == 1 else jnp.concatenate(outs, axis=1)

    att_scale = float(HEAD_DIM ** -0.5 * np.log2(np.e))
    q_ref[...] = (rope(proj("q"), ca_ref, sa1_ref, sa2_ref, ATT_ROT_DIM // 2) * att_scale).astype(BF16)
    k_ref[...] = rope(proj("k"), ca_ref, sa1_ref, sa2_ref, ATT_ROT_DIM // 2).astype(BF16)
    v_ref[...] = proj("v").astype(BF16)
    iq_ref[...] = rope(proj("iq"), ci_ref, si1_ref, si2_ref, IDX_ROT_DIM // 2).astype(BF16)
    ik_ref[...] = rope(proj("ik"), ci_ref, si1_ref, si2_ref, IDX_ROT_DIM // 2).astype(BF16)
    iw_ref[...] = proj("iw") * idx_w_scale
    mq_ref[...] = proj("mq")
    mk_ref[...] = proj("mk")
    mv_ref[...] = proj("mv").astype(BF16)
    mo_ref[...] = proj("mo")
    g = proj("g") + gb_ref[...]
    lane = lax.broadcasted_iota(jnp.int32, g.shape, 1)
    log_f = jnp.minimum(g, 0.0) - jnp.log(1.0 + jnp.exp(-jnp.abs(g)))
    g_ref[...] = jnp.where(lane < MLSTM_HEADS, g, log_f)


def _rope_tables(seq, rot_dim, head_dim):
    pos = jnp.arange(seq, dtype=F32)
    inv_freq = ROPE_THETA ** (-jnp.arange(0, rot_dim, 2, dtype=F32) / rot_dim)
    ang = pos[:, None] * inv_freq[None, :]
    cos, sin = jnp.cos(ang), jnp.sin(ang)
    half = rot_dim // 2
    lane = np.arange(LANES) % head_dim
    fidx = lane % half
    first = lane < half
    second = (lane >= half) & (lane < rot_dim)
    cos_l, sin_l = cos[:, fidx], sin[:, fidx]
    c = jnp.where(jnp.asarray(first | second)[None, :], cos_l, 1.0)
    s1 = jnp.where(jnp.asarray(first)[None, :], -sin_l, 0.0)
    s2 = jnp.where(jnp.asarray(second)[None, :], sin_l, 0.0)
    return c, s1, s2


def _pack_w_in(w_in):
    sizes = (ATT_WIDTH, ATT_KV_HEADS * HEAD_DIM, ATT_KV_HEADS * HEAD_DIM, IDX_HEADS * IDX_DIM, IDX_DIM,
             IDX_HEADS, MLSTM_WIDTH, MLSTM_WIDTH, MLSTM_WIDTH, MLSTM_HEADS, MLSTM_HEADS, MLSTM_WIDTH)
    offs = np.concatenate([[0], np.cumsum(sizes)])
    aq, ak, av, iq, ik, iw, mq, mk, mv, mi, mf, mo = (w_in[:, offs[i]:offs[i + 1]] for i in range(12))
    d = w_in.shape[0]
    head_order = [h for s in range(Q_PER_KV) for h in (s, Q_PER_KV + s)]
    aq = aq.reshape(d, ATT_HEADS, HEAD_DIM)[:, head_order, :].reshape(d, ATT_WIDTH)
    ik_rep = jnp.tile(ik, (1, LANES // IDX_DIM))
    iw_pad = jnp.pad(iw, ((0, 0), (0, LANES - IDX_HEADS)))
    g_pad = jnp.pad(jnp.concatenate([mi, mf], axis=1), ((0, 0), (0, LANES - 2 * MLSTM_HEADS)))
    packed = jnp.concatenate([aq, ak, av, iq, ik_rep, iw_pad, mq, mk, mv, mo, g_pad], axis=1)
    assert packed.shape[1] == PACKED_WIDTH
    return packed.astype(BF16)


def _in_projection(x2, norm_w, w_in, igate_b, fgate_b, seq, tm=1024):
    t, d = x2.shape
    assert t % tm == 0 and seq % tm == 0
    wp = _pack_w_in(w_in)
    gb = jnp.pad(jnp.concatenate([igate_b, fgate_b]), (0, LANES - 2 * MLSTM_HEADS)).reshape(1, LANES).astype(F32)
    ca, sa1, sa2 = _rope_tables(seq, ATT_ROT_DIM, HEAD_DIM)
    ci, si1, si2 = _rope_tables(seq, IDX_ROT_DIM, IDX_DIM)
    n_pos_blocks = seq // tm
    row = lambda i: (i, 0)
    fixed = lambda i: (0, 0)
    pos = lambda i: (i % n_pos_blocks, 0)
    tab = pl.BlockSpec((tm, LANES), pos)
    out_defs = (("q", 512, BF16), ("k", 128, BF16), ("v", 128, BF16), ("iq", 256, BF16), ("ik", 128, BF16),
                ("iw", 128, F32), ("mq", 512, F32), ("mk", 512, F32), ("mv", 512, BF16), ("mo", 512, F32),
                ("g", 128, F32))
    idx_w_scale = (IDX_HEADS ** -0.5) * (IDX_DIM ** -0.5)
    outs = pl.pallas_call(
        functools.partial(_inproj_kernel, idx_w_scale=idx_w_scale),
        grid=(t // tm,),
        in_specs=[pl.BlockSpec((tm, d), row), pl.BlockSpec((1, d), fixed),
                  pl.BlockSpec((d, PACKED_WIDTH), fixed), pl.BlockSpec((1, LANES), fixed),
                  tab, tab, tab, tab, tab, tab],
        out_specs=[pl.BlockSpec((tm, w), row) for _, w, _ in out_defs],
        out_shape=[jax.ShapeDtypeStruct((t, w), dt) for _, w, dt in out_defs],
        compiler_params=_cparams("parallel"),
        name="in_projection",
    )(x2, norm_w.reshape(1, d), wp, gb, ca, sa1, sa2, ci, si1, si2)
    return dict(zip([n for n, _, _ in out_defs], outs))


def _dsa_kernel(q_ref, iq_ref, iw_ref, k_ref, vt_ref, ik_ref, o_ref, score_ref, acc_ref, vk_ref, jcut_ref, p_ref,
                *, k_sel, index_iters):
    i = pl.program_id(1)
    n_keys = Q_BLOCK * (i + 1)
    n_steps = (n_keys + TILES_PER_STEP * KEY_TILE - 1) // (TILES_PER_STEP * KEY_TILE)

    def for_tiles(body, init, lead=None):
        def step(j, carry):
            tiles = [j * TILES_PER_STEP + u for u in range(TILES_PER_STEP)]
            opened = [lead(kt) if lead is not None else None for kt in tiles]
            for kt, head in zip(tiles, opened):
                carry = body(kt, carry) if lead is None else body(kt, head, carry)
            return carry
        return lax.fori_loop(0, n_steps, step, init)
    n_stack = ATT_HEADS * Q_BLOCK

    lane = lax.broadcasted_iota(jnp.int32, (Q_BLOCK, LANES), 1)
    iq = iq_ref[0]
    per_slab = LANES // IDX_DIM
    qi_stack = jnp.concatenate(
        [jnp.where((lane // IDX_DIM) == (h % per_slab), iq[:, (h // per_slab) * LANES:(h // per_slab + 1) * LANES], 0)
         for h in range(IDX_HEADS)], axis=0)
    q = q_ref[0]
    q_stack = jnp.concatenate(
        [jnp.where((lane // HEAD_DIM) == g, q[:, s * LANES:(s + 1) * LANES], 0)
         for s in range(Q_PER_KV) for g in range(ATT_KV_HEADS)], axis=0)
    w_t = iw_ref[0].T

    q_lane = lax.broadcasted_iota(jnp.int32, (1, Q_BLOCK), 1)
    key_limit = (2 * i + 1 + (q_lane >= CHUNK).astype(jnp.int32)) * CHUNK
    k_eff = jnp.minimum(key_limit, k_sel).astype(F32)
    key_iota = lax.broadcasted_iota(jnp.int32, (KEY_TILE, Q_BLOCK), 0)
    nt_dims = (((1,), (1,)), ((), ()))

    def idx_lead(kt):
        start = pl.multiple_of(kt * KEY_TILE, KEY_TILE)
        return lax.dot_general(ik_ref[0, pl.ds(start, KEY_TILE), :], qi_stack, nt_dims,
                               preferred_element_type=F32)

    def idx_body(kt, z, carry):
        rmin, rmax = carry
        start = pl.multiple_of(kt * KEY_TILE, KEY_TILE)
        sc = jnp.zeros((KEY_TILE, Q_BLOCK), F32)
        for h in range(IDX_HEADS):
            sc = sc + jnp.maximum(z[:, h * Q_BLOCK:(h + 1) * Q_BLOCK], 0.0) * w_t[h:h + 1, :]
        adm = (key_iota + start) < key_limit
        score_ref[pl.ds(start, KEY_TILE), :] = jnp.where(adm, sc, -jnp.inf)
        rmin = jnp.minimum(rmin, jnp.min(jnp.where(adm, sc, jnp.inf), axis=0, keepdims=True))
        rmax = jnp.maximum(rmax, jnp.max(jnp.where(adm, sc, -jnp.inf), axis=0, keepdims=True))
        return rmin, rmax

    init = (jnp.full((1, Q_BLOCK), jnp.inf, F32), jnp.full((1, Q_BLOCK), -jnp.inf, F32))
    lo, hi = for_tiles(idx_body, init, lead=idx_lead)

    def fold(x):
        return x.reshape(KEY_TILE // FOLD_ROWS, FOLD_ROWS, Q_BLOCK)

    def count(pred):
        def body(kt, acc):
            start = pl.multiple_of(kt * KEY_TILE, KEY_TILE)
            hit = pred(score_ref[pl.ds(start, KEY_TILE), :], key_iota + start).astype(F32)
            return acc + jnp.sum(fold(hit), axis=0)
        acc = for_tiles(body, jnp.zeros((FOLD_ROWS, Q_BLOCK), F32))
        return jnp.sum(acc, axis=0, keepdims=True)

    def unresolved(c_lo):
        return jnp.max(c_lo - k_eff) > 0.0

    def smallest_at_least(lo):
        def min_body(kt, acc):
            start = pl.multiple_of(kt * KEY_TILE, KEY_TILE)
            x = score_ref[pl.ds(start, KEY_TILE), :]
            return jnp.minimum(acc, jnp.min(fold(jnp.where(x >= lo, x, jnp.inf)), axis=0))
        return jnp.min(for_tiles(min_body, jnp.full((FOLD_ROWS, Q_BLOCK), jnp.inf, F32)), axis=0, keepdims=True)

    def bisect(carry):
        it, lo, hi, c_lo = carry
        for _ in range(BISECT_GROUP):
            mid = 0.5 * (lo + hi)
            c_mid = count(lambda x, _: x >= mid)
            ok = c_mid >= k_eff
            lo, hi, c_lo = jnp.where(ok, mid, lo), jnp.where(ok, hi, mid), jnp.where(ok, c_mid, c_lo)
        return it + BISECT_GROUP, lo, hi, c_lo

    _, lo, _, c_lo = lax.while_loop(lambda c: (c[0] < BISECT_ITERS) & unresolved(c[3]), bisect,
                                    (jnp.int32(0), lo, hi, key_limit.astype(F32)))

    vk_ref[...] = jnp.broadcast_to(lo, vk_ref.shape)
    jcut_ref[...] = jnp.full(jcut_ref.shape, jnp.iinfo(jnp.int32).max, jnp.int32)

    @pl.when(unresolved(c_lo))
    def _():
        vk = smallest_at_least(lo)
        need = k_eff - count(lambda x, _: x > vk)

        def jbisect(_, carry):
            jlo, jhi = carry
            jmid = jnp.right_shift(jlo + jhi, 1)
            ok = count(lambda x, kidx: (x == vk) & (kidx < jmid)) >= need
            return jnp.where(ok, jlo, jmid), jnp.where(ok, jmid, jhi)

        j0 = (jnp.zeros((1, Q_BLOCK), jnp.int32), jnp.full((1, Q_BLOCK), n_keys, jnp.int32))
        _, jcut = lax.fori_loop(0, index_iters, jbisect, j0)
        vk_ref[...] = jnp.broadcast_to(vk, vk_ref.shape)
        jcut_ref[...] = jnp.broadcast_to(jcut, jcut_ref.shape)

    vk = vk_ref[0:1, :]
    jcut = jcut_ref[0:1, :]

    r_i = lax.broadcasted_iota(jnp.int32, (Q_BLOCK, LANES), 0)
    eye = jnp.where(r_i == lane, 1.0, 0.0).astype(BF16)
    q_aug = jnp.concatenate([q_stack, jnp.concatenate([eye] * ATT_HEADS, axis=0)], axis=1)
    acc_ref[...] = jnp.zeros_like(acc_ref)

    def att_lead(kt):
        start = pl.multiple_of(kt * KEY_TILE, KEY_TILE)
        x = score_ref[pl.ds(start, KEY_TILE), :]
        sel = (x > vk) | ((x == vk) & ((key_iota + start) < jcut))
        bias = jnp.where(sel, 0.0, NEG_BIG).astype(BF16)
        k_aug = jnp.concatenate([k_ref[0, pl.ds(start, KEY_TILE), :], bias], axis=1)
        return lax.dot_general(k_aug, q_aug, nt_dims, preferred_element_type=F32)

    def accumulate(kt, alpha):
        acc_ref[...] = alpha * acc_ref[...] + jnp.dot(vt_ref[0, kt], p_ref[...], preferred_element_type=F32)

    def att_body(kt, s, carry):
        m, alpha = carry
        accumulate(jnp.maximum(kt - 1, 0), alpha)
        m_new = jnp.maximum(m, jnp.max(s, axis=0, keepdims=True))
        p_ref[...] = jnp.exp2(s - m_new).astype(BF16)
        return m_new, jnp.exp2(m - m_new)

    p_ref[...] = jnp.zeros_like(p_ref)
    init = (jnp.full((1, n_stack), NEG_BIG, F32), jnp.ones((1, n_stack), F32))
    _, alpha = for_tiles(att_body, init, lead=att_lead)
    accumulate(TILES_PER_STEP * n_steps - 1, alpha)

    out_t = acc_ref[0:LANES, :] / acc_ref[LANES:LANES + 1, :]
    for s in range(Q_PER_KV):
        o0 = out_t[:, (2 * s) * Q_BLOCK:(2 * s + 1) * Q_BLOCK].T
        o1 = out_t[:, (2 * s + 1) * Q_BLOCK:(2 * s + 2) * Q_BLOCK].T
        o_ref[0, :, s * LANES:(s + 1) * LANES] = jnp.where(lane < HEAD_DIM, o0, o1).astype(o_ref.dtype)


def _dsa_attention(q, iq, iw, k, v, ik, k_sel):
    bsz, seq, _ = q.shape
    assert seq % (TILES_PER_STEP * KEY_TILE) == 0 and seq % Q_BLOCK == 0
    n_kt = seq // KEY_TILE
    vt = v.reshape(bsz, n_kt, KEY_TILE, LANES).transpose(0, 1, 3, 2)
    ones_rows = jnp.zeros((bsz, n_kt, V_ROWS - LANES, KEY_TILE), v.dtype).at[:, :, 0, :].set(1.0)
    vt = jnp.concatenate([vt, ones_rows], axis=2)
    blk = lambda w: pl.BlockSpec((1, Q_BLOCK, w), lambda b, i: (b, i, 0))
    full = pl.BlockSpec((1, seq, LANES), lambda b, i: (b, 0, 0))
    return pl.pallas_call(
        functools.partial(_dsa_kernel, k_sel=k_sel, index_iters=seq.bit_length()),
        grid=(bsz, seq // Q_BLOCK),
        in_specs=[blk(ATT_WIDTH), blk(IDX_HEADS * IDX_DIM), blk(LANES), full,
                  pl.BlockSpec((1, n_kt, V_ROWS, KEY_TILE), lambda b, i: (b, 0, 0, 0)), full],
        out_specs=blk(ATT_WIDTH),
        out_shape=jax.ShapeDtypeStruct((bsz, seq, ATT_WIDTH), BF16),
        scratch_shapes=[pltpu.VMEM((seq, Q_BLOCK), F32), pltpu.VMEM((V_ROWS, ATT_HEADS * Q_BLOCK), F32),
                        pltpu.VMEM((SUBLANES, Q_BLOCK), F32), pltpu.VMEM((SUBLANES, Q_BLOCK), jnp.int32),
                        pltpu.VMEM((KEY_TILE, ATT_HEADS * Q_BLOCK), BF16)],
        compiler_params=_cparams("parallel", "parallel"),
        name="dsa_attention",
    )(q, iq, iw, k, vt, ik)


def _mlstm_kernel(mq_ref, mk_ref, mv_ref, mo_ref, gc_ref, gr_ref, cw_ref, cb_ref, nw_ref, o_ref,
                  c_ref, n_ref, m_ref, prev_ref):
    L = MLSTM_CHUNK
    W = MLSTM_WIDTH
    dh = MLSTM_HEAD_DIM

    @pl.when(pl.program_id(1) == 0)
    def _():
        c_ref[...] = jnp.zeros_like(c_ref)
        n_ref[...] = jnp.zeros_like(n_ref)
        m_ref[...] = jnp.zeros_like(m_ref)
        prev_ref[...] = jnp.zeros_like(prev_ref)

    row = lax.broadcasted_iota(jnp.int32, (SUBLANES, W), 0)

    def conv_silu(x, prev8, w4, b):
        y = x * w4[CONV_WIDTH - 1:CONV_WIDTH, :] + b
        for d in range(1, CONV_WIDTH):
            xr = pltpu.roll(x, d, 0)
            head = jnp.where(row < d, pltpu.roll(prev8, d, 0), xr[:SUBLANES])
            xs = jnp.concatenate([head, xr[SUBLANES:]], axis=0)
            y = y + xs * w4[CONV_WIDTH - 1 - d:CONV_WIDTH - d, :]
        return y / (1.0 + jnp.exp(-y))

    xq, xk = mq_ref[0], mk_ref[0]
    cw, cb = cw_ref[...], cb_ref[...]
    q = conv_silu(xq, prev_ref[:, :W], cw[:, :W], cb[:, :W])
    k = conv_silu(xk, prev_ref[:, W:], cw[:, W:], cb[:, W:]) * (dh ** -0.5)
    prev_ref[:, :W] = xq[L - SUBLANES:, :]
    prev_ref[:, W:] = xk[L - SUBLANES:, :]

    g_col = gc_ref[0]
    g_row = gr_ref[0]
    r_i = lax.broadcasted_iota(jnp.int32, (L, L), 0)
    c_i = lax.broadcasted_iota(jnp.int32, (L, L), 1)
    causal = r_i >= c_i
    tri = causal.astype(F32)
    b_col_all = jnp.dot(tri, g_col, preferred_element_type=F32, precision=lax.Precision.HIGHEST)
    b_row_all = jnp.dot(g_row, (r_i <= c_i).astype(F32), preferred_element_type=F32,
                        precision=lax.Precision.HIGHEST)

    v = mv_ref[0]
    mo = mo_ref[0]
    nw = nw_ref[...]
    for h in range(MLSTM_HEADS):
        hs = slice(h * dh, (h + 1) * dh)
        i_col = _lane_col(g_col, h)
        i_row = g_row[h:h + 1, :]
        b_col = _lane_col(b_col_all, MLSTM_HEADS + h)
        b_row = b_row_all[MLSTM_HEADS + h:MLSTM_HEADS + h + 1, :]
        b_last = jnp.sum(g_row[MLSTM_HEADS + h:MLSTM_HEADS + h + 1, :], axis=1, keepdims=True)
        m_prev = m_ref[h:h + 1, 0:1]
        c_mat = c_ref[h]
        n_vec = n_ref[h:h + 1, :]

        d_log = jnp.where(causal, b_col + (i_row - b_row), -jnp.inf)
        inter_log = b_col + m_prev
        m_t = jnp.maximum(inter_log, jnp.max(d_log, axis=1, keepdims=True))
        d_w = jnp.exp(d_log - m_t)
        inter_w = jnp.exp(inter_log - m_t)
        qh, kh, vh = q[:, hs], k[:, hs], v[:, hs]
        qb, kb = qh.astype(BF16), kh.astype(BF16)
        s = lax.dot_general(qb, kb, (((1,), (1,)), ((), ())), preferred_element_type=F32) * d_w
        num = (jnp.dot(s.astype(BF16), vh, preferred_element_type=F32)
               + inter_w * jnp.dot(qb, c_mat.astype(BF16), preferred_element_type=F32))
        den = jnp.sum(s, axis=1, keepdims=True) + inter_w * jnp.sum(qh * n_vec, axis=1, keepdims=True)
        hout = num / jnp.maximum(jnp.abs(den), jnp.exp(-m_t))

        w_log = b_last - b_col + i_col
        m_new = jnp.maximum(b_last + m_prev, jnp.max(w_log, axis=0, keepdims=True))
        decay = jnp.exp(b_last + m_prev - m_new)
        kw = kh * jnp.exp(w_log - m_new)
        c_ref[h] = decay * c_mat + jnp.dot(kw.T.astype(BF16), vh, preferred_element_type=F32)
        n_ref[h:h + 1, :] = decay * n_vec + jnp.sum(kw, axis=0, keepdims=True)
        m_ref[h:h + 1, :] = jnp.broadcast_to(m_new, (1, LANES))

        hn = hout * lax.rsqrt(jnp.mean(hout * hout, axis=-1, keepdims=True) + NORM_EPS)
        gate = 1.0 / (1.0 + jnp.exp(-mo[:, hs]))
        o_ref[0, :, hs] = (hn * nw[:, hs] * gate).astype(o_ref.dtype)


def _mlstm_mixer(mq, mk, mv, mo, g, conv_w, conv_b, norm_w):
    bsz, seq, w = mq.shape
    L = MLSTM_CHUNK
    assert seq % L == 0
    g_row = g[:, :, :SUBLANES].transpose(0, 2, 1)
    blk = lambda width: pl.BlockSpec((1, L, width), lambda b, c: (b, c, 0))
    fixed = lambda shape: pl.BlockSpec(shape, lambda b, c: (0, 0))
    return pl.pallas_call(
        _mlstm_kernel,
        grid=(bsz, seq // L),
        in_specs=[blk(w), blk(w), blk(w), blk(w), blk(LANES),
                  pl.BlockSpec((1, SUBLANES, L), lambda b, c: (b, 0, c)),
                  fixed((CONV_WIDTH, 2 * w)), fixed((1, 2 * w)), fixed((1, w))],
        out_specs=blk(w),
        out_shape=jax.ShapeDtypeStruct((bsz, seq, w), BF16),
        scratch_shapes=[pltpu.VMEM((MLSTM_HEADS, MLSTM_HEAD_DIM, MLSTM_HEAD_DIM), F32),
                        pltpu.VMEM((SUBLANES, MLSTM_HEAD_DIM), F32),
                        pltpu.VMEM((SUBLANES, LANES), F32),
                        pltpu.VMEM((SUBLANES, 2 * w), F32)],
        compiler_params=_cparams("parallel", "arbitrary"),
        name="mlstm_mixer",
    )(mq, mk, mv, mo, g, g_row, conv_w, conv_b.reshape(1, 2 * w), norm_w.reshape(1, w))


def _outproj_router_kernel(att_ref, mem_ref, x_ref, wa_ref, wm_ref, nw_ref, rw_ref, rb_ref,
                           x1_ref, hn_ref, ti_ref, tg_ref, cnt_ref):
    y = (jnp.dot(att_ref[...], wa_ref[...], preferred_element_type=F32)
         + jnp.dot(mem_ref[...], wm_ref[...], preferred_element_type=F32))
    x1 = x_ref[...] + y
    x1_ref[...] = x1
    var = jnp.mean(x1 * x1, axis=-1, keepdims=True)
    hn = (x1 * lax.rsqrt(var + NORM_EPS)) * nw_ref[...]
    hn_ref[...] = hn
    hn_hi = hn.astype(BF16)
    hn_lo = (hn - hn_hi.astype(F32)).astype(BF16)
    rw = rw_ref[...]
    rw_hi = rw.astype(BF16)
    rw_lo = (rw - rw_hi.astype(F32)).astype(BF16)
    part = jnp.dot(hn_hi, jnp.concatenate([rw_hi, rw_lo], axis=1), preferred_element_type=F32)
    logits = (part[:, :LANES] + part[:, LANES:]
              + jnp.dot(hn_lo, rw_hi, preferred_element_type=F32) + rb_ref[...])
    lane_i = lax.broadcasted_iota(jnp.int32, logits.shape, 1)
    lane = lane_i.astype(F32)
    cur = jnp.where(lane_i < N_EXPERTS, logits, -jnp.inf)
    vals, idxs = [], []
    for _ in range(TOP_K):
        mx = jnp.max(cur, axis=1, keepdims=True)
        am = jnp.min(jnp.where(cur == mx, lane, float(LANES)), axis=1, keepdims=True)
        vals.append(mx)
        idxs.append(am)
        cur = jnp.where(lane == am, -jnp.inf, cur)
    exps = [jnp.exp(vk - vals[0]) for vk in vals]
    denom = exps[0]
    for e in exps[1:]:
        denom = denom + e
    @pl.when(pl.program_id(0) == 0)
    def _():
        cnt_ref[...] = jnp.zeros_like(cnt_ref)

    tm = logits.shape[0]
    picks = [lane == am for am in idxs]
    picked = jnp.zeros(logits.shape, F32)
    for pk in picks:
        picked = picked + jnp.where(pk, 1.0, 0.0)
    r_i = lax.broadcasted_iota(jnp.int32, (tm, tm), 0)
    c_i = lax.broadcasted_iota(jnp.int32, (tm, tm), 1)
    before = jnp.dot(jnp.where(c_i < r_i, 1.0, 0.0).astype(BF16), picked.astype(BF16), preferred_element_type=F32)
    base = before + cnt_ref[0:1, :]
    cnt_ref[...] = cnt_ref[...] + jnp.sum(picked, axis=0, keepdims=True)

    ti = jnp.zeros(logits.shape, F32)
    tg = jnp.zeros(logits.shape, F32)
    for kk in range(TOP_K):
        rank = jnp.sum(jnp.where(picks[kk], base, 0.0), axis=1, keepdims=True)
        ti = jnp.where(lane_i == kk, idxs[kk], ti)
        ti = jnp.where(lane_i == TOP_K + kk, rank, ti)
        tg = jnp.where(lane_i == kk, exps[kk] / denom, tg)
    ti_ref[...] = ti.astype(jnp.int32)
    tg_ref[...] = tg


def _outproj_router(att, mem, x2, w_att, w_mem, ffn_norm_w, router_w, router_b, tm=512):
    t, d = x2.shape
    row = lambda i: (i, 0)
    fixed = lambda i: (0, 0)
    rw = jnp.pad(router_w, ((0, 0), (0, LANES - N_EXPERTS)))
    rb = jnp.pad(router_b, (0, LANES - N_EXPERTS)).reshape(1, LANES)
    return pl.pallas_call(
        _outproj_router_kernel,
        grid=(t // tm,),
        in_specs=[pl.BlockSpec((tm, ATT_WIDTH), row), pl.BlockSpec((tm, MLSTM_WIDTH), row),
                  pl.BlockSpec((tm, d), row), pl.BlockSpec((ATT_WIDTH, d), fixed),
                  pl.BlockSpec((MLSTM_WIDTH, d), fixed), pl.BlockSpec((1, d), fixed),
                  pl.BlockSpec((d, LANES), fixed), pl.BlockSpec((1, LANES), fixed)],
        out_specs=[pl.BlockSpec((tm, d), row), pl.BlockSpec((tm, d), row),
                   pl.BlockSpec((tm, LANES), row), pl.BlockSpec((tm, LANES), row),
                   pl.BlockSpec((SUBLANES, LANES), fixed)],
        out_shape=[jax.ShapeDtypeStruct((t, d), F32), jax.ShapeDtypeStruct((t, d), F32),
                   jax.ShapeDtypeStruct((t, LANES), jnp.int32), jax.ShapeDtypeStruct((t, LANES), F32),
                   jax.ShapeDtypeStruct((SUBLANES, LANES), F32)],
        compiler_params=_cparams("arbitrary"),
        name="outproj_router",
    )(att, mem, x2, w_att, w_mem, ffn_norm_w.reshape(1, d), rw, rb)


DISPATCH_ROWS = 256
ISSUE_ROWS = 16


def _dispatch_kernel(fill_ref, nused_ref, dest_ref, hn_ref, x_hbm, stage, zero_buf, sem, zsem, csem, bsem):
    n_blocks = x_hbm.shape[0] // EXPERT_ROWS
    step = pl.program_id(0)
    slot = step % 2
    stage[slot] = hn_ref[...]

    def token_row(j, kk):
        return pltpu.make_async_copy(stage.at[slot, pl.ds(j, 1), :],
                                     x_hbm.at[pl.ds(dest_ref[0, 0, j * TOP_K + kk], 1), :], sem.at[slot])

    def wait_tokens(which):
        all_rows = x_hbm.at[pl.ds(0, TOP_K * DISPATCH_ROWS), :]
        pltpu.make_async_copy(all_rows, all_rows, sem.at[which]).wait()

    def pad_row(r):
        return pltpu.make_async_copy(zero_buf.at[pl.ds(0, 1), :], x_hbm.at[pl.ds(r, 1), :], zsem)

    def pad_chunk(start, size):
        return pltpu.make_async_copy(zero_buf.at[pl.ds(0, size), :],
                                     x_hbm.at[pl.ds(pl.multiple_of(start, SUBLANES), size), :], csem)

    def pad_block(b):
        start = pl.multiple_of(b * EXPERT_ROWS, EXPERT_ROWS)
        return pltpu.make_async_copy(zero_buf, x_hbm.at[pl.ds(start, EXPERT_ROWS), :], bsem)

    def start_tokens(g, c):
        base = pl.multiple_of(g * ISSUE_ROWS, ISSUE_ROWS)
        for r in range(ISSUE_ROWS):
            for kk in range(TOP_K):
                token_row(base + r, kk).start()
        return c

    chunk_sizes = [s for s in (256, 128, 64, 32, 16, 8) if s < EXPERT_ROWS]

    def for_padding(fn):
        def per_expert(e, c):
            first = fill_ref[0, e]
            end = first + fill_ref[1, e]
            aligned = jnp.minimum(((first + SUBLANES - 1) // SUBLANES) * SUBLANES, end)
            lax.fori_loop(first, aligned, lambda r, cc: (fn(pad_row(r)), cc)[1], 0)
            left = end - aligned
            cur = aligned
            for size in chunk_sizes:
                take = (left & size) != 0
                pl.when(take)(functools.partial(lambda cur, size: fn(pad_chunk(cur, size)), cur, size))
                cur = cur + jnp.where(take, size, 0)
            return c
        lax.fori_loop(0, N_EXPERTS, per_expert, 0)
        lax.fori_loop(nused_ref[0], n_blocks, lambda b, cc: (fn(pad_block(b)), cc)[1], 0)

    lax.fori_loop(0, DISPATCH_ROWS // ISSUE_ROWS, start_tokens, 0)

    @pl.when(pl.program_id(0) == 0)
    def _():
        zero_buf[...] = jnp.zeros_like(zero_buf)
        for_padding(lambda cp: cp.start())
        for_padding(lambda cp: cp.wait())

    @pl.when(step > 0)
    def _():
        wait_tokens(1 - slot)

    @pl.when(step == pl.num_programs(0) - 1)
    def _():
        wait_tokens(slot)


def _dispatch_rows(hn, dest, fill, n_blocks_used, n_rows):
    t, d = hn.shape
    tm = DISPATCH_ROWS
    nb = t // tm
    return pl.pallas_call(
        _dispatch_kernel,
        grid_spec=pltpu.PrefetchScalarGridSpec(
            num_scalar_prefetch=2,
            grid=(nb,),
            in_specs=[pl.BlockSpec((1, 1, tm * TOP_K), lambda i, f, n: (i, 0, 0), memory_space=pltpu.SMEM),
                      pl.BlockSpec((tm, d), lambda i, f, n: (i, 0))],
            out_specs=pl.BlockSpec(memory_space=pl.ANY),
            scratch_shapes=[pltpu.VMEM((2, tm, d), hn.dtype), pltpu.VMEM((EXPERT_ROWS, d), hn.dtype),
                            pltpu.SemaphoreType.DMA((2,)), pltpu.SemaphoreType.DMA(()),
                            pltpu.SemaphoreType.DMA(()), pltpu.SemaphoreType.DMA(())],
        ),
        out_shape=jax.ShapeDtypeStruct((n_rows, d), hn.dtype),
        compiler_params=_cparams("arbitrary"),
        name="moe_dispatch",
    )(fill, n_blocks_used, dest.reshape(nb, 1, tm * TOP_K), hn)


def _expert_kernel(be_ref, nu_ref, nxt_ref, par_ref, x_ref, bgu_ref, bdn_ref, wgu_hbm, wdn_hbm, o_ref,
                   wgu_f, wdn_f, wgu_s, wdn_s, sem):
    b = pl.program_id(0)
    e = be_ref[b]
    e_prev = be_ref[jnp.maximum(b - 1, 0)]
    slot = par_ref[b]
    used = b < nu_ref[0]
    d_e = wdn_s.shape[0]

    def fetch(expert, into):
        return (pltpu.make_async_copy(wgu_hbm.at[expert], wgu_f.at[into], sem.at[0, into]),
                pltpu.make_async_copy(wdn_hbm.at[expert], wdn_f.at[into], sem.at[1, into]))

    @pl.when(b == 0)
    def _():
        for cp in fetch(e, slot):
            cp.start()

    @pl.when(used & ((b == 0) | (e != e_prev)))
    def _():
        for cp in fetch(e, slot):
            cp.wait()

        @pl.when(nxt_ref[b] >= 0)
        def _():
            for cp in fetch(nxt_ref[b], 1 - slot):
                cp.start()

        wgu_s[...] = wgu_f[slot].astype(BF16)
        wdn_s[...] = wdn_f[slot].astype(BF16)

    @pl.when(used)
    def _():
        xb = x_ref[...].astype(BF16)
        bgu = bgu_ref[0]
        glu = jnp.dot(xb, wgu_s[:, :d_e], preferred_element_type=F32) + bgu[:, :d_e]
        lin = jnp.dot(xb, wgu_s[:, d_e:], preferred_element_type=F32) + bgu[:, d_e:]
        glu = jnp.minimum(glu, SWIGLU_LIMIT)
        lin = jnp.clip(lin, -SWIGLU_LIMIT, SWIGLU_LIMIT)
        act = glu * (1.0 / (1.0 + jnp.exp(-SWIGLU_ALPHA * glu))) * (lin + 1.0)
        o_ref[...] = jnp.dot(act.astype(BF16), wdn_s[...], preferred_element_type=F32) + bdn_ref[0]

    @pl.when(jnp.logical_not(used))
    def _():
        o_ref[...] = jnp.zeros_like(o_ref)


def _expert_ffn(x_buf, blk_expert, n_blocks_used, next_expert, run_parity, w_gu, b_gu, w_dn, b_dn):
    n_rows, d = x_buf.shape
    ne, _, d2 = w_gu.shape
    d_e = w_dn.shape[1]
    nb = n_rows // EXPERT_ROWS
    last_used = lambda b, be, nu, nx, pr: (jnp.minimum(b, jnp.maximum(nu[0] - 1, 0)), 0)
    by_expert = lambda b, be, nu, nx, pr: (be[b], 0, 0)
    return pl.pallas_call(
        _expert_kernel,
        grid_spec=pltpu.PrefetchScalarGridSpec(
            num_scalar_prefetch=4,
            grid=(nb,),
            in_specs=[pl.BlockSpec((EXPERT_ROWS, d), last_used),
                      pl.BlockSpec((1, 1, d2), by_expert),
                      pl.BlockSpec((1, 1, d), by_expert),
                      pl.BlockSpec(memory_space=pl.ANY),
                      pl.BlockSpec(memory_space=pl.ANY)],
            out_specs=pl.BlockSpec((EXPERT_ROWS, d), lambda b, be, nu, nx, pr: (b, 0)),
            scratch_shapes=[pltpu.VMEM((2, d, d2), w_gu.dtype), pltpu.VMEM((2, d_e, d), w_dn.dtype),
                            pltpu.VMEM((d, d2), BF16), pltpu.VMEM((d_e, d), BF16),
                            pltpu.SemaphoreType.DMA((2, 2))],
        ),
        out_shape=jax.ShapeDtypeStruct((n_rows, d), F32),
        compiler_params=_cparams("arbitrary"),
        name="moe_experts",
    )(blk_expert, n_blocks_used, next_expert, run_parity, x_buf, b_gu.reshape(ne, 1, d2), b_dn.reshape(ne, 1, d),
      w_gu, w_dn)


COMBINE_ROWS = 256


def _combine_kernel(pos_ref, pos_next_ref, g_ref, x1_ref, nw_ref, y_hbm, o_ref, buf, sem, *, apply_norm):
    tm = COMBINE_ROWS
    step = pl.program_id(0)
    slot = step % 2

    def gather(p_ref, into):
        def issue(g, c):
            base = pl.multiple_of(g * ISSUE_ROWS, ISSUE_ROWS)
            for r in range(ISSUE_ROWS):
                for kk in range(TOP_K):
                    pltpu.make_async_copy(y_hbm.at[pl.ds(p_ref[0, 0, (base + r) * TOP_K + kk], 1), :],
                                          buf.at[into, pl.ds(base + (kk * tm + r), 1), :], sem.at[into]).start()
            return c
        lax.fori_loop(0, tm // ISSUE_ROWS, issue, 0)

    @pl.when(step == 0)
    def _():
        gather(pos_ref, 0)

    @pl.when(step + 1 < pl.num_programs(0))
    def _():
        gather(pos_next_ref, 1 - slot)

    pltpu.make_async_copy(y_hbm.at[pl.ds(0, TOP_K * tm), :], buf.at[slot], sem.at[slot]).wait()

    g = g_ref[...]
    acc = x1_ref[...]
    for kk in range(TOP_K):
        acc = acc + buf[slot, pl.ds(kk * tm, tm), :] * _lane_col(g, kk)
    if apply_norm:
        var = jnp.mean(acc * acc, axis=-1, keepdims=True)
        acc = (acc * lax.rsqrt(var + NORM_EPS)) * nw_ref[...]
    o_ref[...] = acc


def _combine(y_buf, pos, gates, x1, final_norm_w, apply_norm):
    t, d = x1.shape
    tm = COMBINE_ROWS
    nb = t // tm
    row = lambda i: (i, 0)
    pos3 = pos.reshape(nb, 1, tm * TOP_K)
    return pl.pallas_call(
        functools.partial(_combine_kernel, apply_norm=apply_norm),
        grid=(nb,),
        in_specs=[pl.BlockSpec((1, 1, tm * TOP_K), lambda i: (i, 0, 0), memory_space=pltpu.SMEM),
                  pl.BlockSpec((1, 1, tm * TOP_K), lambda i: (jnp.minimum(i + 1, nb - 1), 0, 0),
                               memory_space=pltpu.SMEM),
                  pl.BlockSpec((tm, LANES), row), pl.BlockSpec((tm, d), row),
                  pl.BlockSpec((1, d), lambda i: (0, 0)), pl.BlockSpec(memory_space=pl.ANY)],
        out_specs=pl.BlockSpec((tm, d), row),
        out_shape=jax.ShapeDtypeStruct((t, d), F32),
        scratch_shapes=[pltpu.VMEM((2, TOP_K * tm, d), F32), pltpu.SemaphoreType.DMA((2,))],
        compiler_params=_cparams("arbitrary"),
        name="moe_combine",
    )(pos3, pos3, gates, x1, final_norm_w.reshape(1, d), y_buf)


def _routing_tables(top_idx, rank, counts, n_tokens):
    padded = (counts + EXPERT_ROWS - 1) // EXPERT_ROWS * EXPERT_ROWS
    pad_end = jnp.cumsum(padded)
    pad_start = pad_end - padded
    onehot = (top_idx[:, :, None] == jnp.arange(N_EXPERTS, dtype=jnp.int32)[None, None, :]).astype(F32)
    start_of = jnp.einsum("tke,e->tk", onehot, pad_start.astype(F32), precision=lax.Precision.HIGHEST)
    dest = (start_of.astype(jnp.int32) + rank).astype(jnp.int32)
    n_rows = n_tokens * TOP_K + N_EXPERTS * EXPERT_ROWS
    n_blocks = n_rows // EXPERT_ROWS
    blk_start = jnp.arange(n_blocks, dtype=jnp.int32) * EXPERT_ROWS
    blk_expert = jnp.minimum(jnp.sum((pad_end[None, :] <= blk_start[:, None]).astype(jnp.int32), axis=1),
                             N_EXPERTS - 1).astype(jnp.int32)
    n_blocks_used = (pad_end[-1:] // EXPERT_ROWS).astype(jnp.int32)
    fill = jnp.stack([pad_start + counts, padded - counts]).astype(jnp.int32)
    first = jnp.concatenate([jnp.ones((1,), jnp.bool_), blk_expert[1:] != blk_expert[:-1]])
    run_parity = ((jnp.cumsum(first.astype(jnp.int32)) - 1) % 2).astype(jnp.int32)
    next_run = (pad_end[blk_expert] // EXPERT_ROWS).astype(jnp.int32)
    next_expert = jnp.where(next_run < n_blocks_used[0],
                            blk_expert[jnp.minimum(next_run, n_blocks - 1)], -1).astype(jnp.int32)
    return dest, fill, blk_expert, n_blocks_used, next_expert, run_parity, n_rows


def kernel(x, mix_norm_w, w_in, conv_w, conv_b, igate_b, fgate_b, mlstm_norm_w, w_out, ffn_norm_w,
           router_w, router_b, w_gate_up, b_gate_up, w_down, b_down, final_norm_w):
    bsz, seq, d = x.shape
    depth = w_in.shape[0]
    t = bsz * seq
    k_sel = min(TOPK_MAX, seq // 4)
    head_order = [h for s in range(Q_PER_KV) for h in (s, Q_PER_KV + s)]
    x2 = x.reshape(t, d)
    for layer in range(depth):
        p = _in_projection(x2, mix_norm_w[layer], w_in[layer], igate_b[layer], fgate_b[layer], seq)
        r3 = lambda a: a.reshape(bsz, seq, a.shape[-1])
        att = _dsa_attention(r3(p["q"]), r3(p["iq"]), r3(p["iw"]), r3(p["k"]), r3(p["v"]), r3(p["ik"]), k_sel)
        mem = _mlstm_mixer(r3(p["mq"]), r3(p["mk"]), r3(p["mv"]), r3(p["mo"]), r3(p["g"]),
                           conv_w[layer], conv_b[layer], mlstm_norm_w[layer])
        wo = w_out[layer]
        w_att = wo[:ATT_WIDTH].reshape(ATT_HEADS, HEAD_DIM, d)[jnp.asarray(head_order)].reshape(ATT_WIDTH, d)
        x1, hn, top_i, top_g, counts = _outproj_router(
            att.reshape(t, ATT_WIDTH), mem.reshape(t, MLSTM_WIDTH), x2, w_att.astype(BF16),
            wo[ATT_WIDTH:].astype(BF16), ffn_norm_w[layer], router_w[layer], router_b[layer])
        dest, fill, blk_expert, n_blocks_used, next_expert, run_parity, n_rows = _routing_tables(
            top_i[:, :TOP_K], top_i[:, TOP_K:2 * TOP_K], counts[0, :N_EXPERTS].astype(jnp.int32), t)
        x_buf = _dispatch_rows(hn, dest, fill, n_blocks_used, n_rows)
        y_buf = _expert_ffn(x_buf, blk_expert, n_blocks_used, next_expert, run_parity, w_gate_up[layer],
                            b_gate_up[layer], w_down[layer], b_down[layer])
        x2 = _combine(y_buf, dest, top_g, x1, final_norm_w, apply_norm=(layer == depth - 1))
    return x2.reshape(bsz, seq, d)
```

```python
import functools

import jax
import jax.numpy as jnp
import numpy as np
from jax import lax
from jax.experimental import pallas as pl
from jax.experimental.pallas import tpu as pltpu

F32 = jnp.float32
BF16 = jnp.bfloat16

LANES = 128
SUBLANES = 8
VMEM_LIMIT_BYTES = 56 * 1024 * 1024

NORM_EPS = 1e-6
ROPE_THETA = 500000.0
CHUNK = 64

HEAD_DIM = 64
ATT_HEADS = 8
ATT_KV_HEADS = 2
Q_PER_KV = ATT_HEADS // ATT_KV_HEADS
ATT_ROT_DIM = HEAD_DIM // 4
IDX_HEADS = 8
IDX_DIM = 32
IDX_ROT_DIM = IDX_DIM // 4
TOPK_MAX = 256
ATT_WIDTH = ATT_HEADS * HEAD_DIM

MLSTM_HEAD_DIM = 128
MLSTM_HEADS = 4
MLSTM_WIDTH = MLSTM_HEADS * MLSTM_HEAD_DIM
CONV_WIDTH = 4

N_EXPERTS = 32
TOP_K = 4
SWIGLU_ALPHA = 1.702
SWIGLU_LIMIT = 7.0

Q_BLOCK = 128
KEY_TILE = 256
BISECT_ITERS = 32
BISECT_GROUP = 4
FOLD_ROWS = 32
TILES_PER_STEP = 2
V_ROWS = LANES + 16
MLSTM_CHUNK = 256
EXPERT_ROWS = 512
NEG_BIG = -1e30


def _cparams(*sem):
    return pltpu.CompilerParams(dimension_semantics=sem, vmem_limit_bytes=VMEM_LIMIT_BYTES)


_SEG_WIDTHS = (("q", 512), ("k", 128), ("v", 128), ("iq", 256), ("ik", 128), ("iw", 128),
               ("mq", 512), ("mk", 512), ("mv", 512), ("mo", 512), ("g", 128))
_SEG = {}
_off = 0
for _name, _w in _SEG_WIDTHS:
    _SEG[_name] = (_off, _w)
    _off += _w
PACKED_WIDTH = _off


def _lane_col(x, idx):
    lane = lax.broadcasted_iota(jnp.int32, x.shape, 1)
    return jnp.sum(jnp.where(lane == idx, x, 0.0), axis=1, keepdims=True)


def _inproj_kernel(x_ref, nw_ref, w_ref, gb_ref, ca_ref, sa1_ref, sa2_ref, ci_ref, si1_ref, si2_ref,
                   q_ref, k_ref, v_ref, iq_ref, ik_ref, iw_ref, mq_ref, mk_ref, mv_ref, mo_ref, g_ref,
                   *, idx_w_scale):
    x = x_ref[...]
    var = jnp.mean(x * x, axis=-1, keepdims=True)
    h = (x * lax.rsqrt(var + NORM_EPS)) * nw_ref[...]
    hb = h.astype(BF16)

    def proj(name):
        lo, width = _SEG[name]
        return jnp.dot(hb, w_ref[:, lo:lo + width], preferred_element_type=F32)

    def rope(p, c_ref, s1_ref, s2_ref, half):
        c, s1, s2 = c_ref[...], s1_ref[...], s2_ref[...]
        outs = []
        for j in range(p.shape[1] // LANES):
            xs = p[:, j * LANES:(j + 1) * LANES]
            outs.append(xs * c + pltpu.roll(xs, LANES - half, 1) * s1 + pltpu.roll(xs, half, 1) * s2)
        return outs[0] if len(outs) == 1 else jnp.concatenate(outs, axis=1)

    att_scale = float(HEAD_DIM ** -0.5 * np.log2(np.e))
    q_ref[...] = (rope(proj("q"), ca_ref, sa1_ref, sa2_ref, ATT_ROT_DIM // 2) * att_scale).astype(BF16)
    k_ref[...] = rope(proj("k"), ca_ref, sa1_ref, sa2_ref, ATT_ROT_DIM // 2).astype(BF16)
    v_ref[...] = proj("v").astype(BF16)
    iq_ref[...] = rope(proj("iq"), ci_ref, si1_ref, si2_ref, IDX_ROT_DIM // 2).astype(BF16)
    ik_ref[...] = rope(proj("ik"), ci_ref, si1_ref, si2_ref, IDX_ROT_DIM // 2).astype(BF16)
    iw_ref[...] = proj("iw") * idx_w_scale
    mq_ref[...] = proj("mq")
    mk_ref[...] = proj("mk")
    mv_ref[...] = proj("mv").astype(BF16)
    mo_ref[...] = proj("mo")
    g = proj("g") + gb_ref[...]
    lane = lax.broadcasted_iota(jnp.int32, g.shape, 1)
    log_f = jnp.minimum(g, 0.0) - jnp.log(1.0 + jnp.exp(-jnp.abs(g)))
    g_ref[...] = jnp.where(lane < MLSTM_HEADS, g, log_f)


def _rope_tables(seq, rot_dim, head_dim):
    pos = jnp.arange(seq, dtype=F32)
    inv_freq = ROPE_THETA ** (-jnp.arange(0, rot_dim, 2, dtype=F32) / rot_dim)
    ang = pos[:, None] * inv_freq[None, :]
    cos, sin = jnp.cos(ang), jnp.sin(ang)
    half = rot_dim // 2
    lane = np.arange(LANES) % head_dim
    fidx = lane % half
    first = lane < half
    second = (lane >= half) & (lane < rot_dim)
    cos_l, sin_l = cos[:, fidx], sin[:, fidx]
    c = jnp.where(jnp.asarray(first | second)[None, :], cos_l, 1.0)
    s1 = jnp.where(jnp.asarray(first)[None, :], -sin_l, 0.0)
    s2 = jnp.where(jnp.asarray(second)[None, :], sin_l, 0.0)
    return c, s1, s2


def _pack_w_in(w_in):
    sizes = (ATT_WIDTH, ATT_KV_HEADS * HEAD_DIM, ATT_KV_HEADS * HEAD_DIM, IDX_HEADS * IDX_DIM, IDX_DIM,
             IDX_HEADS, MLSTM_WIDTH, MLSTM_WIDTH, MLSTM_WIDTH, MLSTM_HEADS, MLSTM_HEADS, MLSTM_WIDTH)
    offs = np.concatenate([[0], np.cumsum(sizes)])
    aq, ak, av, iq, ik, iw, mq, mk, mv, mi, mf, mo = (w_in[:, offs[i]:offs[i + 1]] for i in range(12))
    d = w_in.shape[0]
    head_order = [h for s in range(Q_PER_KV) for h in (s, Q_PER_KV + s)]
    aq = aq.reshape(d, ATT_HEADS, HEAD_DIM)[:, head_order, :].reshape(d, ATT_WIDTH)
    ik_rep = jnp.tile(ik, (1, LANES // IDX_DIM))
    iw_pad = jnp.pad(iw, ((0, 0), (0, LANES - IDX_HEADS)))
    g_pad = jnp.pad(jnp.concatenate([mi, mf], axis=1), ((0, 0), (0, LANES - 2 * MLSTM_HEADS)))
    packed = jnp.concatenate([aq, ak, av, iq, ik_rep, iw_pad, mq, mk, mv, mo, g_pad], axis=1)
    assert packed.shape[1] == PACKED_WIDTH
    return packed.astype(BF16)


def _in_projection(x2, norm_w, w_in, igate_b, fgate_b, seq, tm=1024):
    t, d = x2.shape
    assert t % tm == 0 and seq % tm == 0
    wp = _pack_w_in(w_in)
    gb = jnp.pad(jnp.concatenate([igate_b, fgate_b]), (0, LANES - 2 * MLSTM_HEADS)).reshape(1, LANES).astype(F32)
    ca, sa1, sa2 = _rope_tables(seq, ATT_ROT_DIM, HEAD_DIM)
    ci, si1, si2 = _rope_tables(seq, IDX_ROT_DIM, IDX_DIM)
    n_pos_blocks = seq // tm
    row = lambda i: (i, 0)
    fixed = lambda i: (0, 0)
    pos = lambda i: (i % n_pos_blocks, 0)
    tab = pl.BlockSpec((tm, LANES), pos)
    out_defs = (("q", 512, BF16), ("k", 128, BF16), ("v", 128, BF16), ("iq", 256, BF16), ("ik", 128, BF16),
                ("iw", 128, F32), ("mq", 512, F32), ("mk", 512, F32), ("mv", 512, BF16), ("mo", 512, F32),
                ("g", 128, F32))
    idx_w_scale = (IDX_HEADS ** -0.5) * (IDX_DIM ** -0.5)
    outs = pl.pallas_call(
        functools.partial(_inproj_kernel, idx_w_scale=idx_w_scale),
        grid=(t // tm,),
        in_specs=[pl.BlockSpec((tm, d), row), pl.BlockSpec((1, d), fixed),
                  pl.BlockSpec((d, PACKED_WIDTH), fixed), pl.BlockSpec((1, LANES), fixed),
                  tab, tab, tab, tab, tab, tab],
        out_specs=[pl.BlockSpec((tm, w), row) for _, w, _ in out_defs],
        out_shape=[jax.ShapeDtypeStruct((t, w), dt) for _, w, dt in out_defs],
        compiler_params=_cparams("parallel"),
        name="in_projection",
    )(x2, norm_w.reshape(1, d), wp, gb, ca, sa1, sa2, ci, si1, si2)
    return dict(zip([n for n, _, _ in out_defs], outs))


def _dsa_kernel(q_ref, iq_ref, iw_ref, k_ref, vt_ref, ik_ref, o_ref, score_ref, acc_ref, vk_ref, jcut_ref, p_ref,
                *, k_sel, index_iters):
    i = pl.program_id(1)
    n_keys = Q_BLOCK * (i + 1)
    n_steps = (n_keys + TILES_PER_STEP * KEY_TILE - 1) // (TILES_PER_STEP * KEY_TILE)

    def for_tiles(body, init, lead=None):
        def step(j, carry):
            tiles = [j * TILES_PER_STEP + u for u in range(TILES_PER_STEP)]
            opened = [lead(kt) if lead is not None else None for kt in tiles]
            for kt, head in zip(tiles, opened):
                carry = body(kt, carry) if lead is None else body(kt, head, carry)
            return carry
        return lax.fori_loop(0, n_steps, step, init)
    n_stack = ATT_HEADS * Q_BLOCK

    lane = lax.broadcasted_iota(jnp.int32, (Q_BLOCK, LANES), 1)
    iq = iq_ref[0]
    per_slab = LANES // IDX_DIM
    qi_stack = jnp.concatenate(
        [jnp.where((lane // IDX_DIM) == (h % per_slab), iq[:, (h // per_slab) * LANES:(h // per_slab + 1) * LANES], 0)
         for h in range(IDX_HEADS)], axis=0)
    q = q_ref[0]
    q_stack = jnp.concatenate(
        [jnp.where((lane // HEAD_DIM) == g, q[:, s * LANES:(s + 1) * LANES], 0)
         for s in range(Q_PER_KV) for g in range(ATT_KV_HEADS)], axis=0)
    w_t = iw_ref[0].T

    q_lane = lax.broadcasted_iota(jnp.int32, (1, Q_BLOCK), 1)
    key_limit = (2 * i + 1 + (q_lane >= CHUNK).astype(jnp.int32)) * CHUNK
    k_eff = jnp.minimum(key_limit, k_sel).astype(F32)
    key_iota = lax.broadcasted_iota(jnp.int32, (KEY_TILE, Q_BLOCK), 0)
    nt_dims = (((1,), (1,)), ((), ()))

    def idx_lead(kt):
        start = pl.multiple_of(kt * KEY_TILE, KEY_TILE)
        return lax.dot_general(ik_ref[0, pl.ds(start, KEY_TILE), :], qi_stack, nt_dims,
                               preferred_element_type=F32)

    def idx_body(kt, z, carry):
        rmin, rmax = carry
        start = pl.multiple_of(kt * KEY_TILE, KEY_TILE)
        sc = jnp.zeros((KEY_TILE, Q_BLOCK), F32)
        for h in range(IDX_HEADS):
            sc = sc + jnp.maximum(z[:, h * Q_BLOCK:(h + 1) * Q_BLOCK], 0.0) * w_t[h:h + 1, :]
        adm = (key_iota + start) < key_limit
        score_ref[pl.ds(start, KEY_TILE), :] = jnp.where(adm, sc, -jnp.inf)
        rmin = jnp.minimum(rmin, jnp.min(jnp.where(adm, sc, jnp.inf), axis=0, keepdims=True))
        rmax = jnp.maximum(rmax, jnp.max(jnp.where(adm, sc, -jnp.inf), axis=0, keepdims=True))
        return rmin, rmax

    init = (jnp.full((1, Q_BLOCK), jnp.inf, F32), jnp.full((1, Q_BLOCK), -jnp.inf, F32))
    lo, hi = for_tiles(idx_body, init, lead=idx_lead)

    def fold(x):
        return x.reshape(KEY_TILE // FOLD_ROWS, FOLD_ROWS, Q_BLOCK)

    def count(pred):
        def body(kt, acc):
            start = pl.multiple_of(kt * KEY_TILE, KEY_TILE)
            hit = pred(score_ref[pl.ds(start, KEY_TILE), :], key_iota + start).astype(F32)
            return acc + jnp.sum(fold(hit), axis=0)
        acc = for_tiles(body, jnp.zeros((FOLD_ROWS, Q_BLOCK), F32))
        return jnp.sum(acc, axis=0, keepdims=True)

    def unresolved(c_lo):
        return jnp.max(c_lo - k_eff) > 0.0

    def smallest_at_least(lo):
        def min_body(kt, acc):
            start = pl.multiple_of(kt * KEY_TILE, KEY_TILE)
            x = score_ref[pl.ds(start, KEY_TILE), :]
            return jnp.minimum(acc, jnp.min(fold(jnp.where(x >= lo, x, jnp.inf)), axis=0))
        return jnp.min(for_tiles(min_body, jnp.full((FOLD_ROWS, Q_BLOCK), jnp.inf, F32)), axis=0, keepdims=True)

    def bisect(carry):
        it, lo, hi, c_lo = carry
        for _ in range(BISECT_GROUP):
            mid = 0.5 * (lo + hi)
            c_mid = count(lambda x, _: x >= mid)
            ok = c_mid >= k_eff
            lo, hi, c_lo = jnp.where(ok, mid, lo), jnp.where(ok, hi, mid), jnp.where(ok, c_mid, c_lo)
        return it + BISECT_GROUP, lo, hi, c_lo

    _, lo, _, c_lo = lax.while_loop(lambda c: (c[0] < BISECT_ITERS) & unresolved(c[3]), bisect,
                                    (jnp.int32(0), lo, hi, key_limit.astype(F32)))

    vk_ref[...] = jnp.broadcast_to(lo, vk_ref.shape)
    jcut_ref[...] = jnp.full(jcut_ref.shape, jnp.iinfo(jnp.int32).max, jnp.int32)

    @pl.when(unresolved(c_lo))
    def _():
        vk = smallest_at_least(lo)
        need = k_eff - count(lambda x, _: x > vk)

        def jbisect(_, carry):
            jlo, jhi = carry
            jmid = jnp.right_shift(jlo + jhi, 1)
            ok = count(lambda x, kidx: (x == vk) & (kidx < jmid)) >= need
            return jnp.where(ok, jlo, jmid), jnp.where(ok, jmid, jhi)

        j0 = (jnp.zeros((1, Q_BLOCK), jnp.int32), jnp.full((1, Q_BLOCK), n_keys, jnp.int32))
        _, jcut = lax.fori_loop(0, index_iters, jbisect, j0)
        vk_ref[...] = jnp.broadcast_to(vk, vk_ref.shape)
        jcut_ref[...] = jnp.broadcast_to(jcut, jcut_ref.shape)

    vk = vk_ref[0:1, :]
    jcut = jcut_ref[0:1, :]

    r_i = lax.broadcasted_iota(jnp.int32, (Q_BLOCK, LANES), 0)
    eye = jnp.where(r_i == lane, 1.0, 0.0).astype(BF16)
    q_aug = jnp.concatenate([q_stack, jnp.concatenate([eye] * ATT_HEADS, axis=0)], axis=1)
    acc_ref[...] = jnp.zeros_like(acc_ref)

    def att_lead(kt):
        start = pl.multiple_of(kt * KEY_TILE, KEY_TILE)
        x = score_ref[pl.ds(start, KEY_TILE), :]
        sel = (x > vk) | ((x == vk) & ((key_iota + start) < jcut))
        bias = jnp.where(sel, 0.0, NEG_BIG).astype(BF16)
        k_aug = jnp.concatenate([k_ref[0, pl.ds(start, KEY_TILE), :], bias], axis=1)
        return lax.dot_general(k_aug, q_aug, nt_dims, preferred_element_type=F32)

    def accumulate(kt, alpha):
        acc_ref[...] = alpha * acc_ref[...] + jnp.dot(vt_ref[0, kt], p_ref[...], preferred_element_type=F32)

    def att_body(kt, s, carry):
        m, alpha = carry
        accumulate(jnp.maximum(kt - 1, 0), alpha)
        m_new = jnp.maximum(m, jnp.max(s, axis=0, keepdims=True))
        p_ref[...] = jnp.exp2(s - m_new).astype(BF16)
        return m_new, jnp.exp2(m - m_new)

    p_ref[...] = jnp.zeros_like(p_ref)
    init = (jnp.full((1, n_stack), NEG_BIG, F32), jnp.ones((1, n_stack), F32))
    _, alpha = for_tiles(att_body, init, lead=att_lead)
    accumulate(TILES_PER_STEP * n_steps - 1, alpha)

    out_t = acc_ref[0:LANES, :] / acc_ref[LANES:LANES + 1, :]
    for s in range(Q_PER_KV):
        o0 = out_t[:, (2 * s) * Q_BLOCK:(2 * s + 1) * Q_BLOCK].T
        o1 = out_t[:, (2 * s + 1) * Q_BLOCK:(2 * s + 2) * Q_BLOCK].T
        o_ref[0, :, s * LANES:(s + 1) * LANES] = jnp.where(lane < HEAD_DIM, o0, o1).astype(o_ref.dtype)


def _dsa_attention(q, iq, iw, k, v, ik, k_sel):
    bsz, seq, _ = q.shape
    assert seq % (TILES_PER_STEP * KEY_TILE) == 0 and seq % Q_BLOCK == 0
    n_kt = seq // KEY_TILE
    vt = v.reshape(bsz, n_kt, KEY_TILE, LANES).transpose(0, 1, 3, 2)
    ones_rows = jnp.zeros((bsz, n_kt, V_ROWS - LANES, KEY_TILE), v.dtype).at[:, :, 0, :].set(1.0)
    vt = jnp.concatenate([vt, ones_rows], axis=2)
    blk = lambda w: pl.BlockSpec((1, Q_BLOCK, w), lambda b, i: (b, i, 0))
    full = pl.BlockSpec((1, seq, LANES), lambda b, i: (b, 0, 0))
    return pl.pallas_call(
        functools.partial(_dsa_kernel, k_sel=k_sel, index_iters=seq.bit_length()),
        grid=(bsz, seq // Q_BLOCK),
        in_specs=[blk(ATT_WIDTH), blk(IDX_HEADS * IDX_DIM), blk(LANES), full,
                  pl.BlockSpec((1, n_kt, V_ROWS, KEY_TILE), lambda b, i: (b, 0, 0, 0)), full],
        out_specs=blk(ATT_WIDTH),
        out_shape=jax.ShapeDtypeStruct((bsz, seq, ATT_WIDTH), BF16),
        scratch_shapes=[pltpu.VMEM((seq, Q_BLOCK), F32), pltpu.VMEM((V_ROWS, ATT_HEADS * Q_BLOCK), F32),
                        pltpu.VMEM((SUBLANES, Q_BLOCK), F32), pltpu.VMEM((SUBLANES, Q_BLOCK), jnp.int32),
                        pltpu.VMEM((KEY_TILE, ATT_HEADS * Q_BLOCK), BF16)],
        compiler_params=_cparams("parallel", "parallel"),
        name="dsa_attention",
    )(q, iq, iw, k, vt, ik)


def _mlstm_kernel(mq_ref, mk_ref, mv_ref, mo_ref, gc_ref, gr_ref, cw_ref, cb_ref, nw_ref, o_ref,
                  c_ref, n_ref, m_ref, prev_ref):
    L = MLSTM_CHUNK
    W = MLSTM_WIDTH
    dh = MLSTM_HEAD_DIM

    @pl.when(pl.program_id(1) == 0)
    def _():
        c_ref[...] = jnp.zeros_like(c_ref)
        n_ref[...] = jnp.zeros_like(n_ref)
        m_ref[...] = jnp.zeros_like(m_ref)
        prev_ref[...] = jnp.zeros_like(prev_ref)

    row = lax.broadcasted_iota(jnp.int32, (SUBLANES, W), 0)

    def conv_silu(x, prev8, w4, b):
        y = x * w4[CONV_WIDTH - 1:CONV_WIDTH, :] + b
        for d in range(1, CONV_WIDTH):
            xr = pltpu.roll(x, d, 0)
            head = jnp.where(row < d, pltpu.roll(prev8, d, 0), xr[:SUBLANES])
            xs = jnp.concatenate([head, xr[SUBLANES:]], axis=0)
            y = y + xs * w4[CONV_WIDTH - 1 - d:CONV_WIDTH - d, :]
        return y / (1.0 + jnp.exp(-y))

    xq, xk = mq_ref[0], mk_ref[0]
    cw, cb = cw_ref[...], cb_ref[...]
    q = conv_silu(xq, prev_ref[:, :W], cw[:, :W], cb[:, :W])
    k = conv_silu(xk, prev_ref[:, W:], cw[:, W:], cb[:, W:]) * (dh ** -0.5)
    prev_ref[:, :W] = xq[L - SUBLANES:, :]
    prev_ref[:, W:] = xk[L - SUBLANES:, :]

    g_col = gc_ref[0]
    g_row = gr_ref[0]
    r_i = lax.broadcasted_iota(jnp.int32, (L, L), 0)
    c_i = lax.broadcasted_iota(jnp.int32, (L, L), 1)
    causal = r_i >= c_i
    tri = causal.astype(F32)
    b_col_all = jnp.dot(tri, g_col, preferred_element_type=F32, precision=lax.Precision.HIGHEST)
    b_row_all = jnp.dot(g_row, (r_i <= c_i).astype(F32), preferred_element_type=F32,
                        precision=lax.Precision.HIGHEST)

    v = mv_ref[0]
    mo = mo_ref[0]
    nw = nw_ref[...]
    for h in range(MLSTM_HEADS):
        hs = slice(h * dh, (h + 1) * dh)
        i_col = _lane_col(g_col, h)
        i_row = g_row[h:h + 1, :]
        b_col = _lane_col(b_col_all, MLSTM_HEADS + h)
        b_row = b_row_all[MLSTM_HEADS + h:MLSTM_HEADS + h + 1, :]
        b_last = jnp.sum(g_row[MLSTM_HEADS + h:MLSTM_HEADS + h + 1, :], axis=1, keepdims=True)
        m_prev = m_ref[h:h + 1, 0:1]
        c_mat = c_ref[h]
        n_vec = n_ref[h:h + 1, :]

        d_log = jnp.where(causal, b_col + (i_row - b_row), -jnp.inf)
        inter_log = b_col + m_prev
        m_t = jnp.maximum(inter_log, jnp.max(d_log, axis=1, keepdims=True))
        d_w = jnp.exp(d_log - m_t)
        inter_w = jnp.exp(inter_log - m_t)
        qh, kh, vh = q[:, hs], k[:, hs], v[:, hs]
        qb, kb = qh.astype(BF16), kh.astype(BF16)
        s = lax.dot_general(qb, kb, (((1,), (1,)), ((), ())), preferred_element_type=F32) * d_w
        num = (jnp.dot(s.astype(BF16), vh, preferred_element_type=F32)
               + inter_w * jnp.dot(qb, c_mat.astype(BF16), preferred_element_type=F32))
        den = jnp.sum(s, axis=1, keepdims=True) + inter_w * jnp.sum(qh * n_vec, axis=1, keepdims=True)
        hout = num / jnp.maximum(jnp.abs(den), jnp.exp(-m_t))

        w_log = b_last - b_col + i_col
        m_new = jnp.maximum(b_last + m_prev, jnp.max(w_log, axis=0, keepdims=True))
        decay = jnp.exp(b_last + m_prev - m_new)
        kw = kh * jnp.exp(w_log - m_new)
        c_ref[h] = decay * c_mat + jnp.dot(kw.T.astype(BF16), vh, preferred_element_type=F32)
        n_ref[h:h + 1, :] = decay * n_vec + jnp.sum(kw, axis=0, keepdims=True)
        m_ref[h:h + 1, :] = jnp.broadcast_to(m_new, (1, LANES))

        hn = hout * lax.rsqrt(jnp.mean(hout * hout, axis=-1, keepdims=True) + NORM_EPS)
        gate = 1.0 / (1.0 + jnp.exp(-mo[:, hs]))
        o_ref[0, :, hs] = (hn * nw[:, hs] * gate).astype(o_ref.dtype)


def _mlstm_mixer(mq, mk, mv, mo, g, conv_w, conv_b, norm_w):
    bsz, seq, w = mq.shape
    L = MLSTM_CHUNK
    assert seq % L == 0
    g_row = g[:, :, :SUBLANES].transpose(0, 2, 1)
    blk = lambda width: pl.BlockSpec((1, L, width), lambda b, c: (b, c, 0))
    fixed = lambda shape: pl.BlockSpec(shape, lambda b, c: (0, 0))
    return pl.pallas_call(
        _mlstm_kernel,
        grid=(bsz, seq // L),
        in_specs=[blk(w), blk(w), blk(w), blk(w), blk(LANES),
                  pl.BlockSpec((1, SUBLANES, L), lambda b, c: (b, 0, c)),
                  fixed((CONV_WIDTH, 2 * w)), fixed((1, 2 * w)), fixed((1, w))],
        out_specs=blk(w),
        out_shape=jax.ShapeDtypeStruct((bsz, seq, w), BF16),
        scratch_shapes=[pltpu.VMEM((MLSTM_HEADS, MLSTM_HEAD_DIM, MLSTM_HEAD_DIM), F32),
                        pltpu.VMEM((SUBLANES, MLSTM_HEAD_DIM), F32),
                        pltpu.VMEM((SUBLANES, LANES), F32),
                        pltpu.VMEM((SUBLANES, 2 * w), F32)],
        compiler_params=_cparams("parallel", "arbitrary"),
        name="mlstm_mixer",
    )(mq, mk, mv, mo, g, g_row, conv_w, conv_b.reshape(1, 2 * w), norm_w.reshape(1, w))


def _outproj_router_kernel(att_ref, mem_ref, x_ref, wa_ref, wm_ref, nw_ref, rw_ref, rb_ref,
                           x1_ref, hn_ref, ti_ref, tg_ref, cnt_ref):
    y = (jnp.dot(att_ref[...], wa_ref[...], preferred_element_type=F32)
         + jnp.dot(mem_ref[...], wm_ref[...], preferred_element_type=F32))
    x1 = x_ref[...] + y
    x1_ref[...] = x1
    var = jnp.mean(x1 * x1, axis=-1, keepdims=True)
    hn = (x1 * lax.rsqrt(var + NORM_EPS)) * nw_ref[...]
    hn_ref[...] = hn
    hn_hi = hn.astype(BF16)
    hn_lo = (hn - hn_hi.astype(F32)).astype(BF16)
    rw = rw_ref[...]
    rw_hi = rw.astype(BF16)
    rw_lo = (rw - rw_hi.astype(F32)).astype(BF16)
    part = jnp.dot(hn_hi, jnp.concatenate([rw_hi, rw_lo], axis=1), preferred_element_type=F32)
    logits = (part[:, :LANES] + part[:, LANES:]
              + jnp.dot(hn_lo, rw_hi, preferred_element_type=F32) + rb_ref[...])
    lane_i = lax.broadcasted_iota(jnp.int32, logits.shape, 1)
    lane = lane_i.astype(F32)
    cur = jnp.where(lane_i < N_EXPERTS, logits, -jnp.inf)
    vals, idxs = [], []
    for _ in range(TOP_K):
        mx = jnp.max(cur, axis=1, keepdims=True)
        am = jnp.min(jnp.where(cur == mx, lane, float(LANES)), axis=1, keepdims=True)
        vals.append(mx)
        idxs.append(am)
        cur = jnp.where(lane == am, -jnp.inf, cur)
    exps = [jnp.exp(vk - vals[0]) for vk in vals]
    denom = exps[0]
    for e in exps[1:]:
        denom = denom + e
    @pl.when(pl.program_id(0) == 0)
    def _():
        cnt_ref[...] = jnp.zeros_like(cnt_ref)

    tm = logits.shape[0]
    picks = [lane == am for am in idxs]
    picked = jnp.zeros(logits.shape, F32)
    for pk in picks:
        picked = picked + jnp.where(pk, 1.0, 0.0)
    r_i = lax.broadcasted_iota(jnp.int32, (tm, tm), 0)
    c_i = lax.broadcasted_iota(jnp.int32, (tm, tm), 1)
    before = jnp.dot(jnp.where(c_i < r_i, 1.0, 0.0).astype(BF16), picked.astype(BF16), preferred_element_type=F32)
    base = before + cnt_ref[0:1, :]
    cnt_ref[...] = cnt_ref[...] + jnp.sum(picked, axis=0, keepdims=True)

    ti = jnp.zeros(logits.shape, F32)
    tg = jnp.zeros(logits.shape, F32)
    for kk in range(TOP_K):
        rank = jnp.sum(jnp.where(picks[kk], base, 0.0), axis=1, keepdims=True)
        ti = jnp.where(lane_i == kk, idxs[kk], ti)
        ti = jnp.where(lane_i == TOP_K + kk, rank, ti)
        tg = jnp.where(lane_i == kk, exps[kk] / denom, tg)
    ti_ref[...] = ti.astype(jnp.int32)
    tg_ref[...] = tg


def _outproj_router(att, mem, x2, w_att, w_mem, ffn_norm_w, router_w, router_b, tm=512):
    t, d = x2.shape
    row = lambda i: (i, 0)
    fixed = lambda i: (0, 0)
    rw = jnp.pad(router_w, ((0, 0), (0, LANES - N_EXPERTS)))
    rb = jnp.pad(router_b, (0, LANES - N_EXPERTS)).reshape(1, LANES)
    return pl.pallas_call(
        _outproj_router_kernel,
        grid=(t // tm,),
        in_specs=[pl.BlockSpec((tm, ATT_WIDTH), row), pl.BlockSpec((tm, MLSTM_WIDTH), row),
                  pl.BlockSpec((tm, d), row), pl.BlockSpec((ATT_WIDTH, d), fixed),
                  pl.BlockSpec((MLSTM_WIDTH, d), fixed), pl.BlockSpec((1, d), fixed),
                  pl.BlockSpec((d, LANES), fixed), pl.BlockSpec((1, LANES), fixed)],
        out_specs=[pl.BlockSpec((tm, d), row), pl.BlockSpec((tm, d), row),
                   pl.BlockSpec((tm, LANES), row), pl.BlockSpec((tm, LANES), row),
                   pl.BlockSpec((SUBLANES, LANES), fixed)],
        out_shape=[jax.ShapeDtypeStruct((t, d), F32), jax.ShapeDtypeStruct((t, d), F32),
                   jax.ShapeDtypeStruct((t, LANES), jnp.int32), jax.ShapeDtypeStruct((t, LANES), F32),
                   jax.ShapeDtypeStruct((SUBLANES, LANES), F32)],
        compiler_params=_cparams("arbitrary"),
        name="outproj_router",
    )(att, mem, x2, w_att, w_mem, ffn_norm_w.reshape(1, d), rw, rb)


DISPATCH_ROWS = 256
ISSUE_ROWS = 16
DMA_PRIORITIES = 2


def _dispatch_kernel(fill_ref, nused_ref, dest_ref, hn_ref, x_hbm, stage, zero_buf, sem, zsem, csem, bsem):
    n_blocks = x_hbm.shape[0] // EXPERT_ROWS
    step = pl.program_id(0)
    slot = step % 2
    stage[slot] = hn_ref[...].reshape(stage.shape[1:])

    def token_row(g, r, kk):
        return pltpu.make_async_copy(stage.at[slot, g, pl.ds(r, 1), :],
                                     x_hbm.at[pl.ds(dest_ref[0, 0, (g * ISSUE_ROWS + r) * TOP_K + kk], 1), :],
                                     sem.at[slot])

    def wait_tokens(which):
        all_rows = x_hbm.at[pl.ds(0, TOP_K * DISPATCH_ROWS), :]
        pltpu.make_async_copy(all_rows, all_rows, sem.at[which]).wait()

    def pad_row(r):
        return pltpu.make_async_copy(zero_buf.at[pl.ds(0, 1), :], x_hbm.at[pl.ds(r, 1), :], zsem)

    def pad_chunk(start, size):
        return pltpu.make_async_copy(zero_buf.at[pl.ds(0, size), :],
                                     x_hbm.at[pl.ds(pl.multiple_of(start, SUBLANES), size), :], csem)

    def pad_block(b):
        start = pl.multiple_of(b * EXPERT_ROWS, EXPERT_ROWS)
        return pltpu.make_async_copy(zero_buf, x_hbm.at[pl.ds(start, EXPERT_ROWS), :], bsem)

    def start_tokens(g, c):
        for r in range(ISSUE_ROWS):
            for kk in range(TOP_K):
                token_row(g, r, kk).start(priority=(r * TOP_K + kk) % DMA_PRIORITIES)
        return c

    chunk_sizes = [s for s in (256, 128, 64, 32, 16, 8) if s < EXPERT_ROWS]

    def for_padding(fn):
        def per_expert(e, c):
            first = fill_ref[0, e]
            end = first + fill_ref[1, e]
            aligned = jnp.minimum(((first + SUBLANES - 1) // SUBLANES) * SUBLANES, end)
            lax.fori_loop(first, aligned, lambda r, cc: (fn(pad_row(r)), cc)[1], 0)
            left = end - aligned
            cur = aligned
            for size in chunk_sizes:
                take = (left & size) != 0
                pl.when(take)(functools.partial(lambda cur, size: fn(pad_chunk(cur, size)), cur, size))
                cur = cur + jnp.where(take, size, 0)
            return c
        lax.fori_loop(0, N_EXPERTS, per_expert, 0)
        lax.fori_loop(nused_ref[0], n_blocks, lambda b, cc: (fn(pad_block(b)), cc)[1], 0)

    lax.fori_loop(0, DISPATCH_ROWS // ISSUE_ROWS, start_tokens, 0)

    @pl.when(pl.program_id(0) == 0)
    def _():
        zero_buf[...] = jnp.zeros_like(zero_buf)
        for_padding(lambda cp: cp.start())
        for_padding(lambda cp: cp.wait())

    @pl.when(step > 0)
    def _():
        wait_tokens(1 - slot)

    @pl.when(step == pl.num_programs(0) - 1)
    def _():
        wait_tokens(slot)


def _dispatch_rows(hn, dest, fill, n_blocks_used, n_rows):
    t, d = hn.shape
    tm = DISPATCH_ROWS
    nb = t // tm
    return pl.pallas_call(
        _dispatch_kernel,
        grid_spec=pltpu.PrefetchScalarGridSpec(
            num_scalar_prefetch=2,
            grid=(nb,),
            in_specs=[pl.BlockSpec((1, 1, tm * TOP_K), lambda i, f, n: (i, 0, 0), memory_space=pltpu.SMEM),
                      pl.BlockSpec((tm, d), lambda i, f, n: (i, 0))],
            out_specs=pl.BlockSpec(memory_space=pl.ANY),
            scratch_shapes=[pltpu.VMEM((2, tm // ISSUE_ROWS, ISSUE_ROWS, d), hn.dtype),
                            pltpu.VMEM((EXPERT_ROWS, d), hn.dtype),
                            pltpu.SemaphoreType.DMA((2,)), pltpu.SemaphoreType.DMA(()),
                            pltpu.SemaphoreType.DMA(()), pltpu.SemaphoreType.DMA(())],
        ),
        out_shape=jax.ShapeDtypeStruct((n_rows, d), hn.dtype),
        compiler_params=_cparams("arbitrary"),
        name="moe_dispatch",
    )(fill, n_blocks_used, dest.reshape(nb, 1, tm * TOP_K), hn)


def _expert_kernel(be_ref, nu_ref, nxt_ref, par_ref, x_ref, bgu_ref, bdn_ref, wgu_hbm, wdn_hbm, o_ref,
                   wgu_f, wdn_f, wgu_s, wdn_s, sem):
    b = pl.program_id(0)
    e = be_ref[b]
    e_prev = be_ref[jnp.maximum(b - 1, 0)]
    slot = par_ref[b]
    used = b < nu_ref[0]
    d_e = wdn_s.shape[0]

    def fetch(expert, into):
        return (pltpu.make_async_copy(wgu_hbm.at[expert], wgu_f.at[into], sem.at[0, into]),
                pltpu.make_async_copy(wdn_hbm.at[expert], wdn_f.at[into], sem.at[1, into]))

    @pl.when(b == 0)
    def _():
        for cp in fetch(e, slot):
            cp.start()

    @pl.when(used & ((b == 0) | (e != e_prev)))
    def _():
        for cp in fetch(e, slot):
            cp.wait()

        @pl.when(nxt_ref[b] >= 0)
        def _():
            for cp in fetch(nxt_ref[b], 1 - slot):
                cp.start()

        wgu_s[...] = wgu_f[slot].astype(BF16)
        wdn_s[...] = wdn_f[slot].astype(BF16)

    @pl.when(used)
    def _():
        xb = x_ref[...].astype(BF16)
        bgu = bgu_ref[0]
        glu = jnp.dot(xb, wgu_s[:, :d_e], preferred_element_type=F32) + bgu[:, :d_e]
        lin = jnp.dot(xb, wgu_s[:, d_e:], preferred_element_type=F32) + bgu[:, d_e:]
        glu = jnp.minimum(glu, SWIGLU_LIMIT)
        lin = jnp.clip(lin, -SWIGLU_LIMIT, SWIGLU_LIMIT)
        act = glu * (1.0 / (1.0 + jnp.exp(-SWIGLU_ALPHA * glu))) * (lin + 1.0)
        o_ref[...] = jnp.dot(act.astype(BF16), wdn_s[...], preferred_element_type=F32) + bdn_ref[0]

    @pl.when(jnp.logical_not(used))
    def _():
        o_ref[...] = jnp.zeros_like(o_ref)


def _expert_ffn(x_buf, blk_expert, n_blocks_used, next_expert, run_parity, w_gu, b_gu, w_dn, b_dn):
    n_rows, d = x_buf.shape
    ne, _, d2 = w_gu.shape
    d_e = w_dn.shape[1]
    nb = n_rows // EXPERT_ROWS
    last_used = lambda b, be, nu, nx, pr: (jnp.minimum(b, jnp.maximum(nu[0] - 1, 0)), 0)
    by_expert = lambda b, be, nu, nx, pr: (be[b], 0, 0)
    return pl.pallas_call(
        _expert_kernel,
        grid_spec=pltpu.PrefetchScalarGridSpec(
            num_scalar_prefetch=4,
            grid=(nb,),
            in_specs=[pl.BlockSpec((EXPERT_ROWS, d), last_used),
                      pl.BlockSpec((1, 1, d2), by_expert),
                      pl.BlockSpec((1, 1, d), by_expert),
                      pl.BlockSpec(memory_space=pl.ANY),
                      pl.BlockSpec(memory_space=pl.ANY)],
            out_specs=pl.BlockSpec((EXPERT_ROWS, d), lambda b, be, nu, nx, pr: (b, 0)),
            scratch_shapes=[pltpu.VMEM((2, d, d2), w_gu.dtype), pltpu.VMEM((2, d_e, d), w_dn.dtype),
                            pltpu.VMEM((d, d2), BF16), pltpu.VMEM((d_e, d), BF16),
                            pltpu.SemaphoreType.DMA((2, 2))],
        ),
        out_shape=jax.ShapeDtypeStruct((n_rows, d), F32),
        compiler_params=_cparams("arbitrary"),
        name="moe_experts",
    )(blk_expert, n_blocks_used, next_expert, run_parity, x_buf, b_gu.reshape(ne, 1, d2), b_dn.reshape(ne, 1, d),
      w_gu, w_dn)


COMBINE_ROWS = 256


def _combine_kernel(pos_ref, pos_next_ref, g_ref, x1_ref, nw_ref, y_hbm, o_ref, buf, sem, *, apply_norm):
    tm = COMBINE_ROWS
    step = pl.program_id(0)
    slot = step % 2

    def gather(p_ref, into):
        def issue(g, c):
            base = g * (ISSUE_ROWS * TOP_K)
            for r in range(ISSUE_ROWS):
                for kk in range(TOP_K):
                    pltpu.make_async_copy(y_hbm.at[pl.ds(p_ref[0, 0, base + (r * TOP_K + kk)], 1), :],
                                          buf.at[into, g, pl.ds(kk * ISSUE_ROWS + r, 1), :],
                                          sem.at[into]).start(priority=(r * TOP_K + kk) % DMA_PRIORITIES)
            return c
        lax.fori_loop(0, tm // ISSUE_ROWS, issue, 0)

    @pl.when(step == 0)
    def _():
        gather(pos_ref, 0)

    @pl.when(step + 1 < pl.num_programs(0))
    def _():
        gather(pos_next_ref, 1 - slot)

    pltpu.make_async_copy(buf.at[slot], buf.at[slot], sem.at[slot]).wait()

    g = g_ref[...]
    acc = x1_ref[...]
    for kk in range(TOP_K):
        rows = buf[slot, :, pl.ds(kk * ISSUE_ROWS, ISSUE_ROWS), :].reshape(tm, acc.shape[1])
        acc = acc + rows * _lane_col(g, kk)
    if apply_norm:
        var = jnp.mean(acc * acc, axis=-1, keepdims=True)
        acc = (acc * lax.rsqrt(var + NORM_EPS)) * nw_ref[...]
    o_ref[...] = acc


def _combine(y_buf, pos, gates, x1, final_norm_w, apply_norm):
    t, d = x1.shape
    tm = COMBINE_ROWS
    nb = t // tm
    row = lambda i: (i, 0)
    pos3 = pos.reshape(nb, 1, tm * TOP_K)
    return pl.pallas_call(
        functools.partial(_combine_kernel, apply_norm=apply_norm),
        grid=(nb,),
        in_specs=[pl.BlockSpec((1, 1, tm * TOP_K), lambda i: (i, 0, 0), memory_space=pltpu.SMEM),
                  pl.BlockSpec((1, 1, tm * TOP_K), lambda i: (jnp.minimum(i + 1, nb - 1), 0, 0),
                               memory_space=pltpu.SMEM),
                  pl.BlockSpec((tm, LANES), row), pl.BlockSpec((tm, d), row),
                  pl.BlockSpec((1, d), lambda i: (0, 0)), pl.BlockSpec(memory_space=pl.ANY)],
        out_specs=pl.BlockSpec((tm, d), row),
        out_shape=jax.ShapeDtypeStruct((t, d), F32),
        scratch_shapes=[pltpu.VMEM((2, tm // ISSUE_ROWS, TOP_K * ISSUE_ROWS, d), F32),
                        pltpu.SemaphoreType.DMA((2,))],
        compiler_params=_cparams("arbitrary"),
        name="moe_combine",
    )(pos3, pos3, gates, x1, final_norm_w.reshape(1, d), y_buf)


def _routing_tables(top_idx, rank, counts, n_tokens):
    padded = (counts + EXPERT_ROWS - 1) // EXPERT_ROWS * EXPERT_ROWS
    pad_end = jnp.cumsum(padded)
    pad_start = pad_end - padded
    onehot = (top_idx[:, :, None] == jnp.arange(N_EXPERTS, dtype=jnp.int32)[None, None, :]).astype(F32)
    start_of = jnp.einsum("tke,e->tk", onehot, pad_start.astype(F32), precision=lax.Precision.HIGHEST)
    dest = (start_of.astype(jnp.int32) + rank).astype(jnp.int32)
    n_rows = n_tokens * TOP_K + N_EXPERTS * EXPERT_ROWS
    n_blocks = n_rows // EXPERT_ROWS
    blk_start = jnp.arange(n_blocks, dtype=jnp.int32) * EXPERT_ROWS
    blk_expert = jnp.minimum(jnp.sum((pad_end[None, :] <= blk_start[:, None]).astype(jnp.int32), axis=1),
                             N_EXPERTS - 1).astype(jnp.int32)
    n_blocks_used = (pad_end[-1:] // EXPERT_ROWS).astype(jnp.int32)
    fill = jnp.stack([pad_start + counts, padded - counts]).astype(jnp.int32)
    first = jnp.concatenate([jnp.ones((1,), jnp.bool_), blk_expert[1:] != blk_expert[:-1]])
    run_parity = ((jnp.cumsum(first.astype(jnp.int32)) - 1) % 2).astype(jnp.int32)
    next_run = (pad_end[blk_expert] // EXPERT_ROWS).astype(jnp.int32)
    next_expert = jnp.where(next_run < n_blocks_used[0],
                            blk_expert[jnp.minimum(next_run, n_blocks - 1)], -1).astype(jnp.int32)
    return dest, fill, blk_expert, n_blocks_used, next_expert, run_parity, n_rows


def kernel(x, mix_norm_w, w_in, conv_w, conv_b, igate_b, fgate_b, mlstm_norm_w, w_out, ffn_norm_w,
           router_w, router_b, w_gate_up, b_gate_up, w_down, b_down, final_norm_w):
    bsz, seq, d = x.shape
    depth = w_in.shape[0]
    t = bsz * seq
    k_sel = min(TOPK_MAX, seq // 4)
    head_order = [h for s in range(Q_PER_KV) for h in (s, Q_PER_KV + s)]
    x2 = x.reshape(t, d)
    for layer in range(depth):
        p = _in_projection(x2, mix_norm_w[layer], w_in[layer], igate_b[layer], fgate_b[layer], seq)
        r3 = lambda a: a.reshape(bsz, seq, a.shape[-1])
        att = _dsa_attention(r3(p["q"]), r3(p["iq"]), r3(p["iw"]), r3(p["k"]), r3(p["v"]), r3(p["ik"]), k_sel)
        mem = _mlstm_mixer(r3(p["mq"]), r3(p["mk"]), r3(p["mv"]), r3(p["mo"]), r3(p["g"]),
                           conv_w[layer], conv_b[layer], mlstm_norm_w[layer])
        wo = w_out[layer]
        w_att = wo[:ATT_WIDTH].reshape(ATT_HEADS, HEAD_DIM, d)[jnp.asarray(head_order)].reshape(ATT_WIDTH, d)
        x1, hn, top_i, top_g, counts = _outproj_router(
            att.reshape(t, ATT_WIDTH), mem.reshape(t, MLSTM_WIDTH), x2, w_att.astype(BF16),
            wo[ATT_WIDTH:].astype(BF16), ffn_norm_w[layer], router_w[layer], router_b[layer])
        dest, fill, blk_expert, n_blocks_used, next_expert, run_parity, n_rows = _routing_tables(
            top_i[:, :TOP_K], top_i[:, TOP_K:2 * TOP_K], counts[0, :N_EXPERTS].astype(jnp.int32), t)
        x_buf = _dispatch_rows(hn, dest, fill, n_blocks_used, n_rows)
        y_buf = _expert_ffn(x_buf, blk_expert, n_blocks_used, next_expert, run_parity, w_gate_up[layer],
                            b_gate_up[layer], w_down[layer], b_down[layer])
        x2 = _combine(y_buf, dest, top_g, x1, final_norm_w, apply_norm=(layer == depth - 1))
    return x2.reshape(bsz, seq, d)
```

```python
import functools

import jax
import jax.numpy as jnp
import numpy as np
from jax import lax
from jax.experimental import pallas as pl
from jax.experimental.pallas import tpu as pltpu

F32 = jnp.float32
BF16 = jnp.bfloat16

LANES = 128
SUBLANES = 8
VMEM_LIMIT_BYTES = 56 * 1024 * 1024

NORM_EPS = 1e-6
ROPE_THETA = 500000.0
CHUNK = 64

HEAD_DIM = 64
ATT_HEADS = 8
ATT_KV_HEADS = 2
Q_PER_KV = ATT_HEADS // ATT_KV_HEADS
ATT_ROT_DIM = HEAD_DIM // 4
IDX_HEADS = 8
IDX_DIM = 32
IDX_ROT_DIM = IDX_DIM // 4
TOPK_MAX = 256
ATT_WIDTH = ATT_HEADS * HEAD_DIM

MLSTM_HEAD_DIM = 128
MLSTM_HEADS = 4
MLSTM_WIDTH = MLSTM_HEADS * MLSTM_HEAD_DIM
CONV_WIDTH = 4

N_EXPERTS = 32
TOP_K = 4
SWIGLU_ALPHA = 1.702
SWIGLU_LIMIT = 7.0

Q_BLOCK = 128
KEY_TILE = 256
BISECT_ITERS = 32
BISECT_GROUP = 4
FOLD_ROWS = 32
TILES_PER_STEP = 2
V_ROWS = LANES + 16
MLSTM_CHUNK = 256
EXPERT_ROWS = 512
NEG_BIG = -1e30


def _cparams(*sem):
    return pltpu.CompilerParams(dimension_semantics=sem, vmem_limit_bytes=VMEM_LIMIT_BYTES)


_SEG_WIDTHS = (("q", 512), ("k", 128), ("v", 128), ("iq", 256), ("ik", 128), ("iw", 128),
               ("mq", 512), ("mk", 512), ("mv", 512), ("mo", 512), ("g", 128))
_SEG = {}
_off = 0
for _name, _w in _SEG_WIDTHS:
    _SEG[_name] = (_off, _w)
    _off += _w
PACKED_WIDTH = _off


def _lane_col(x, idx):
    lane = lax.broadcasted_iota(jnp.int32, x.shape, 1)
    return jnp.sum(jnp.where(lane == idx, x, 0.0), axis=1, keepdims=True)


def _inproj_kernel(x_ref, nw_ref, w_ref, gb_ref, ca_ref, sa1_ref, sa2_ref, ci_ref, si1_ref, si2_ref,
                   q_ref, k_ref, v_ref, iq_ref, ik_ref, iw_ref, mq_ref, mk_ref, mv_ref, mo_ref, g_ref,
                   *, idx_w_scale):
    x = x_ref[...]
    var = jnp.mean(x * x, axis=-1, keepdims=True)
    h = (x * lax.rsqrt(var + NORM_EPS)) * nw_ref[...]
    hb = h.astype(BF16)

    def proj(name):
        lo, width = _SEG[name]
        return jnp.dot(hb, w_ref[:, lo:lo + width], preferred_element_type=F32)

    def rope(p, c_ref, s1_ref, s2_ref, half):
        c, s1, s2 = c_ref[...], s1_ref[...], s2_ref[...]
        outs = []
        for j in range(p.shape[1] // LANES):
            xs = p[:, j * LANES:(j + 1) * LANES]
            outs.append(xs * c + pltpu.roll(xs, LANES - half, 1) * s1 + pltpu.roll(xs, half, 1) * s2)
        return outs[0] if len(outs) == 1 else jnp.concatenate(outs, axis=1)

    att_scale = float(HEAD_DIM ** -0.5 * np.log2(np.e))
    q_ref[...] = (rope(proj("q"), ca_ref, sa1_ref, sa2_ref, ATT_ROT_DIM // 2) * att_scale).astype(BF16)
    k_ref[...] = rope(proj("k"), ca_ref, sa1_ref, sa2_ref, ATT_ROT_DIM // 2).astype(BF16)
    v_ref[...] = proj("v").astype(BF16)
    iq_ref[...] = rope(proj("iq"), ci_ref, si1_ref, si2_ref, IDX_ROT_DIM // 2).astype(BF16)
    ik_ref[...] = rope(proj("ik"), ci_ref, si1_ref, si2_ref, IDX_ROT_DIM // 2).astype(BF16)
    iw_ref[...] = proj("iw") * idx_w_scale
    mq_ref[...] = proj("mq")
    mk_ref[...] = proj("mk")
    mv_ref[...] = proj("mv").astype(BF16)
    mo_ref[...] = proj("mo")
    g = proj("g") + gb_ref[...]
    lane = lax.broadcasted_iota(jnp.int32, g.shape, 1)
    log_f = jnp.minimum(g, 0.0) - jnp.log(1.0 + jnp.exp(-jnp.abs(g)))
    g_ref[...] = jnp.where(lane < MLSTM_HEADS, g, log_f)


def _rope_tables(seq, rot_dim, head_dim):
    pos = jnp.arange(seq, dtype=F32)
    inv_freq = ROPE_THETA ** (-jnp.arange(0, rot_dim, 2, dtype=F32) / rot_dim)
    ang = pos[:, None] * inv_freq[None, :]
    cos, sin = jnp.cos(ang), jnp.sin(ang)
    half = rot_dim // 2
    lane = np.arange(LANES) % head_dim
    fidx = lane % half
    first = lane < half
    second = (lane >= half) & (lane < rot_dim)
    cos_l, sin_l = cos[:, fidx], sin[:, fidx]
    c = jnp.where(jnp.asarray(first | second)[None, :], cos_l, 1.0)
    s1 = jnp.where(jnp.asarray(first)[None, :], -sin_l, 0.0)
    s2 = jnp.where(jnp.asarray(second)[None, :], sin_l, 0.0)
    return c, s1, s2


def _pack_w_in(w_in):
    sizes = (ATT_WIDTH, ATT_KV_HEADS * HEAD_DIM, ATT_KV_HEADS * HEAD_DIM, IDX_HEADS * IDX_DIM, IDX_DIM,
             IDX_HEADS, MLSTM_WIDTH, MLSTM_WIDTH, MLSTM_WIDTH, MLSTM_HEADS, MLSTM_HEADS, MLSTM_WIDTH)
    offs = np.concatenate([[0], np.cumsum(sizes)])
    aq, ak, av, iq, ik, iw, mq, mk, mv, mi, mf, mo = (w_in[:, offs[i]:offs[i + 1]] for i in range(12))
    d = w_in.shape[0]
    head_order = [h for s in range(Q_PER_KV) for h in (s, Q_PER_KV + s)]
    aq = aq.reshape(d, ATT_HEADS, HEAD_DIM)[:, head_order, :].reshape(d, ATT_WIDTH)
    ik_rep = jnp.tile(ik, (1, LANES // IDX_DIM))
    iw_pad = jnp.pad(iw, ((0, 0), (0, LANES - IDX_HEADS)))
    g_pad = jnp.pad(jnp.concatenate([mi, mf], axis=1), ((0, 0), (0, LANES - 2 * MLSTM_HEADS)))
    packed = jnp.concatenate([aq, ak, av, iq, ik_rep, iw_pad, mq, mk, mv, mo, g_pad], axis=1)
    assert packed.shape[1] == PACKED_WIDTH
    return packed.astype(BF16)


def _in_projection(x2, norm_w, w_in, igate_b, fgate_b, seq, tm=1024):
    t, d = x2.shape
    assert t % tm == 0 and seq % tm == 0
    wp = _pack_w_in(w_in)
    gb = jnp.pad(jnp.concatenate([igate_b, fgate_b]), (0, LANES - 2 * MLSTM_HEADS)).reshape(1, LANES).astype(F32)
    ca, sa1, sa2 = _rope_tables(seq, ATT_ROT_DIM, HEAD_DIM)
    ci, si1, si2 = _rope_tables(seq, IDX_ROT_DIM, IDX_DIM)
    n_pos_blocks = seq // tm
    row = lambda i: (i, 0)
    fixed = lambda i: (0, 0)
    pos = lambda i: (i % n_pos_blocks, 0)
    tab = pl.BlockSpec((tm, LANES), pos)
    out_defs = (("q", 512, BF16), ("k", 128, BF16), ("v", 128, BF16), ("iq", 256, BF16), ("ik", 128, BF16),
                ("iw", 128, F32), ("mq", 512, F32), ("mk", 512, F32), ("mv", 512, BF16), ("mo", 512, F32),
                ("g", 128, F32))
    idx_w_scale = (IDX_HEADS ** -0.5) * (IDX_DIM ** -0.5)
    outs = pl.pallas_call(
        functools.partial(_inproj_kernel, idx_w_scale=idx_w_scale),
        grid=(t // tm,),
        in_specs=[pl.BlockSpec((tm, d), row), pl.BlockSpec((1, d), fixed),
                  pl.BlockSpec((d, PACKED_WIDTH), fixed), pl.BlockSpec((1, LANES), fixed),
                  tab, tab, tab, tab, tab, tab],
        out_specs=[pl.BlockSpec((tm, w), row) for _, w, _ in out_defs],
        out_shape=[jax.ShapeDtypeStruct((t, w), dt) for _, w, dt in out_defs],
        compiler_params=_cparams("parallel"),
        name="in_projection",
    )(x2, norm_w.reshape(1, d), wp, gb, ca, sa1, sa2, ci, si1, si2)
    return dict(zip([n for n, _, _ in out_defs], outs))


def _dsa_kernel(q_ref, iq_ref, iw_ref, k_ref, vt_ref, ik_ref, o_ref, score_ref, acc_ref, vk_ref, jcut_ref, p_ref,
                *, k_sel, index_iters):
    i = pl.program_id(1)
    n_keys = Q_BLOCK * (i + 1)
    n_tiles = (n_keys + KEY_TILE - 1) // KEY_TILE
    n_steps = n_tiles // TILES_PER_STEP
    n_left = n_tiles - n_steps * TILES_PER_STEP

    def for_tiles(body, init, lead=None):
        def run(tiles, carry):
            opened = [lead(kt) if lead is not None else None for kt in tiles]
            for kt, head in zip(tiles, opened):
                carry = body(kt, carry) if lead is None else body(kt, head, carry)
            return carry

        carry = lax.fori_loop(0, n_steps, lambda j, c: run([j * TILES_PER_STEP + u for u in range(TILES_PER_STEP)], c),
                              init)
        for u in range(TILES_PER_STEP - 1):
            carry = lax.cond(n_left > u, functools.partial(run, [n_steps * TILES_PER_STEP + u]), lambda c: c, carry)
        return carry
    n_stack = ATT_HEADS * Q_BLOCK

    lane = lax.broadcasted_iota(jnp.int32, (Q_BLOCK, LANES), 1)
    iq = iq_ref[0]
    per_slab = LANES // IDX_DIM
    qi_stack = jnp.concatenate(
        [jnp.where((lane // IDX_DIM) == (h % per_slab), iq[:, (h // per_slab) * LANES:(h // per_slab + 1) * LANES], 0)
         for h in range(IDX_HEADS)], axis=0)
    q = q_ref[0]
    q_stack = jnp.concatenate(
        [jnp.where((lane // HEAD_DIM) == g, q[:, s * LANES:(s + 1) * LANES], 0)
         for s in range(Q_PER_KV) for g in range(ATT_KV_HEADS)], axis=0)
    w_t = iw_ref[0].T

    q_lane = lax.broadcasted_iota(jnp.int32, (1, Q_BLOCK), 1)
    key_limit = (2 * i + 1 + (q_lane >= CHUNK).astype(jnp.int32)) * CHUNK
    k_eff = jnp.minimum(key_limit, k_sel).astype(F32)
    key_iota = lax.broadcasted_iota(jnp.int32, (KEY_TILE, Q_BLOCK), 0)
    nt_dims = (((1,), (1,)), ((), ()))

    def idx_lead(kt):
        start = pl.multiple_of(kt * KEY_TILE, KEY_TILE)
        return lax.dot_general(ik_ref[0, pl.ds(start, KEY_TILE), :], qi_stack, nt_dims,
                               preferred_element_type=F32)

    def idx_body(kt, z, carry):
        rmin, rmax = carry
        start = pl.multiple_of(kt * KEY_TILE, KEY_TILE)
        sc = jnp.zeros((KEY_TILE, Q_BLOCK), F32)
        for h in range(IDX_HEADS):
            sc = sc + jnp.maximum(z[:, h * Q_BLOCK:(h + 1) * Q_BLOCK], 0.0) * w_t[h:h + 1, :]
        adm = (key_iota + start) < key_limit
        score_ref[pl.ds(start, KEY_TILE), :] = jnp.where(adm, sc, -jnp.inf)
        rmin = jnp.minimum(rmin, jnp.min(jnp.where(adm, sc, jnp.inf), axis=0, keepdims=True))
        rmax = jnp.maximum(rmax, jnp.max(jnp.where(adm, sc, -jnp.inf), axis=0, keepdims=True))
        return rmin, rmax

    init = (jnp.full((1, Q_BLOCK), jnp.inf, F32), jnp.full((1, Q_BLOCK), -jnp.inf, F32))
    lo, hi = for_tiles(idx_body, init, lead=idx_lead)

    def fold(x):
        return x.reshape(KEY_TILE // FOLD_ROWS, FOLD_ROWS, Q_BLOCK)

    def count(pred):
        def body(kt, acc):
            start = pl.multiple_of(kt * KEY_TILE, KEY_TILE)
            hit = pred(score_ref[pl.ds(start, KEY_TILE), :], key_iota + start).astype(F32)
            return acc + jnp.sum(fold(hit), axis=0)
        acc = for_tiles(body, jnp.zeros((FOLD_ROWS, Q_BLOCK), F32))
        return jnp.sum(acc, axis=0, keepdims=True)

    def unresolved(c_lo):
        return jnp.max(c_lo - k_eff) > 0.0

    def smallest_at_least(lo):
        def min_body(kt, acc):
            start = pl.multiple_of(kt * KEY_TILE, KEY_TILE)
            x = score_ref[pl.ds(start, KEY_TILE), :]
            return jnp.minimum(acc, jnp.min(fold(jnp.where(x >= lo, x, jnp.inf)), axis=0))
        return jnp.min(for_tiles(min_body, jnp.full((FOLD_ROWS, Q_BLOCK), jnp.inf, F32)), axis=0, keepdims=True)

    def bisect(carry):
        it, lo, hi, c_lo = carry
        for _ in range(BISECT_GROUP):
            mid = 0.5 * (lo + hi)
            c_mid = count(lambda x, _: x >= mid)
            ok = c_mid >= k_eff
            lo, hi, c_lo = jnp.where(ok, mid, lo), jnp.where(ok, hi, mid), jnp.where(ok, c_mid, c_lo)
        return it + BISECT_GROUP, lo, hi, c_lo

    _, lo, _, c_lo = lax.while_loop(lambda c: (c[0] < BISECT_ITERS) & unresolved(c[3]), bisect,
                                    (jnp.int32(0), lo, hi, key_limit.astype(F32)))

    vk_ref[...] = jnp.broadcast_to(lo, vk_ref.shape)
    jcut_ref[...] = jnp.full(jcut_ref.shape, jnp.iinfo(jnp.int32).max, jnp.int32)

    @pl.when(unresolved(c_lo))
    def _():
        vk = smallest_at_least(lo)
        need = k_eff - count(lambda x, _: x > vk)

        def jbisect(_, carry):
            jlo, jhi = carry
            jmid = jnp.right_shift(jlo + jhi, 1)
            ok = count(lambda x, kidx: (x == vk) & (kidx < jmid)) >= need
            return jnp.where(ok, jlo, jmid), jnp.where(ok, jmid, jhi)

        j0 = (jnp.zeros((1, Q_BLOCK), jnp.int32), jnp.full((1, Q_BLOCK), n_keys, jnp.int32))
        _, jcut = lax.fori_loop(0, index_iters, jbisect, j0)
        vk_ref[...] = jnp.broadcast_to(vk, vk_ref.shape)
        jcut_ref[...] = jnp.broadcast_to(jcut, jcut_ref.shape)

    vk = vk_ref[0:1, :]
    jcut = jcut_ref[0:1, :]

    r_i = lax.broadcasted_iota(jnp.int32, (Q_BLOCK, LANES), 0)
    eye = jnp.where(r_i == lane, 1.0, 0.0).astype(BF16)
    q_aug = jnp.concatenate([q_stack, jnp.concatenate([eye] * ATT_HEADS, axis=0)], axis=1)
    acc_ref[...] = jnp.zeros_like(acc_ref)

    def att_lead(kt):
        start = pl.multiple_of(kt * KEY_TILE, KEY_TILE)
        x = score_ref[pl.ds(start, KEY_TILE), :]
        sel = (x > vk) | ((x == vk) & ((key_iota + start) < jcut))
        bias = jnp.where(sel, 0.0, NEG_BIG).astype(BF16)
        k_aug = jnp.concatenate([k_ref[0, pl.ds(start, KEY_TILE), :], bias], axis=1)
        return lax.dot_general(k_aug, q_aug, nt_dims, preferred_element_type=F32)

    def accumulate(kt, alpha):
        acc_ref[...] = alpha * acc_ref[...] + jnp.dot(vt_ref[0, kt], p_ref[...], preferred_element_type=F32)

    def att_body(kt, s, carry):
        m, alpha = carry
        accumulate(jnp.maximum(kt - 1, 0), alpha)
        m_new = jnp.maximum(m, jnp.max(s, axis=0, keepdims=True))
        p_ref[...] = jnp.exp2(s - m_new).astype(BF16)
        return m_new, jnp.exp2(m - m_new)

    p_ref[...] = jnp.zeros_like(p_ref)
    init = (jnp.full((1, n_stack), NEG_BIG, F32), jnp.ones((1, n_stack), F32))
    _, alpha = for_tiles(att_body, init, lead=att_lead)
    accumulate(n_tiles - 1, alpha)

    out_t = acc_ref[0:LANES, :] / acc_ref[LANES:LANES + 1, :]
    for s in range(Q_PER_KV):
        o0 = out_t[:, (2 * s) * Q_BLOCK:(2 * s + 1) * Q_BLOCK].T
        o1 = out_t[:, (2 * s + 1) * Q_BLOCK:(2 * s + 2) * Q_BLOCK].T
        o_ref[0, :, s * LANES:(s + 1) * LANES] = jnp.where(lane < HEAD_DIM, o0, o1).astype(o_ref.dtype)


def _dsa_attention(q, iq, iw, k, v, ik, k_sel):
    bsz, seq, _ = q.shape
    assert seq % (TILES_PER_STEP * KEY_TILE) == 0 and seq % Q_BLOCK == 0
    n_kt = seq // KEY_TILE
    vt = v.reshape(bsz, n_kt, KEY_TILE, LANES).transpose(0, 1, 3, 2)
    ones_rows = jnp.zeros((bsz, n_kt, V_ROWS - LANES, KEY_TILE), v.dtype).at[:, :, 0, :].set(1.0)
    vt = jnp.concatenate([vt, ones_rows], axis=2)
    blk = lambda w: pl.BlockSpec((1, Q_BLOCK, w), lambda b, i: (b, i, 0))
    full = pl.BlockSpec((1, seq, LANES), lambda b, i: (b, 0, 0))
    return pl.pallas_call(
        functools.partial(_dsa_kernel, k_sel=k_sel, index_iters=seq.bit_length()),
        grid=(bsz, seq // Q_BLOCK),
        in_specs=[blk(ATT_WIDTH), blk(IDX_HEADS * IDX_DIM), blk(LANES), full,
                  pl.BlockSpec((1, n_kt, V_ROWS, KEY_TILE), lambda b, i: (b, 0, 0, 0)), full],
        out_specs=blk(ATT_WIDTH),
        out_shape=jax.ShapeDtypeStruct((bsz, seq, ATT_WIDTH), BF16),
        scratch_shapes=[pltpu.VMEM((seq, Q_BLOCK), F32), pltpu.VMEM((V_ROWS, ATT_HEADS * Q_BLOCK), F32),
                        pltpu.VMEM((SUBLANES, Q_BLOCK), F32), pltpu.VMEM((SUBLANES, Q_BLOCK), jnp.int32),
                        pltpu.VMEM((KEY_TILE, ATT_HEADS * Q_BLOCK), BF16)],
        compiler_params=_cparams("parallel", "parallel"),
        name="dsa_attention",
    )(q, iq, iw, k, vt, ik)


def _mlstm_kernel(mq_ref, mk_ref, mv_ref, mo_ref, gc_ref, gr_ref, cw_ref, cb_ref, nw_ref, o_ref,
                  c_ref, n_ref, m_ref, prev_ref):
    L = MLSTM_CHUNK
    W = MLSTM_WIDTH
    dh = MLSTM_HEAD_DIM

    @pl.when(pl.program_id(1) == 0)
    def _():
        c_ref[...] = jnp.zeros_like(c_ref)
        n_ref[...] = jnp.zeros_like(n_ref)
        m_ref[...] = jnp.zeros_like(m_ref)
        prev_ref[...] = jnp.zeros_like(prev_ref)

    row = lax.broadcasted_iota(jnp.int32, (SUBLANES, W), 0)

    def conv_silu(x, prev8, w4, b):
        y = x * w4[CONV_WIDTH - 1:CONV_WIDTH, :] + b
        for d in range(1, CONV_WIDTH):
            xr = pltpu.roll(x, d, 0)
            head = jnp.where(row < d, pltpu.roll(prev8, d, 0), xr[:SUBLANES])
            xs = jnp.concatenate([head, xr[SUBLANES:]], axis=0)
            y = y + xs * w4[CONV_WIDTH - 1 - d:CONV_WIDTH - d, :]
        return y / (1.0 + jnp.exp(-y))

    xq, xk = mq_ref[0], mk_ref[0]
    cw, cb = cw_ref[...], cb_ref[...]
    q = conv_silu(xq, prev_ref[:, :W], cw[:, :W], cb[:, :W])
    k = conv_silu(xk, prev_ref[:, W:], cw[:, W:], cb[:, W:]) * (dh ** -0.5)
    prev_ref[:, :W] = xq[L - SUBLANES:, :]
    prev_ref[:, W:] = xk[L - SUBLANES:, :]

    g_col = gc_ref[0]
    g_row = gr_ref[0]
    r_i = lax.broadcasted_iota(jnp.int32, (L, L), 0)
    c_i = lax.broadcasted_iota(jnp.int32, (L, L), 1)
    causal = r_i >= c_i
    tri = causal.astype(F32)
    b_col_all = jnp.dot(tri, g_col, preferred_element_type=F32, precision=lax.Precision.HIGHEST)
    b_row_all = jnp.dot(g_row, (r_i <= c_i).astype(F32), preferred_element_type=F32,
                        precision=lax.Precision.HIGHEST)

    v = mv_ref[0]
    mo = mo_ref[0]
    nw = nw_ref[...]
    for h in range(MLSTM_HEADS):
        hs = slice(h * dh, (h + 1) * dh)
        i_col = _lane_col(g_col, h)
        i_row = g_row[h:h + 1, :]
        b_col = _lane_col(b_col_all, MLSTM_HEADS + h)
        b_row = b_row_all[MLSTM_HEADS + h:MLSTM_HEADS + h + 1, :]
        b_last = jnp.sum(g_row[MLSTM_HEADS + h:MLSTM_HEADS + h + 1, :], axis=1, keepdims=True)
        m_prev = m_ref[h:h + 1, 0:1]
        c_mat = c_ref[h]
        n_vec = n_ref[h:h + 1, :]

        d_log = jnp.where(causal, b_col + (i_row - b_row), -jnp.inf)
        inter_log = b_col + m_prev
        m_t = jnp.maximum(inter_log, jnp.max(d_log, axis=1, keepdims=True))
        d_w = jnp.exp(d_log - m_t)
        inter_w = jnp.exp(inter_log - m_t)
        qh, kh, vh = q[:, hs], k[:, hs], v[:, hs]
        qb, kb = qh.astype(BF16), kh.astype(BF16)
        s = lax.dot_general(qb, kb, (((1,), (1,)), ((), ())), preferred_element_type=F32) * d_w
        num = (jnp.dot(s.astype(BF16), vh, preferred_element_type=F32)
               + inter_w * jnp.dot(qb, c_mat.astype(BF16), preferred_element_type=F32))
        den = jnp.sum(s, axis=1, keepdims=True) + inter_w * jnp.sum(qh * n_vec, axis=1, keepdims=True)
        hout = num / jnp.maximum(jnp.abs(den), jnp.exp(-m_t))

        w_log = b_last - b_col + i_col
        m_new = jnp.maximum(b_last + m_prev, jnp.max(w_log, axis=0, keepdims=True))
        decay = jnp.exp(b_last + m_prev - m_new)
        kw = kh * jnp.exp(w_log - m_new)
        c_ref[h] = decay * c_mat + jnp.dot(kw.T.astype(BF16), vh, preferred_element_type=F32)
        n_ref[h:h + 1, :] = decay * n_vec + jnp.sum(kw, axis=0, keepdims=True)
        m_ref[h:h + 1, :] = jnp.broadcast_to(m_new, (1, LANES))

        hn = hout * lax.rsqrt(jnp.mean(hout * hout, axis=-1, keepdims=True) + NORM_EPS)
        gate = 1.0 / (1.0 + jnp.exp(-mo[:, hs]))
        o_ref[0, :, hs] = (hn * nw[:, hs] * gate).astype(o_ref.dtype)


def _mlstm_mixer(mq, mk, mv, mo, g, conv_w, conv_b, norm_w):
    bsz, seq, w = mq.shape
    L = MLSTM_CHUNK
    assert seq % L == 0
    g_row = g[:, :, :SUBLANES].transpose(0, 2, 1)
    blk = lambda width: pl.BlockSpec((1, L, width), lambda b, c: (b, c, 0))
    fixed = lambda shape: pl.BlockSpec(shape, lambda b, c: (0, 0))
    return pl.pallas_call(
        _mlstm_kernel,
        grid=(bsz, seq // L),
        in_specs=[blk(w), blk(w), blk(w), blk(w), blk(LANES),
                  pl.BlockSpec((1, SUBLANES, L), lambda b, c: (b, 0, c)),
                  fixed((CONV_WIDTH, 2 * w)), fixed((1, 2 * w)), fixed((1, w))],
        out_specs=blk(w),
        out_shape=jax.ShapeDtypeStruct((bsz, seq, w), BF16),
        scratch_shapes=[pltpu.VMEM((MLSTM_HEADS, MLSTM_HEAD_DIM, MLSTM_HEAD_DIM), F32),
                        pltpu.VMEM((SUBLANES, MLSTM_HEAD_DIM), F32),
                        pltpu.VMEM((SUBLANES, LANES), F32),
                        pltpu.VMEM((SUBLANES, 2 * w), F32)],
        compiler_params=_cparams("parallel", "arbitrary"),
        name="mlstm_mixer",
    )(mq, mk, mv, mo, g, g_row, conv_w, conv_b.reshape(1, 2 * w), norm_w.reshape(1, w))


def _outproj_router_kernel(att_ref, mem_ref, x_ref, wa_ref, wm_ref, nw_ref, rw_ref, rb_ref,
                           x1_ref, hn_ref, ti_ref, tg_ref, cnt_ref):
    y = (jnp.dot(att_ref[...], wa_ref[...], preferred_element_type=F32)
         + jnp.dot(mem_ref[...], wm_ref[...], preferred_element_type=F32))
    x1 = x_ref[...] + y
    x1_ref[...] = x1
    var = jnp.mean(x1 * x1, axis=-1, keepdims=True)
    hn = (x1 * lax.rsqrt(var + NORM_EPS)) * nw_ref[...]
    hn_ref[...] = hn
    hn_hi = hn.astype(BF16)
    hn_lo = (hn - hn_hi.astype(F32)).astype(BF16)
    rw = rw_ref[...]
    rw_hi = rw.astype(BF16)
    rw_lo = (rw - rw_hi.astype(F32)).astype(BF16)
    part = jnp.dot(hn_hi, jnp.concatenate([rw_hi, rw_lo], axis=1), preferred_element_type=F32)
    logits = (part[:, :LANES] + part[:, LANES:]
              + jnp.dot(hn_lo, rw_hi, preferred_element_type=F32) + rb_ref[...])
    lane_i = lax.broadcasted_iota(jnp.int32, logits.shape, 1)
    lane = lane_i.astype(F32)
    cur = jnp.where(lane_i < N_EXPERTS, logits, -jnp.inf)
    vals, idxs = [], []
    for _ in range(TOP_K):
        mx = jnp.max(cur, axis=1, keepdims=True)
        am = jnp.min(jnp.where(cur == mx, lane, float(LANES)), axis=1, keepdims=True)
        vals.append(mx)
        idxs.append(am)
        cur = jnp.where(lane == am, -jnp.inf, cur)
    exps = [jnp.exp(vk - vals[0]) for vk in vals]
    denom = exps[0]
    for e in exps[1:]:
        denom = denom + e
    @pl.when(pl.program_id(0) == 0)
    def _():
        cnt_ref[...] = jnp.zeros_like(cnt_ref)

    tm = logits.shape[0]
    picks = [lane == am for am in idxs]
    picked = jnp.zeros(logits.shape, F32)
    for pk in picks:
        picked = picked + jnp.where(pk, 1.0, 0.0)
    r_i = lax.broadcasted_iota(jnp.int32, (tm, tm), 0)
    c_i = lax.broadcasted_iota(jnp.int32, (tm, tm), 1)
    before = jnp.dot(jnp.where(c_i < r_i, 1.0, 0.0).astype(BF16), picked.astype(BF16), preferred_element_type=F32)
    base = before + cnt_ref[0:1, :]
    cnt_ref[...] = cnt_ref[...] + jnp.sum(picked, axis=0, keepdims=True)

    ti = jnp.zeros(logits.shape, F32)
    tg = jnp.zeros(logits.shape, F32)
    for kk in range(TOP_K):
        rank = jnp.sum(jnp.where(picks[kk], base, 0.0), axis=1, keepdims=True)
        ti = jnp.where(lane_i == kk, idxs[kk], ti)
        ti = jnp.where(lane_i == TOP_K + kk, rank, ti)
        tg = jnp.where(lane_i == kk, exps[kk] / denom, tg)
    ti_ref[...] = ti.astype(jnp.int32)
    tg_ref[...] = tg


def _outproj_router(att, mem, x2, w_att, w_mem, ffn_norm_w, router_w, router_b, tm=512):
    t, d = x2.shape
    row = lambda i: (i, 0)
    fixed = lambda i: (0, 0)
    rw = jnp.pad(router_w, ((0, 0), (0, LANES - N_EXPERTS)))
    rb = jnp.pad(router_b, (0, LANES - N_EXPERTS)).reshape(1, LANES)
    return pl.pallas_call(
        _outproj_router_kernel,
        grid=(t // tm,),
        in_specs=[pl.BlockSpec((tm, ATT_WIDTH), row), pl.BlockSpec((tm, MLSTM_WIDTH), row),
                  pl.BlockSpec((tm, d), row), pl.BlockSpec((ATT_WIDTH, d), fixed),
                  pl.BlockSpec((MLSTM_WIDTH, d), fixed), pl.BlockSpec((1, d), fixed),
                  pl.BlockSpec((d, LANES), fixed), pl.BlockSpec((1, LANES), fixed)],
        out_specs=[pl.BlockSpec((tm, d), row), pl.BlockSpec((tm, d), row),
                   pl.BlockSpec((tm, LANES), row), pl.BlockSpec((tm, LANES), row),
                   pl.BlockSpec((SUBLANES, LANES), fixed)],
        out_shape=[jax.ShapeDtypeStruct((t, d), F32), jax.ShapeDtypeStruct((t, d), F32),
                   jax.ShapeDtypeStruct((t, LANES), jnp.int32), jax.ShapeDtypeStruct((t, LANES), F32),
                   jax.ShapeDtypeStruct((SUBLANES, LANES), F32)],
        compiler_params=_cparams("arbitrary"),
        name="outproj_router",
    )(att, mem, x2, w_att, w_mem, ffn_norm_w.reshape(1, d), rw, rb)


DISPATCH_ROWS = 256
ISSUE_ROWS = 16
DMA_PRIORITIES = 2


def _dispatch_kernel(fill_ref, nused_ref, dest_ref, hn_ref, x_hbm, stage, zero_buf, sem, zsem, csem, bsem):
    n_blocks = x_hbm.shape[0] // EXPERT_ROWS
    step = pl.program_id(0)
    slot = step % 2
    stage[slot] = hn_ref[...].reshape(stage.shape[1:])

    def token_row(g, r, kk):
        return pltpu.make_async_copy(stage.at[slot, g, pl.ds(r, 1), :],
                                     x_hbm.at[pl.ds(dest_ref[0, 0, (g * ISSUE_ROWS + r) * TOP_K + kk], 1), :],
                                     sem.at[slot])

    def wait_tokens(which):
        all_rows = x_hbm.at[pl.ds(0, TOP_K * DISPATCH_ROWS), :]
        pltpu.make_async_copy(all_rows, all_rows, sem.at[which]).wait()

    def pad_row(r):
        return pltpu.make_async_copy(zero_buf.at[pl.ds(0, 1), :], x_hbm.at[pl.ds(r, 1), :], zsem)

    def pad_chunk(start, size):
        return pltpu.make_async_copy(zero_buf.at[pl.ds(0, size), :],
                                     x_hbm.at[pl.ds(pl.multiple_of(start, SUBLANES), size), :], csem)

    def pad_block(b):
        start = pl.multiple_of(b * EXPERT_ROWS, EXPERT_ROWS)
        return pltpu.make_async_copy(zero_buf, x_hbm.at[pl.ds(start, EXPERT_ROWS), :], bsem)

    def start_tokens(g, c):
        for r in range(ISSUE_ROWS):
            for kk in range(TOP_K):
                token_row(g, r, kk).start(priority=(r * TOP_K + kk) % DMA_PRIORITIES)
        return c

    chunk_sizes = [s for s in (256, 128, 64, 32, 16, 8) if s < EXPERT_ROWS]

    def for_padding(fn):
        def per_expert(e, c):
            first = fill_ref[0, e]
            end = first + fill_ref[1, e]
            aligned = jnp.minimum(((first + SUBLANES - 1) // SUBLANES) * SUBLANES, end)
            lax.fori_loop(first, aligned, lambda r, cc: (fn(pad_row(r)), cc)[1], 0)
            left = end - aligned
            cur = aligned
            for size in chunk_sizes:
                take = (left & size) != 0
                pl.when(take)(functools.partial(lambda cur, size: fn(pad_chunk(cur, size)), cur, size))
                cur = cur + jnp.where(take, size, 0)
            return c
        lax.fori_loop(0, N_EXPERTS, per_expert, 0)
        lax.fori_loop(nused_ref[0], n_blocks, lambda b, cc: (fn(pad_block(b)), cc)[1], 0)

    lax.fori_loop(0, DISPATCH_ROWS // ISSUE_ROWS, start_tokens, 0)

    @pl.when(pl.program_id(0) == 0)
    def _():
        zero_buf[...] = jnp.zeros_like(zero_buf)
        for_padding(lambda cp: cp.start())
        for_padding(lambda cp: cp.wait())

    @pl.when(step > 0)
    def _():
        wait_tokens(1 - slot)

    @pl.when(step == pl.num_programs(0) - 1)
    def _():
        wait_tokens(slot)


def _dispatch_rows(hn, dest, fill, n_blocks_used, n_rows):
    t, d = hn.shape
    tm = DISPATCH_ROWS
    nb = t // tm
    return pl.pallas_call(
        _dispatch_kernel,
        grid_spec=pltpu.PrefetchScalarGridSpec(
            num_scalar_prefetch=2,
            grid=(nb,),
            in_specs=[pl.BlockSpec((1, 1, tm * TOP_K), lambda i, f, n: (i, 0, 0), memory_space=pltpu.SMEM),
                      pl.BlockSpec((tm, d), lambda i, f, n: (i, 0))],
            out_specs=pl.BlockSpec(memory_space=pl.ANY),
            scratch_shapes=[pltpu.VMEM((2, tm // ISSUE_ROWS, ISSUE_ROWS, d), hn.dtype),
                            pltpu.VMEM((EXPERT_ROWS, d), hn.dtype),
                            pltpu.SemaphoreType.DMA((2,)), pltpu.SemaphoreType.DMA(()),
                            pltpu.SemaphoreType.DMA(()), pltpu.SemaphoreType.DMA(())],
        ),
        out_shape=jax.ShapeDtypeStruct((n_rows, d), hn.dtype),
        compiler_params=_cparams("arbitrary"),
        name="moe_dispatch",
    )(fill, n_blocks_used, dest.reshape(nb, 1, tm * TOP_K), hn)


def _expert_kernel(be_ref, nu_ref, nxt_ref, par_ref, x_ref, bgu_ref, bdn_ref, wgu_hbm, wdn_hbm, o_ref,
                   wgu_f, wdn_f, wgu_s, wdn_s, sem):
    b = pl.program_id(0)
    e = be_ref[b]
    e_prev = be_ref[jnp.maximum(b - 1, 0)]
    slot = par_ref[b]
    used = b < nu_ref[0]
    d_e = wdn_s.shape[0]

    def fetch(expert, into):
        return (pltpu.make_async_copy(wgu_hbm.at[expert], wgu_f.at[into], sem.at[0, into]),
                pltpu.make_async_copy(wdn_hbm.at[expert], wdn_f.at[into], sem.at[1, into]))

    @pl.when(b == 0)
    def _():
        for cp in fetch(e, slot):
            cp.start()

    @pl.when(used & ((b == 0) | (e != e_prev)))
    def _():
        for cp in fetch(e, slot):
            cp.wait()

        @pl.when(nxt_ref[b] >= 0)
        def _():
            for cp in fetch(nxt_ref[b], 1 - slot):
                cp.start()

        wgu_s[...] = wgu_f[slot].astype(BF16)
        wdn_s[...] = wdn_f[slot].astype(BF16)

    @pl.when(used)
    def _():
        xb = x_ref[...].astype(BF16)
        bgu = bgu_ref[0]
        glu = jnp.dot(xb, wgu_s[:, :d_e], preferred_element_type=F32) + bgu[:, :d_e]
        lin = jnp.dot(xb, wgu_s[:, d_e:], preferred_element_type=F32) + bgu[:, d_e:]
        glu = jnp.minimum(glu, SWIGLU_LIMIT)
        lin = jnp.clip(lin, -SWIGLU_LIMIT, SWIGLU_LIMIT)
        act = glu * (1.0 / (1.0 + jnp.exp(-SWIGLU_ALPHA * glu))) * (lin + 1.0)
        o_ref[...] = jnp.dot(act.astype(BF16), wdn_s[...], preferred_element_type=F32) + bdn_ref[0]

    @pl.when(jnp.logical_not(used))
    def _():
        o_ref[...] = jnp.zeros_like(o_ref)


def _expert_ffn(x_buf, blk_expert, n_blocks_used, next_expert, run_parity, w_gu, b_gu, w_dn, b_dn):
    n_rows, d = x_buf.shape
    ne, _, d2 = w_gu.shape
    d_e = w_dn.shape[1]
    nb = n_rows // EXPERT_ROWS
    last_used = lambda b, be, nu, nx, pr: (jnp.minimum(b, jnp.maximum(nu[0] - 1, 0)), 0)
    by_expert = lambda b, be, nu, nx, pr: (be[b], 0, 0)
    return pl.pallas_call(
        _expert_kernel,
        grid_spec=pltpu.PrefetchScalarGridSpec(
            num_scalar_prefetch=4,
            grid=(nb,),
            in_specs=[pl.BlockSpec((EXPERT_ROWS, d), last_used),
                      pl.BlockSpec((1, 1, d2), by_expert),
                      pl.BlockSpec((1, 1, d), by_expert),
                      pl.BlockSpec(memory_space=pl.ANY),
                      pl.BlockSpec(memory_space=pl.ANY)],
            out_specs=pl.BlockSpec((EXPERT_ROWS, d), lambda b, be, nu, nx, pr: (b, 0)),
            scratch_shapes=[pltpu.VMEM((2, d, d2), w_gu.dtype), pltpu.VMEM((2, d_e, d), w_dn.dtype),
                            pltpu.VMEM((d, d2), BF16), pltpu.VMEM((d_e, d), BF16),
                            pltpu.SemaphoreType.DMA((2, 2))],
        ),
        out_shape=jax.ShapeDtypeStruct((n_rows, d), F32),
        compiler_params=_cparams("arbitrary"),
        name="moe_experts",
    )(blk_expert, n_blocks_used, next_expert, run_parity, x_buf, b_gu.reshape(ne, 1, d2), b_dn.reshape(ne, 1, d),
      w_gu, w_dn)


COMBINE_ROWS = 256


def _combine_kernel(pos_ref, pos_next_ref, g_ref, x1_ref, nw_ref, y_hbm, o_ref, buf, sem, *, apply_norm):
    tm = COMBINE_ROWS
    step = pl.program_id(0)
    slot = step % 2

    def gather(p_ref, into):
        def issue(g, c):
            base = g * (ISSUE_ROWS * TOP_K)
            for r in range(ISSUE_ROWS):
                for kk in range(TOP_K):
                    pltpu.make_async_copy(y_hbm.at[pl.ds(p_ref[0, 0, base + (r * TOP_K + kk)], 1), :],
                                          buf.at[into, g, pl.ds(kk * ISSUE_ROWS + r, 1), :],
                                          sem.at[into]).start(priority=(r * TOP_K + kk) % DMA_PRIORITIES)
            return c
        lax.fori_loop(0, tm // ISSUE_ROWS, issue, 0)

    @pl.when(step == 0)
    def _():
        gather(pos_ref, 0)

    @pl.when(step + 1 < pl.num_programs(0))
    def _():
        gather(pos_next_ref, 1 - slot)

    pltpu.make_async_copy(buf.at[slot], buf.at[slot], sem.at[slot]).wait()

    g = g_ref[...]
    acc = x1_ref[...]
    for kk in range(TOP_K):
        rows = buf[slot, :, pl.ds(kk * ISSUE_ROWS, ISSUE_ROWS), :].reshape(tm, acc.shape[1])
        acc = acc + rows * _lane_col(g, kk)
    if apply_norm:
        var = jnp.mean(acc * acc, axis=-1, keepdims=True)
        acc = (acc * lax.rsqrt(var + NORM_EPS)) * nw_ref[...]
    o_ref[...] = acc


def _combine(y_buf, pos, gates, x1, final_norm_w, apply_norm):
    t, d = x1.shape
    tm = COMBINE_ROWS
    nb = t // tm
    row = lambda i: (i, 0)
    pos3 = pos.reshape(nb, 1, tm * TOP_K)
    return pl.pallas_call(
        functools.partial(_combine_kernel, apply_norm=apply_norm),
        grid=(nb,),
        in_specs=[pl.BlockSpec((1, 1, tm * TOP_K), lambda i: (i, 0, 0), memory_space=pltpu.SMEM),
                  pl.BlockSpec((1, 1, tm * TOP_K), lambda i: (jnp.minimum(i + 1, nb - 1), 0, 0),
                               memory_space=pltpu.SMEM),
                  pl.BlockSpec((tm, LANES), row), pl.BlockSpec((tm, d), row),
                  pl.BlockSpec((1, d), lambda i: (0, 0)), pl.BlockSpec(memory_space=pl.ANY)],
        out_specs=pl.BlockSpec((tm, d), row),
        out_shape=jax.ShapeDtypeStruct((t, d), F32),
        scratch_shapes=[pltpu.VMEM((2, tm // ISSUE_ROWS, TOP_K * ISSUE_ROWS, d), F32),
                        pltpu.SemaphoreType.DMA((2,))],
        compiler_params=_cparams("arbitrary"),
        name="moe_combine",
    )(pos3, pos3, gates, x1, final_norm_w.reshape(1, d), y_buf)


def _routing_tables(top_idx, rank, counts, n_tokens):
    padded = (counts + EXPERT_ROWS - 1) // EXPERT_ROWS * EXPERT_ROWS
    pad_end = jnp.cumsum(padded)
    pad_start = pad_end - padded
    onehot = (top_idx[:, :, None] == jnp.arange(N_EXPERTS, dtype=jnp.int32)[None, None, :]).astype(F32)
    start_of = jnp.einsum("tke,e->tk", onehot, pad_start.astype(F32), precision=lax.Precision.HIGHEST)
    dest = (start_of.astype(jnp.int32) + rank).astype(jnp.int32)
    n_rows = n_tokens * TOP_K + N_EXPERTS * EXPERT_ROWS
    n_blocks = n_rows // EXPERT_ROWS
    blk_start = jnp.arange(n_blocks, dtype=jnp.int32) * EXPERT_ROWS
    blk_expert = jnp.minimum(jnp.sum((pad_end[None, :] <= blk_start[:, None]).astype(jnp.int32), axis=1),
                             N_EXPERTS - 1).astype(jnp.int32)
    n_blocks_used = (pad_end[-1:] // EXPERT_ROWS).astype(jnp.int32)
    fill = jnp.stack([pad_start + counts, padded - counts]).astype(jnp.int32)
    first = jnp.concatenate([jnp.ones((1,), jnp.bool_), blk_expert[1:] != blk_expert[:-1]])
    run_parity = ((jnp.cumsum(first.astype(jnp.int32)) - 1) % 2).astype(jnp.int32)
    next_run = (pad_end[blk_expert] // EXPERT_ROWS).astype(jnp.int32)
    next_expert = jnp.where(next_run < n_blocks_used[0],
                            blk_expert[jnp.minimum(next_run, n_blocks - 1)], -1).astype(jnp.int32)
    return dest, fill, blk_expert, n_blocks_used, next_expert, run_parity, n_rows


def kernel(x, mix_norm_w, w_in, conv_w, conv_b, igate_b, fgate_b, mlstm_norm_w, w_out, ffn_norm_w,
           router_w, router_b, w_gate_up, b_gate_up, w_down, b_down, final_norm_w):
    bsz, seq, d = x.shape
    depth = w_in.shape[0]
    t = bsz * seq
    k_sel = min(TOPK_MAX, seq // 4)
    head_order = [h for s in range(Q_PER_KV) for h in (s, Q_PER_KV + s)]
    x2 = x.reshape(t, d)
    for layer in range(depth):
        p = _in_projection(x2, mix_norm_w[layer], w_in[layer], igate_b[layer], fgate_b[layer], seq)
        r3 = lambda a: a.reshape(bsz, seq, a.shape[-1])
        att = _dsa_attention(r3(p["q"]), r3(p["iq"]), r3(p["iw"]), r3(p["k"]), r3(p["v"]), r3(p["ik"]), k_sel)
        mem = _mlstm_mixer(r3(p["mq"]), r3(p["mk"]), r3(p["mv"]), r3(p["mo"]), r3(p["g"]),
                           conv_w[layer], conv_b[layer], mlstm_norm_w[layer])
        wo = w_out[layer]
        w_att = wo[:ATT_WIDTH].reshape(ATT_HEADS, HEAD_DIM, d)[jnp.asarray(head_order)].reshape(ATT_WIDTH, d)
        x1, hn, top_i, top_g, counts = _outproj_router(
            att.reshape(t, ATT_WIDTH), mem.reshape(t, MLSTM_WIDTH), x2, w_att.astype(BF16),
            wo[ATT_WIDTH:].astype(BF16), ffn_norm_w[layer], router_w[layer], router_b[layer])
        dest, fill, blk_expert, n_blocks_used, next_expert, run_parity, n_rows = _routing_tables(
            top_i[:, :TOP_K], top_i[:, TOP_K:2 * TOP_K], counts[0, :N_EXPERTS].astype(jnp.int32), t)
        x_buf = _dispatch_rows(hn, dest, fill, n_blocks_used, n_rows)
        y_buf = _expert_ffn(x_buf, blk_expert, n_blocks_used, next_expert, run_parity, w_gate_up[layer],
                            b_gate_up[layer], w_down[layer], b_down[layer])
        x2 = _combine(y_buf, dest, top_g, x1, final_norm_w, apply_norm=(layer == depth - 1))
    return x2.reshape(bsz, seq, d)
```

```python
import functools

import jax
import jax.numpy as jnp
import numpy as np
from jax import lax
from jax.experimental import pallas as pl
from jax.experimental.pallas import tpu as pltpu

F32 = jnp.float32
BF16 = jnp.bfloat16

LANES = 128
SUBLANES = 8
VMEM_LIMIT_BYTES = 56 * 1024 * 1024

NORM_EPS = 1e-6
ROPE_THETA = 500000.0
CHUNK = 64

HEAD_DIM = 64
ATT_HEADS = 8
ATT_KV_HEADS = 2
Q_PER_KV = ATT_HEADS // ATT_KV_HEADS
ATT_ROT_DIM = HEAD_DIM // 4
IDX_HEADS = 8
IDX_DIM = 32
IDX_ROT_DIM = IDX_DIM // 4
TOPK_MAX = 256
ATT_WIDTH = ATT_HEADS * HEAD_DIM

MLSTM_HEAD_DIM = 128
MLSTM_HEADS = 4
MLSTM_WIDTH = MLSTM_HEADS * MLSTM_HEAD_DIM
CONV_WIDTH = 4

N_EXPERTS = 32
TOP_K = 4
SWIGLU_ALPHA = 1.702
SWIGLU_LIMIT = 7.0

Q_BLOCK = 128
KEY_TILE = 256
BISECT_ITERS = 32
BISECT_GROUP = 4
FOLD_ROWS = 32
TILES_PER_STEP = 2
V_ROWS = LANES + 16
MLSTM_CHUNK = 256
EXPERT_ROWS = 512
ROUTE_ROWS = 256
NEG_BIG = -1e30


def _cparams(*sem):
    return pltpu.CompilerParams(dimension_semantics=sem, vmem_limit_bytes=VMEM_LIMIT_BYTES)


_SEG_WIDTHS = (("q", 512), ("k", 128), ("v", 128), ("iq", 256), ("ik", 128), ("iw", 128),
               ("mq", 512), ("mk", 512), ("mv", 512), ("mo", 512), ("g", 128))
_SEG = {}
_off = 0
for _name, _w in _SEG_WIDTHS:
    _SEG[_name] = (_off, _w)
    _off += _w
PACKED_WIDTH = _off


def _lane_col(x, idx):
    lane = lax.broadcasted_iota(jnp.int32, x.shape, 1)
    return jnp.sum(jnp.where(lane == idx, x, 0.0), axis=1, keepdims=True)


def _inproj_kernel(x_ref, nw_ref, w_ref, gb_ref, ca_ref, sa1_ref, sa2_ref, ci_ref, si1_ref, si2_ref,
                   q_ref, k_ref, v_ref, iq_ref, ik_ref, iw_ref, mq_ref, mk_ref, mv_ref, mo_ref, g_ref,
                   *, idx_w_scale):
    x = x_ref[...]
    var = jnp.mean(x * x, axis=-1, keepdims=True)
    h = (x * lax.rsqrt(var + NORM_EPS)) * nw_ref[...]
    hb = h.astype(BF16)

    def proj(name):
        lo, width = _SEG[name]
        return jnp.dot(hb, w_ref[:, lo:lo + width], preferred_element_type=F32)

    def rope(p, c_ref, s1_ref, s2_ref, half):
        c, s1, s2 = c_ref[...], s1_ref[...], s2_ref[...]
        outs = []
        for j in range(p.shape[1] // LANES):
            xs = p[:, j * LANES:(j + 1) * LANES]
            outs.append(xs * c + pltpu.roll(xs, LANES - half, 1) * s1 + pltpu.roll(xs, half, 1) * s2)
        return outs[0] if len(outs) == 1 else jnp.concatenate(outs, axis=1)

    att_scale = float(HEAD_DIM ** -0.5 * np.log2(np.e))
    q_ref[...] = (rope(proj("q"), ca_ref, sa1_ref, sa2_ref, ATT_ROT_DIM // 2) * att_scale).astype(BF16)
    k_ref[...] = rope(proj("k"), ca_ref, sa1_ref, sa2_ref, ATT_ROT_DIM // 2).astype(BF16)
    v_ref[...] = proj("v").astype(BF16)
    iq_ref[...] = rope(proj("iq"), ci_ref, si1_ref, si2_ref, IDX_ROT_DIM // 2).astype(BF16)
    ik_ref[...] = rope(proj("ik"), ci_ref, si1_ref, si2_ref, IDX_ROT_DIM // 2).astype(BF16)
    iw_ref[...] = proj("iw") * idx_w_scale
    mq_ref[...] = proj("mq")
    mk_ref[...] = proj("mk")
    mv_ref[...] = proj("mv").astype(BF16)
    mo_ref[...] = proj("mo")
    g = proj("g") + gb_ref[...]
    lane = lax.broadcasted_iota(jnp.int32, g.shape, 1)
    log_f = jnp.minimum(g, 0.0) - jnp.log(1.0 + jnp.exp(-jnp.abs(g)))
    g_ref[...] = jnp.where(lane < MLSTM_HEADS, g, log_f)


def _rope_tables(seq, rot_dim, head_dim):
    pos = jnp.arange(seq, dtype=F32)
    inv_freq = ROPE_THETA ** (-jnp.arange(0, rot_dim, 2, dtype=F32) / rot_dim)
    ang = pos[:, None] * inv_freq[None, :]
    cos, sin = jnp.cos(ang), jnp.sin(ang)
    half = rot_dim // 2
    lane = np.arange(LANES) % head_dim
    fidx = lane % half
    first = lane < half
    second = (lane >= half) & (lane < rot_dim)
    cos_l, sin_l = cos[:, fidx], sin[:, fidx]
    c = jnp.where(jnp.asarray(first | second)[None, :], cos_l, 1.0)
    s1 = jnp.where(jnp.asarray(first)[None, :], -sin_l, 0.0)
    s2 = jnp.where(jnp.asarray(second)[None, :], sin_l, 0.0)
    return c, s1, s2


def _pack_w_in(w_in):
    sizes = (ATT_WIDTH, ATT_KV_HEADS * HEAD_DIM, ATT_KV_HEADS * HEAD_DIM, IDX_HEADS * IDX_DIM, IDX_DIM,
             IDX_HEADS, MLSTM_WIDTH, MLSTM_WIDTH, MLSTM_WIDTH, MLSTM_HEADS, MLSTM_HEADS, MLSTM_WIDTH)
    offs = np.concatenate([[0], np.cumsum(sizes)])
    aq, ak, av, iq, ik, iw, mq, mk, mv, mi, mf, mo = (w_in[:, offs[i]:offs[i + 1]] for i in range(12))
    d = w_in.shape[0]
    head_order = [h for s in range(Q_PER_KV) for h in (s, Q_PER_KV + s)]
    aq = aq.reshape(d, ATT_HEADS, HEAD_DIM)[:, head_order, :].reshape(d, ATT_WIDTH)
    ik_rep = jnp.tile(ik, (1, LANES // IDX_DIM))
    iw_pad = jnp.pad(iw, ((0, 0), (0, LANES - IDX_HEADS)))
    g_pad = jnp.pad(jnp.concatenate([mi, mf], axis=1), ((0, 0), (0, LANES - 2 * MLSTM_HEADS)))
    packed = jnp.concatenate([aq, ak, av, iq, ik_rep, iw_pad, mq, mk, mv, mo, g_pad], axis=1)
    assert packed.shape[1] == PACKED_WIDTH
    return packed.astype(BF16)


def _in_projection(x2, norm_w, w_in, igate_b, fgate_b, seq, tm=1024):
    t, d = x2.shape
    assert t % tm == 0 and seq % tm == 0
    wp = _pack_w_in(w_in)
    gb = jnp.pad(jnp.concatenate([igate_b, fgate_b]), (0, LANES - 2 * MLSTM_HEADS)).reshape(1, LANES).astype(F32)
    ca, sa1, sa2 = _rope_tables(seq, ATT_ROT_DIM, HEAD_DIM)
    ci, si1, si2 = _rope_tables(seq, IDX_ROT_DIM, IDX_DIM)
    n_pos_blocks = seq // tm
    row = lambda i: (i, 0)
    fixed = lambda i: (0, 0)
    pos = lambda i: (i % n_pos_blocks, 0)
    tab = pl.BlockSpec((tm, LANES), pos)
    out_defs = (("q", 512, BF16), ("k", 128, BF16), ("v", 128, BF16), ("iq", 256, BF16), ("ik", 128, BF16),
                ("iw", 128, F32), ("mq", 512, F32), ("mk", 512, F32), ("mv", 512, BF16), ("mo", 512, F32),
                ("g", 128, F32))
    idx_w_scale = (IDX_HEADS ** -0.5) * (IDX_DIM ** -0.5)
    outs = pl.pallas_call(
        functools.partial(_inproj_kernel, idx_w_scale=idx_w_scale),
        grid=(t // tm,),
        in_specs=[pl.BlockSpec((tm, d), row), pl.BlockSpec((1, d), fixed),
                  pl.BlockSpec((d, PACKED_WIDTH), fixed), pl.BlockSpec((1, LANES), fixed),
                  tab, tab, tab, tab, tab, tab],
        out_specs=[pl.BlockSpec((tm, w), row) for _, w, _ in out_defs],
        out_shape=[jax.ShapeDtypeStruct((t, w), dt) for _, w, dt in out_defs],
        compiler_params=_cparams("parallel"),
        name="in_projection",
    )(x2, norm_w.reshape(1, d), wp, gb, ca, sa1, sa2, ci, si1, si2)
    return dict(zip([n for n, _, _ in out_defs], outs))


def _dsa_kernel(q_ref, iq_ref, iw_ref, k_ref, vt_ref, ik_ref, o_ref, score_ref, acc_ref, vk_ref, jcut_ref, p_ref,
                *, k_sel, index_iters):
    i = pl.program_id(1)
    n_keys = Q_BLOCK * (i + 1)
    n_tiles = (n_keys + KEY_TILE - 1) // KEY_TILE
    n_steps = n_tiles // TILES_PER_STEP
    n_left = n_tiles - n_steps * TILES_PER_STEP

    def for_tiles(body, init, lead=None):
        def run(tiles, carry):
            opened = [lead(kt) if lead is not None else None for kt in tiles]
            for kt, head in zip(tiles, opened):
                carry = body(kt, carry) if lead is None else body(kt, head, carry)
            return carry

        carry = lax.fori_loop(0, n_steps, lambda j, c: run([j * TILES_PER_STEP + u for u in range(TILES_PER_STEP)], c),
                              init)
        for u in range(TILES_PER_STEP - 1):
            carry = lax.cond(n_left > u, functools.partial(run, [n_steps * TILES_PER_STEP + u]), lambda c: c, carry)
        return carry
    n_stack = ATT_HEADS * Q_BLOCK

    lane = lax.broadcasted_iota(jnp.int32, (Q_BLOCK, LANES), 1)
    iq = iq_ref[0]
    per_slab = LANES // IDX_DIM
    qi_stack = jnp.concatenate(
        [jnp.where((lane // IDX_DIM) == (h % per_slab), iq[:, (h // per_slab) * LANES:(h // per_slab + 1) * LANES], 0)
         for h in range(IDX_HEADS)], axis=0)
    q = q_ref[0]
    q_stack = jnp.concatenate(
        [jnp.where((lane // HEAD_DIM) == g, q[:, s * LANES:(s + 1) * LANES], 0)
         for s in range(Q_PER_KV) for g in range(ATT_KV_HEADS)], axis=0)
    w_t = iw_ref[0].T

    q_lane = lax.broadcasted_iota(jnp.int32, (1, Q_BLOCK), 1)
    key_limit = (2 * i + 1 + (q_lane >= CHUNK).astype(jnp.int32)) * CHUNK
    k_eff = jnp.minimum(key_limit, k_sel).astype(F32)
    key_iota = lax.broadcasted_iota(jnp.int32, (KEY_TILE, Q_BLOCK), 0)
    nt_dims = (((1,), (1,)), ((), ()))

    def idx_lead(kt):
        start = pl.multiple_of(kt * KEY_TILE, KEY_TILE)
        return lax.dot_general(ik_ref[0, pl.ds(start, KEY_TILE), :], qi_stack, nt_dims,
                               preferred_element_type=F32)

    def idx_body(kt, z, carry):
        rmin, rmax = carry
        start = pl.multiple_of(kt * KEY_TILE, KEY_TILE)
        sc = jnp.zeros((KEY_TILE, Q_BLOCK), F32)
        for h in range(IDX_HEADS):
            sc = sc + jnp.maximum(z[:, h * Q_BLOCK:(h + 1) * Q_BLOCK], 0.0) * w_t[h:h + 1, :]
        adm = (key_iota + start) < key_limit
        score_ref[pl.ds(start, KEY_TILE), :] = jnp.where(adm, sc, -jnp.inf)
        rmin = jnp.minimum(rmin, jnp.min(jnp.where(adm, sc, jnp.inf), axis=0, keepdims=True))
        rmax = jnp.maximum(rmax, jnp.max(jnp.where(adm, sc, -jnp.inf), axis=0, keepdims=True))
        return rmin, rmax

    init = (jnp.full((1, Q_BLOCK), jnp.inf, F32), jnp.full((1, Q_BLOCK), -jnp.inf, F32))
    lo, hi = for_tiles(idx_body, init, lead=idx_lead)

    def fold(x):
        return x.reshape(KEY_TILE // FOLD_ROWS, FOLD_ROWS, Q_BLOCK)

    def count(pred):
        def body(kt, acc):
            start = pl.multiple_of(kt * KEY_TILE, KEY_TILE)
            hit = pred(score_ref[pl.ds(start, KEY_TILE), :], key_iota + start).astype(F32)
            return acc + jnp.sum(fold(hit), axis=0)
        acc = for_tiles(body, jnp.zeros((FOLD_ROWS, Q_BLOCK), F32))
        return jnp.sum(acc, axis=0, keepdims=True)

    def unresolved(c_lo):
        return jnp.max(c_lo - k_eff) > 0.0

    def smallest_at_least(lo):
        def min_body(kt, acc):
            start = pl.multiple_of(kt * KEY_TILE, KEY_TILE)
            x = score_ref[pl.ds(start, KEY_TILE), :]
            return jnp.minimum(acc, jnp.min(fold(jnp.where(x >= lo, x, jnp.inf)), axis=0))
        return jnp.min(for_tiles(min_body, jnp.full((FOLD_ROWS, Q_BLOCK), jnp.inf, F32)), axis=0, keepdims=True)

    def bisect(carry):
        it, lo, hi, c_lo = carry
        for _ in range(BISECT_GROUP):
            mid = 0.5 * (lo + hi)
            c_mid = count(lambda x, _: x >= mid)
            ok = c_mid >= k_eff
            lo, hi, c_lo = jnp.where(ok, mid, lo), jnp.where(ok, hi, mid), jnp.where(ok, c_mid, c_lo)
        return it + BISECT_GROUP, lo, hi, c_lo

    _, lo, _, c_lo = lax.while_loop(lambda c: (c[0] < BISECT_ITERS) & unresolved(c[3]), bisect,
                                    (jnp.int32(0), lo, hi, key_limit.astype(F32)))

    vk_ref[...] = jnp.broadcast_to(lo, vk_ref.shape)
    jcut_ref[...] = jnp.full(jcut_ref.shape, jnp.iinfo(jnp.int32).max, jnp.int32)

    @pl.when(unresolved(c_lo))
    def _():
        vk = smallest_at_least(lo)
        need = k_eff - count(lambda x, _: x > vk)

        def jbisect(_, carry):
            jlo, jhi = carry
            jmid = jnp.right_shift(jlo + jhi, 1)
            ok = count(lambda x, kidx: (x == vk) & (kidx < jmid)) >= need
            return jnp.where(ok, jlo, jmid), jnp.where(ok, jmid, jhi)

        j0 = (jnp.zeros((1, Q_BLOCK), jnp.int32), jnp.full((1, Q_BLOCK), n_keys, jnp.int32))
        _, jcut = lax.fori_loop(0, index_iters, jbisect, j0)
        vk_ref[...] = jnp.broadcast_to(vk, vk_ref.shape)
        jcut_ref[...] = jnp.broadcast_to(jcut, jcut_ref.shape)

    vk = vk_ref[0:1, :]
    jcut = jcut_ref[0:1, :]

    r_i = lax.broadcasted_iota(jnp.int32, (Q_BLOCK, LANES), 0)
    eye = jnp.where(r_i == lane, 1.0, 0.0).astype(BF16)
    q_aug = jnp.concatenate([q_stack, jnp.concatenate([eye] * ATT_HEADS, axis=0)], axis=1)
    acc_ref[...] = jnp.zeros_like(acc_ref)

    def att_lead(kt):
        start = pl.multiple_of(kt * KEY_TILE, KEY_TILE)
        x = score_ref[pl.ds(start, KEY_TILE), :]
        sel = (x > vk) | ((x == vk) & ((key_iota + start) < jcut))
        bias = jnp.where(sel, 0.0, NEG_BIG).astype(BF16)
        k_aug = jnp.concatenate([k_ref[0, pl.ds(start, KEY_TILE), :], bias], axis=1)
        return lax.dot_general(k_aug, q_aug, nt_dims, preferred_element_type=F32)

    def accumulate(kt, alpha):
        acc_ref[...] = alpha * acc_ref[...] + jnp.dot(vt_ref[0, kt], p_ref[...], preferred_element_type=F32)

    def att_body(kt, s, carry):
        m, alpha = carry
        accumulate(jnp.maximum(kt - 1, 0), alpha)
        m_new = jnp.maximum(m, jnp.max(s, axis=0, keepdims=True))
        p_ref[...] = jnp.exp2(s - m_new).astype(BF16)
        return m_new, jnp.exp2(m - m_new)

    p_ref[...] = jnp.zeros_like(p_ref)
    init = (jnp.full((1, n_stack), NEG_BIG, F32), jnp.ones((1, n_stack), F32))
    _, alpha = for_tiles(att_body, init, lead=att_lead)
    accumulate(n_tiles - 1, alpha)

    out_t = acc_ref[0:LANES, :] / acc_ref[LANES:LANES + 1, :]
    for s in range(Q_PER_KV):
        o0 = out_t[:, (2 * s) * Q_BLOCK:(2 * s + 1) * Q_BLOCK].T
        o1 = out_t[:, (2 * s + 1) * Q_BLOCK:(2 * s + 2) * Q_BLOCK].T
        o_ref[0, :, s * LANES:(s + 1) * LANES] = jnp.where(lane < HEAD_DIM, o0, o1).astype(o_ref.dtype)


def _dsa_attention(q, iq, iw, k, v, ik, k_sel):
    bsz, seq, _ = q.shape
    assert seq % (TILES_PER_STEP * KEY_TILE) == 0 and seq % Q_BLOCK == 0
    n_kt = seq // KEY_TILE
    vt = v.reshape(bsz, n_kt, KEY_TILE, LANES).transpose(0, 1, 3, 2)
    ones_rows = jnp.zeros((bsz, n_kt, V_ROWS - LANES, KEY_TILE), v.dtype).at[:, :, 0, :].set(1.0)
    vt = jnp.concatenate([vt, ones_rows], axis=2)
    blk = lambda w: pl.BlockSpec((1, Q_BLOCK, w), lambda b, i: (b, i, 0))
    full = pl.BlockSpec((1, seq, LANES), lambda b, i: (b, 0, 0))
    return pl.pallas_call(
        functools.partial(_dsa_kernel, k_sel=k_sel, index_iters=seq.bit_length()),
        grid=(bsz, seq // Q_BLOCK),
        in_specs=[blk(ATT_WIDTH), blk(IDX_HEADS * IDX_DIM), blk(LANES), full,
                  pl.BlockSpec((1, n_kt, V_ROWS, KEY_TILE), lambda b, i: (b, 0, 0, 0)), full],
        out_specs=blk(ATT_WIDTH),
        out_shape=jax.ShapeDtypeStruct((bsz, seq, ATT_WIDTH), BF16),
        scratch_shapes=[pltpu.VMEM((seq, Q_BLOCK), F32), pltpu.VMEM((V_ROWS, ATT_HEADS * Q_BLOCK), F32),
                        pltpu.VMEM((SUBLANES, Q_BLOCK), F32), pltpu.VMEM((SUBLANES, Q_BLOCK), jnp.int32),
                        pltpu.VMEM((KEY_TILE, ATT_HEADS * Q_BLOCK), BF16)],
        compiler_params=_cparams("parallel", "parallel"),
        name="dsa_attention",
    )(q, iq, iw, k, vt, ik)


def _mlstm_kernel(mq_ref, mk_ref, mv_ref, mo_ref, gc_ref, gr_ref, cw_ref, cb_ref, nw_ref, o_ref,
                  c_ref, n_ref, m_ref, prev_ref):
    L = MLSTM_CHUNK
    W = MLSTM_WIDTH
    dh = MLSTM_HEAD_DIM

    @pl.when(pl.program_id(1) == 0)
    def _():
        c_ref[...] = jnp.zeros_like(c_ref)
        n_ref[...] = jnp.zeros_like(n_ref)
        m_ref[...] = jnp.zeros_like(m_ref)
        prev_ref[...] = jnp.zeros_like(prev_ref)

    row = lax.broadcasted_iota(jnp.int32, (SUBLANES, W), 0)

    def conv_silu(x, prev8, w4, b):
        y = x * w4[CONV_WIDTH - 1:CONV_WIDTH, :] + b
        for d in range(1, CONV_WIDTH):
            xr = pltpu.roll(x, d, 0)
            head = jnp.where(row < d, pltpu.roll(prev8, d, 0), xr[:SUBLANES])
            xs = jnp.concatenate([head, xr[SUBLANES:]], axis=0)
            y = y + xs * w4[CONV_WIDTH - 1 - d:CONV_WIDTH - d, :]
        return y / (1.0 + jnp.exp(-y))

    xq, xk = mq_ref[0], mk_ref[0]
    cw, cb = cw_ref[...], cb_ref[...]
    q = conv_silu(xq, prev_ref[:, :W], cw[:, :W], cb[:, :W])
    k = conv_silu(xk, prev_ref[:, W:], cw[:, W:], cb[:, W:]) * (dh ** -0.5)
    prev_ref[:, :W] = xq[L - SUBLANES:, :]
    prev_ref[:, W:] = xk[L - SUBLANES:, :]

    g_col = gc_ref[0]
    g_row = gr_ref[0]
    r_i = lax.broadcasted_iota(jnp.int32, (L, L), 0)
    c_i = lax.broadcasted_iota(jnp.int32, (L, L), 1)
    causal = r_i >= c_i
    tri = causal.astype(F32)
    b_col_all = jnp.dot(tri, g_col, preferred_element_type=F32, precision=lax.Precision.HIGHEST)
    b_row_all = jnp.dot(g_row, (r_i <= c_i).astype(F32), preferred_element_type=F32,
                        precision=lax.Precision.HIGHEST)

    v = mv_ref[0]
    mo = mo_ref[0]
    nw = nw_ref[...]
    for h in range(MLSTM_HEADS):
        hs = slice(h * dh, (h + 1) * dh)
        i_col = _lane_col(g_col, h)
        i_row = g_row[h:h + 1, :]
        b_col = _lane_col(b_col_all, MLSTM_HEADS + h)
        b_row = b_row_all[MLSTM_HEADS + h:MLSTM_HEADS + h + 1, :]
        b_last = jnp.sum(g_row[MLSTM_HEADS + h:MLSTM_HEADS + h + 1, :], axis=1, keepdims=True)
        m_prev = m_ref[h:h + 1, 0:1]
        c_mat = c_ref[h]
        n_vec = n_ref[h:h + 1, :]

        d_log = jnp.where(causal, b_col + (i_row - b_row), -jnp.inf)
        inter_log = b_col + m_prev
        m_t = jnp.maximum(inter_log, jnp.max(d_log, axis=1, keepdims=True))
        d_w = jnp.exp(d_log - m_t)
        inter_w = jnp.exp(inter_log - m_t)
        qh, kh, vh = q[:, hs], k[:, hs], v[:, hs]
        qb, kb = qh.astype(BF16), kh.astype(BF16)
        s = lax.dot_general(qb, kb, (((1,), (1,)), ((), ())), preferred_element_type=F32) * d_w
        num = (jnp.dot(s.astype(BF16), vh, preferred_element_type=F32)
               + inter_w * jnp.dot(qb, c_mat.astype(BF16), preferred_element_type=F32))
        den = jnp.sum(s, axis=1, keepdims=True) + inter_w * jnp.sum(qh * n_vec, axis=1, keepdims=True)
        hout = num / jnp.maximum(jnp.abs(den), jnp.exp(-m_t))

        w_log = b_last - b_col + i_col
        m_new = jnp.maximum(b_last + m_prev, jnp.max(w_log, axis=0, keepdims=True))
        decay = jnp.exp(b_last + m_prev - m_new)
        kw = kh * jnp.exp(w_log - m_new)
        c_ref[h] = decay * c_mat + jnp.dot(kw.T.astype(BF16), vh, preferred_element_type=F32)
        n_ref[h:h + 1, :] = decay * n_vec + jnp.sum(kw, axis=0, keepdims=True)
        m_ref[h:h + 1, :] = jnp.broadcast_to(m_new, (1, LANES))

        hn = hout * lax.rsqrt(jnp.mean(hout * hout, axis=-1, keepdims=True) + NORM_EPS)
        gate = 1.0 / (1.0 + jnp.exp(-mo[:, hs]))
        o_ref[0, :, hs] = (hn * nw[:, hs] * gate).astype(o_ref.dtype)


def _mlstm_mixer(mq, mk, mv, mo, g, conv_w, conv_b, norm_w):
    bsz, seq, w = mq.shape
    L = MLSTM_CHUNK
    assert seq % L == 0
    g_row = g[:, :, :SUBLANES].transpose(0, 2, 1)
    blk = lambda width: pl.BlockSpec((1, L, width), lambda b, c: (b, c, 0))
    fixed = lambda shape: pl.BlockSpec(shape, lambda b, c: (0, 0))
    return pl.pallas_call(
        _mlstm_kernel,
        grid=(bsz, seq // L),
        in_specs=[blk(w), blk(w), blk(w), blk(w), blk(LANES),
                  pl.BlockSpec((1, SUBLANES, L), lambda b, c: (b, 0, c)),
                  fixed((CONV_WIDTH, 2 * w)), fixed((1, 2 * w)), fixed((1, w))],
        out_specs=blk(w),
        out_shape=jax.ShapeDtypeStruct((bsz, seq, w), BF16),
        scratch_shapes=[pltpu.VMEM((MLSTM_HEADS, MLSTM_HEAD_DIM, MLSTM_HEAD_DIM), F32),
                        pltpu.VMEM((SUBLANES, MLSTM_HEAD_DIM), F32),
                        pltpu.VMEM((SUBLANES, LANES), F32),
                        pltpu.VMEM((SUBLANES, 2 * w), F32)],
        compiler_params=_cparams("parallel", "arbitrary"),
        name="mlstm_mixer",
    )(mq, mk, mv, mo, g, g_row, conv_w, conv_b.reshape(1, 2 * w), norm_w.reshape(1, w))


def _outproj_router_kernel(att_ref, mem_ref, x_ref, wa_ref, wm_ref, nw_ref, rw_ref, rb_ref,
                           x1_ref, hn_ref, ti_ref, tg_ref, cnt_ref):
    y = (jnp.dot(att_ref[...], wa_ref[...], preferred_element_type=F32)
         + jnp.dot(mem_ref[...], wm_ref[...], preferred_element_type=F32))
    x1 = x_ref[...] + y
    x1_ref[...] = x1
    var = jnp.mean(x1 * x1, axis=-1, keepdims=True)
    hn = (x1 * lax.rsqrt(var + NORM_EPS)) * nw_ref[...]
    hn_ref[...] = hn
    hn_hi = hn.astype(BF16)
    hn_lo = (hn - hn_hi.astype(F32)).astype(BF16)
    rw = rw_ref[...]
    rw_hi = rw.astype(BF16)
    rw_lo = (rw - rw_hi.astype(F32)).astype(BF16)
    part = jnp.dot(hn_hi, jnp.concatenate([rw_hi, rw_lo], axis=1), preferred_element_type=F32)
    logits = (part[:, :LANES] + part[:, LANES:]
              + jnp.dot(hn_lo, rw_hi, preferred_element_type=F32) + rb_ref[...])
    lane_i = lax.broadcasted_iota(jnp.int32, logits.shape, 1)
    lane = lane_i.astype(F32)
    cur = jnp.where(lane_i < N_EXPERTS, logits, -jnp.inf)
    vals, idxs = [], []
    for _ in range(TOP_K):
        mx = jnp.max(cur, axis=1, keepdims=True)
        am = jnp.min(jnp.where(cur == mx, lane, float(LANES)), axis=1, keepdims=True)
        vals.append(mx)
        idxs.append(am)
        cur = jnp.where(lane == am, -jnp.inf, cur)
    exps = [jnp.exp(vk - vals[0]) for vk in vals]
    denom = exps[0]
    for e in exps[1:]:
        denom = denom + e
    @pl.when(pl.program_id(0) == 0)
    def _():
        cnt_ref[...] = jnp.zeros_like(cnt_ref)

    tm = logits.shape[0]
    picks = [lane == am for am in idxs]
    picked = jnp.zeros(logits.shape, F32)
    for pk in picks:
        picked = picked + jnp.where(pk, 1.0, 0.0)
    r_i = lax.broadcasted_iota(jnp.int32, (tm, tm), 0)
    c_i = lax.broadcasted_iota(jnp.int32, (tm, tm), 1)
    before = jnp.dot(jnp.where(c_i < r_i, 1.0, 0.0).astype(BF16), picked.astype(BF16), preferred_element_type=F32)
    base = before + cnt_ref[0:1, :]
    cnt_ref[...] = cnt_ref[...] + jnp.sum(picked, axis=0, keepdims=True)

    ti = jnp.zeros(logits.shape, F32)
    tg = jnp.zeros(logits.shape, F32)
    for kk in range(TOP_K):
        rank = jnp.sum(jnp.where(picks[kk], base, 0.0), axis=1, keepdims=True)
        ti = jnp.where(lane_i == kk, idxs[kk], ti)
        ti = jnp.where(lane_i == TOP_K + kk, rank, ti)
        tg = jnp.where(lane_i == kk, exps[kk] / denom, tg)
    ti_ref[...] = ti.T[0:2 * TOP_K, :].astype(jnp.int32)
    tg_ref[...] = tg


def _outproj_router(att, mem, x2, w_att, w_mem, ffn_norm_w, router_w, router_b, tm=512):
    t, d = x2.shape
    row = lambda i: (i, 0)
    fixed = lambda i: (0, 0)
    rw = jnp.pad(router_w, ((0, 0), (0, LANES - N_EXPERTS)))
    rb = jnp.pad(router_b, (0, LANES - N_EXPERTS)).reshape(1, LANES)
    return pl.pallas_call(
        _outproj_router_kernel,
        grid=(t // tm,),
        in_specs=[pl.BlockSpec((tm, ATT_WIDTH), row), pl.BlockSpec((tm, MLSTM_WIDTH), row),
                  pl.BlockSpec((tm, d), row), pl.BlockSpec((ATT_WIDTH, d), fixed),
                  pl.BlockSpec((MLSTM_WIDTH, d), fixed), pl.BlockSpec((1, d), fixed),
                  pl.BlockSpec((d, LANES), fixed), pl.BlockSpec((1, LANES), fixed)],
        out_specs=[pl.BlockSpec((tm, d), row), pl.BlockSpec((tm, d), row),
                   pl.BlockSpec((2 * TOP_K, tm), lambda i: (0, i)), pl.BlockSpec((tm, LANES), row),
                   pl.BlockSpec((SUBLANES, LANES), fixed)],
        out_shape=[jax.ShapeDtypeStruct((t, d), F32), jax.ShapeDtypeStruct((t, d), F32),
                   jax.ShapeDtypeStruct((2 * TOP_K, t), jnp.int32), jax.ShapeDtypeStruct((t, LANES), F32),
                   jax.ShapeDtypeStruct((SUBLANES, LANES), F32)],
        compiler_params=_cparams("arbitrary"),
        name="outproj_router",
    )(att, mem, x2, w_att, w_mem, ffn_norm_w.reshape(1, d), rw, rb)


DISPATCH_ROWS = ROUTE_ROWS
ISSUE_ROWS = 16
DMA_PRIORITIES = 2


def _dispatch_kernel(fill_ref, nused_ref, dest_ref, hn_ref, x_hbm, stage, zero_buf, sem, zsem, csem, bsem):
    n_blocks = x_hbm.shape[0] // EXPERT_ROWS
    step = pl.program_id(0)
    slot = step % 2
    stage[slot] = hn_ref[...].reshape(stage.shape[1:])

    def token_row(g, r, kk):
        return pltpu.make_async_copy(stage.at[slot, g, pl.ds(r, 1), :],
                                     x_hbm.at[pl.ds(dest_ref[0, 0, g * ISSUE_ROWS + (kk * DISPATCH_ROWS + r)], 1), :],
                                     sem.at[slot])

    def wait_tokens(which):
        all_rows = x_hbm.at[pl.ds(0, TOP_K * DISPATCH_ROWS), :]
        pltpu.make_async_copy(all_rows, all_rows, sem.at[which]).wait()

    def pad_row(r):
        return pltpu.make_async_copy(zero_buf.at[pl.ds(0, 1), :], x_hbm.at[pl.ds(r, 1), :], zsem)

    def pad_chunk(start, size):
        return pltpu.make_async_copy(zero_buf.at[pl.ds(0, size), :],
                                     x_hbm.at[pl.ds(pl.multiple_of(start, SUBLANES), size), :], csem)

    def pad_block(b):
        start = pl.multiple_of(b * EXPERT_ROWS, EXPERT_ROWS)
        return pltpu.make_async_copy(zero_buf, x_hbm.at[pl.ds(start, EXPERT_ROWS), :], bsem)

    def start_tokens(g, c):
        for r in range(ISSUE_ROWS):
            for kk in range(TOP_K):
                token_row(g, r, kk).start(priority=(r * TOP_K + kk) % DMA_PRIORITIES)
        return c

    chunk_sizes = [s for s in (256, 128, 64, 32, 16, 8) if s < EXPERT_ROWS]

    def for_padding(fn):
        def per_expert(e, c):
            first = fill_ref[0, e]
            end = first + fill_ref[1, e]
            aligned = jnp.minimum(((first + SUBLANES - 1) // SUBLANES) * SUBLANES, end)
            lax.fori_loop(first, aligned, lambda r, cc: (fn(pad_row(r)), cc)[1], 0)
            left = end - aligned
            cur = aligned
            for size in chunk_sizes:
                take = (left & size) != 0
                pl.when(take)(functools.partial(lambda cur, size: fn(pad_chunk(cur, size)), cur, size))
                cur = cur + jnp.where(take, size, 0)
            return c
        lax.fori_loop(0, N_EXPERTS, per_expert, 0)
        lax.fori_loop(nused_ref[0], n_blocks, lambda b, cc: (fn(pad_block(b)), cc)[1], 0)

    lax.fori_loop(0, DISPATCH_ROWS // ISSUE_ROWS, start_tokens, 0)

    @pl.when(pl.program_id(0) == 0)
    def _():
        zero_buf[...] = jnp.zeros_like(zero_buf)
        for_padding(lambda cp: cp.start())
        for_padding(lambda cp: cp.wait())

    @pl.when(step > 0)
    def _():
        wait_tokens(1 - slot)

    @pl.when(step == pl.num_programs(0) - 1)
    def _():
        wait_tokens(slot)


def _dispatch_rows(hn, dest, fill, n_blocks_used, n_rows):
    t, d = hn.shape
    tm = DISPATCH_ROWS
    nb = t // tm
    return pl.pallas_call(
        _dispatch_kernel,
        grid_spec=pltpu.PrefetchScalarGridSpec(
            num_scalar_prefetch=2,
            grid=(nb,),
            in_specs=[pl.BlockSpec((1, 1, TOP_K * tm), lambda i, f, n: (i, 0, 0), memory_space=pltpu.SMEM),
                      pl.BlockSpec((tm, d), lambda i, f, n: (i, 0))],
            out_specs=pl.BlockSpec(memory_space=pl.ANY),
            scratch_shapes=[pltpu.VMEM((2, tm // ISSUE_ROWS, ISSUE_ROWS, d), hn.dtype),
                            pltpu.VMEM((EXPERT_ROWS, d), hn.dtype),
                            pltpu.SemaphoreType.DMA((2,)), pltpu.SemaphoreType.DMA(()),
                            pltpu.SemaphoreType.DMA(()), pltpu.SemaphoreType.DMA(())],
        ),
        out_shape=jax.ShapeDtypeStruct((n_rows, d), hn.dtype),
        compiler_params=_cparams("arbitrary"),
        name="moe_dispatch",
    )(fill, n_blocks_used, dest, hn)


def _expert_kernel(be_ref, nu_ref, nxt_ref, par_ref, x_ref, bgu_ref, bdn_ref, wgu_hbm, wdn_hbm, o_ref,
                   wgu_f, wdn_f, wgu_s, wdn_s, sem):
    b = pl.program_id(0)
    e = be_ref[b]
    e_prev = be_ref[jnp.maximum(b - 1, 0)]
    slot = par_ref[b]
    used = b < nu_ref[0]
    d_e = wdn_s.shape[0]

    def fetch(expert, into):
        return (pltpu.make_async_copy(wgu_hbm.at[expert], wgu_f.at[into], sem.at[0, into]),
                pltpu.make_async_copy(wdn_hbm.at[expert], wdn_f.at[into], sem.at[1, into]))

    @pl.when(b == 0)
    def _():
        for cp in fetch(e, slot):
            cp.start()

    @pl.when(used & ((b == 0) | (e != e_prev)))
    def _():
        for cp in fetch(e, slot):
            cp.wait()

        @pl.when(nxt_ref[b] >= 0)
        def _():
            for cp in fetch(nxt_ref[b], 1 - slot):
                cp.start()

        wgu_s[...] = wgu_f[slot].astype(BF16)
        wdn_s[...] = wdn_f[slot].astype(BF16)

    @pl.when(used)
    def _():
        xb = x_ref[...].astype(BF16)
        bgu = bgu_ref[0]
        glu = jnp.dot(xb, wgu_s[:, :d_e], preferred_element_type=F32) + bgu[:, :d_e]
        lin = jnp.dot(xb, wgu_s[:, d_e:], preferred_element_type=F32) + bgu[:, d_e:]
        glu = jnp.minimum(glu, SWIGLU_LIMIT)
        lin = jnp.clip(lin, -SWIGLU_LIMIT, SWIGLU_LIMIT)
        act = glu * (1.0 / (1.0 + jnp.exp(-SWIGLU_ALPHA * glu))) * (lin + 1.0)
        o_ref[...] = jnp.dot(act.astype(BF16), wdn_s[...], preferred_element_type=F32) + bdn_ref[0]

    @pl.when(jnp.logical_not(used))
    def _():
        o_ref[...] = jnp.zeros_like(o_ref)


def _expert_ffn(x_buf, blk_expert, n_blocks_used, next_expert, run_parity, w_gu, b_gu, w_dn, b_dn):
    n_rows, d = x_buf.shape
    ne, _, d2 = w_gu.shape
    d_e = w_dn.shape[1]
    nb = n_rows // EXPERT_ROWS
    last_used = lambda b, be, nu, nx, pr: (jnp.minimum(b, jnp.maximum(nu[0] - 1, 0)), 0)
    by_expert = lambda b, be, nu, nx, pr: (be[b], 0, 0)
    return pl.pallas_call(
        _expert_kernel,
        grid_spec=pltpu.PrefetchScalarGridSpec(
            num_scalar_prefetch=4,
            grid=(nb,),
            in_specs=[pl.BlockSpec((EXPERT_ROWS, d), last_used),
                      pl.BlockSpec((1, 1, d2), by_expert),
                      pl.BlockSpec((1, 1, d), by_expert),
                      pl.BlockSpec(memory_space=pl.ANY),
                      pl.BlockSpec(memory_space=pl.ANY)],
            out_specs=pl.BlockSpec((EXPERT_ROWS, d), lambda b, be, nu, nx, pr: (b, 0)),
            scratch_shapes=[pltpu.VMEM((2, d, d2), w_gu.dtype), pltpu.VMEM((2, d_e, d), w_dn.dtype),
                            pltpu.VMEM((d, d2), BF16), pltpu.VMEM((d_e, d), BF16),
                            pltpu.SemaphoreType.DMA((2, 2))],
        ),
        out_shape=jax.ShapeDtypeStruct((n_rows, d), F32),
        compiler_params=_cparams("arbitrary"),
        name="moe_experts",
    )(blk_expert, n_blocks_used, next_expert, run_parity, x_buf, b_gu.reshape(ne, 1, d2), b_dn.reshape(ne, 1, d),
      w_gu, w_dn)


COMBINE_ROWS = ROUTE_ROWS


def _combine_kernel(pos_ref, pos_next_ref, g_ref, x1_ref, nw_ref, y_hbm, o_ref, buf, sem, *, apply_norm):
    tm = COMBINE_ROWS
    step = pl.program_id(0)
    slot = step % 2

    def gather(p_ref, into):
        def issue(g, c):
            base = g * ISSUE_ROWS
            for r in range(ISSUE_ROWS):
                for kk in range(TOP_K):
                    pltpu.make_async_copy(y_hbm.at[pl.ds(p_ref[0, 0, base + (kk * tm + r)], 1), :],
                                          buf.at[into, g, pl.ds(kk * ISSUE_ROWS + r, 1), :],
                                          sem.at[into]).start(priority=(r * TOP_K + kk) % DMA_PRIORITIES)
            return c
        lax.fori_loop(0, tm // ISSUE_ROWS, issue, 0)

    @pl.when(step == 0)
    def _():
        gather(pos_ref, 0)

    @pl.when(step + 1 < pl.num_programs(0))
    def _():
        gather(pos_next_ref, 1 - slot)

    pltpu.make_async_copy(buf.at[slot], buf.at[slot], sem.at[slot]).wait()

    g = g_ref[...]
    acc = x1_ref[...]
    for kk in range(TOP_K):
        rows = buf[slot, :, pl.ds(kk * ISSUE_ROWS, ISSUE_ROWS), :].reshape(tm, acc.shape[1])
        acc = acc + rows * _lane_col(g, kk)
    if apply_norm:
        var = jnp.mean(acc * acc, axis=-1, keepdims=True)
        acc = (acc * lax.rsqrt(var + NORM_EPS)) * nw_ref[...]
    o_ref[...] = acc


def _combine(y_buf, pos, gates, x1, final_norm_w, apply_norm):
    t, d = x1.shape
    tm = COMBINE_ROWS
    nb = t // tm
    row = lambda i: (i, 0)
    pos3 = pos
    return pl.pallas_call(
        functools.partial(_combine_kernel, apply_norm=apply_norm),
        grid=(nb,),
        in_specs=[pl.BlockSpec((1, 1, TOP_K * tm), lambda i: (i, 0, 0), memory_space=pltpu.SMEM),
                  pl.BlockSpec((1, 1, TOP_K * tm), lambda i: (jnp.minimum(i + 1, nb - 1), 0, 0),
                               memory_space=pltpu.SMEM),
                  pl.BlockSpec((tm, LANES), row), pl.BlockSpec((tm, d), row),
                  pl.BlockSpec((1, d), lambda i: (0, 0)), pl.BlockSpec(memory_space=pl.ANY)],
        out_specs=pl.BlockSpec((tm, d), row),
        out_shape=jax.ShapeDtypeStruct((t, d), F32),
        scratch_shapes=[pltpu.VMEM((2, tm // ISSUE_ROWS, TOP_K * ISSUE_ROWS, d), F32),
                        pltpu.SemaphoreType.DMA((2,))],
        compiler_params=_cparams("arbitrary"),
        name="moe_combine",
    )(pos3, pos3, gates, x1, final_norm_w.reshape(1, d), y_buf)


def _routing_tables(top_idx, rank, counts, n_tokens):
    padded = (counts + EXPERT_ROWS - 1) // EXPERT_ROWS * EXPERT_ROWS
    pad_end = jnp.cumsum(padded)
    pad_start = pad_end - padded
    onehot = (top_idx[:, :, None] == jnp.arange(N_EXPERTS, dtype=jnp.int32)[None, None, :]).astype(F32)
    start_of = jnp.einsum("kte,e->kt", onehot, pad_start.astype(F32), precision=lax.Precision.HIGHEST)
    dest = (start_of.astype(jnp.int32) + rank).astype(jnp.int32)
    n_tiles = n_tokens // ROUTE_ROWS
    dest = dest.reshape(TOP_K, n_tiles, ROUTE_ROWS).transpose(1, 0, 2).reshape(n_tiles, 1, TOP_K * ROUTE_ROWS)
    n_rows = n_tokens * TOP_K + N_EXPERTS * EXPERT_ROWS
    n_blocks = n_rows // EXPERT_ROWS
    blk_start = jnp.arange(n_blocks, dtype=jnp.int32) * EXPERT_ROWS
    blk_expert = jnp.minimum(jnp.sum((pad_end[None, :] <= blk_start[:, None]).astype(jnp.int32), axis=1),
                             N_EXPERTS - 1).astype(jnp.int32)
    n_blocks_used = (pad_end[-1:] // EXPERT_ROWS).astype(jnp.int32)
    fill = jnp.stack([pad_start + counts, padded - counts]).astype(jnp.int32)
    experts = jnp.arange(N_EXPERTS, dtype=jnp.int32)[None, :]
    has_rows = (counts > 0)[None, :]
    earlier = (experts < blk_expert[:, None]) & has_rows
    run_parity = (jnp.sum(earlier.astype(jnp.int32), axis=1) % 2).astype(jnp.int32)
    later = jnp.min(jnp.where((experts > blk_expert[:, None]) & has_rows, experts, N_EXPERTS), axis=1)
    next_expert = jnp.where(later < N_EXPERTS, later, -1).astype(jnp.int32)
    return dest, fill, blk_expert, n_blocks_used, next_expert, run_parity, n_rows


def kernel(x, mix_norm_w, w_in, conv_w, conv_b, igate_b, fgate_b, mlstm_norm_w, w_out, ffn_norm_w,
           router_w, router_b, w_gate_up, b_gate_up, w_down, b_down, final_norm_w):
    bsz, seq, d = x.shape
    depth = w_in.shape[0]
    t = bsz * seq
    k_sel = min(TOPK_MAX, seq // 4)
    head_order = [h for s in range(Q_PER_KV) for h in (s, Q_PER_KV + s)]
    x2 = x.reshape(t, d)
    for layer in range(depth):
        p = _in_projection(x2, mix_norm_w[layer], w_in[layer], igate_b[layer], fgate_b[layer], seq)
        r3 = lambda a: a.reshape(bsz, seq, a.shape[-1])
        att = _dsa_attention(r3(p["q"]), r3(p["iq"]), r3(p["iw"]), r3(p["k"]), r3(p["v"]), r3(p["ik"]), k_sel)
        mem = _mlstm_mixer(r3(p["mq"]), r3(p["mk"]), r3(p["mv"]), r3(p["mo"]), r3(p["g"]),
                           conv_w[layer], conv_b[layer], mlstm_norm_w[layer])
        wo = w_out[layer]
        w_att = wo[:ATT_WIDTH].reshape(ATT_HEADS, HEAD_DIM, d)[jnp.asarray(head_order)].reshape(ATT_WIDTH, d)
        x1, hn, top_i, top_g, counts = _outproj_router(
            att.reshape(t, ATT_WIDTH), mem.reshape(t, MLSTM_WIDTH), x2, w_att.astype(BF16),
            wo[ATT_WIDTH:].astype(BF16), ffn_norm_w[layer], router_w[layer], router_b[layer])
        dest, fill, blk_expert, n_blocks_used, next_expert, run_parity, n_rows = _routing_tables(
            top_i[:TOP_K], top_i[TOP_K:], counts[0, :N_EXPERTS].astype(jnp.int32), t)
        x_buf = _dispatch_rows(hn, dest, fill, n_blocks_used, n_rows)
        y_buf = _expert_ffn(x_buf, blk_expert, n_blocks_used, next_expert, run_parity, w_gate_up[layer],
                            b_gate_up[layer], w_down[layer], b_down[layer])
        x2 = _combine(y_buf, dest, top_g, x1, final_norm_w, apply_norm=(layer == depth - 1))
    return x2.reshape(bsz, seq, d)
```

```python
import functools

import jax
import jax.numpy as jnp
import numpy as np
from jax import lax
from jax.experimental import pallas as pl
from jax.experimental.pallas import tpu as pltpu

F32 = jnp.float32
BF16 = jnp.bfloat16

LANES = 128
SUBLANES = 8
VMEM_LIMIT_BYTES = 56 * 1024 * 1024

NORM_EPS = 1e-6
ROPE_THETA = 500000.0
CHUNK = 64

HEAD_DIM = 64
ATT_HEADS = 8
ATT_KV_HEADS = 2
Q_PER_KV = ATT_HEADS // ATT_KV_HEADS
ATT_ROT_DIM = HEAD_DIM // 4
IDX_HEADS = 8
IDX_DIM = 32
IDX_ROT_DIM = IDX_DIM // 4
TOPK_MAX = 256
ATT_WIDTH = ATT_HEADS * HEAD_DIM

MLSTM_HEAD_DIM = 128
MLSTM_HEADS = 4
MLSTM_WIDTH = MLSTM_HEADS * MLSTM_HEAD_DIM
CONV_WIDTH = 4

N_EXPERTS = 32
TOP_K = 4
SWIGLU_ALPHA = 1.702
SWIGLU_LIMIT = 7.0

Q_BLOCK = 128
KEY_TILE = 256
BISECT_ITERS = 32
BISECT_GROUP = 4
FOLD_ROWS = 32
TILES_PER_STEP = 4
LEFTOVER_GROUPS = (2, 1)
V_ROWS = LANES + 16
MLSTM_CHUNK = 256
EXPERT_ROWS = 512
ROUTE_ROWS = 256
NEG_BIG = -1e30


def _cparams(*sem):
    return pltpu.CompilerParams(dimension_semantics=sem, vmem_limit_bytes=VMEM_LIMIT_BYTES)


_SEG_WIDTHS = (("q", 512), ("k", 128), ("v", 128), ("iq", 256), ("ik", 128), ("iw", 128),
               ("mq", 512), ("mk", 512), ("mv", 512), ("mo", 512), ("g", 128))
_SEG = {}
_off = 0
for _name, _w in _SEG_WIDTHS:
    _SEG[_name] = (_off, _w)
    _off += _w
PACKED_WIDTH = _off


def _lane_col(x, idx):
    lane = lax.broadcasted_iota(jnp.int32, x.shape, 1)
    return jnp.sum(jnp.where(lane == idx, x, 0.0), axis=1, keepdims=True)


def _inproj_kernel(x_ref, nw_ref, w_ref, gb_ref, ca_ref, sa1_ref, sa2_ref, ci_ref, si1_ref, si2_ref,
                   q_ref, k_ref, v_ref, iq_ref, ik_ref, iw_ref, mq_ref, mk_ref, mv_ref, mo_ref, g_ref,
                   *, idx_w_scale):
    x = x_ref[...]
    var = jnp.mean(x * x, axis=-1, keepdims=True)
    h = (x * lax.rsqrt(var + NORM_EPS)) * nw_ref[...]
    hb = h.astype(BF16)

    def proj(name):
        lo, width = _SEG[name]
        return jnp.dot(hb, w_ref[:, lo:lo + width], preferred_element_type=F32)

    def rope(p, c_ref, s1_ref, s2_ref, half):
        c, s1, s2 = c_ref[...], s1_ref[...], s2_ref[...]
        outs = []
        for j in range(p.shape[1] // LANES):
            xs = p[:, j * LANES:(j + 1) * LANES]
            outs.append(xs * c + pltpu.roll(xs, LANES - half, 1) * s1 + pltpu.roll(xs, half, 1) * s2)
        return outs[0] if len(outs) == 1 else jnp.concatenate(outs, axis=1)

    att_scale = float(HEAD_DIM ** -0.5 * np.log2(np.e))
    q_ref[...] = (rope(proj("q"), ca_ref, sa1_ref, sa2_ref, ATT_ROT_DIM // 2) * att_scale).astype(BF16)
    k_ref[...] = rope(proj("k"), ca_ref, sa1_ref, sa2_ref, ATT_ROT_DIM // 2).astype(BF16)
    v_ref[...] = proj("v").astype(BF16)
    iq_ref[...] = rope(proj("iq"), ci_ref, si1_ref, si2_ref, IDX_ROT_DIM // 2).astype(BF16)
    ik_ref[...] = rope(proj("ik"), ci_ref, si1_ref, si2_ref, IDX_ROT_DIM // 2).astype(BF16)
    iw_ref[...] = proj("iw") * idx_w_scale
    mq_ref[...] = proj("mq")
    mk_ref[...] = proj("mk")
    mv_ref[...] = proj("mv").astype(BF16)
    mo_ref[...] = proj("mo")
    g = proj("g") + gb_ref[...]
    lane = lax.broadcasted_iota(jnp.int32, g.shape, 1)
    log_f = jnp.minimum(g, 0.0) - jnp.log(1.0 + jnp.exp(-jnp.abs(g)))
    g_ref[...] = jnp.where(lane < MLSTM_HEADS, g, log_f)


def _rope_tables(seq, rot_dim, head_dim):
    pos = jnp.arange(seq, dtype=F32)
    inv_freq = ROPE_THETA ** (-jnp.arange(0, rot_dim, 2, dtype=F32) / rot_dim)
    ang = pos[:, None] * inv_freq[None, :]
    cos, sin = jnp.cos(ang), jnp.sin(ang)
    half = rot_dim // 2
    lane = np.arange(LANES) % head_dim
    fidx = lane % half
    first = lane < half
    second = (lane >= half) & (lane < rot_dim)
    cos_l, sin_l = cos[:, fidx], sin[:, fidx]
    c = jnp.where(jnp.asarray(first | second)[None, :], cos_l, 1.0)
    s1 = jnp.where(jnp.asarray(first)[None, :], -sin_l, 0.0)
    s2 = jnp.where(jnp.asarray(second)[None, :], sin_l, 0.0)
    return c, s1, s2


def _pack_w_in(w_in):
    sizes = (ATT_WIDTH, ATT_KV_HEADS * HEAD_DIM, ATT_KV_HEADS * HEAD_DIM, IDX_HEADS * IDX_DIM, IDX_DIM,
             IDX_HEADS, MLSTM_WIDTH, MLSTM_WIDTH, MLSTM_WIDTH, MLSTM_HEADS, MLSTM_HEADS, MLSTM_WIDTH)
    offs = np.concatenate([[0], np.cumsum(sizes)])
    aq, ak, av, iq, ik, iw, mq, mk, mv, mi, mf, mo = (w_in[:, offs[i]:offs[i + 1]] for i in range(12))
    d = w_in.shape[0]
    head_order = [h for s in range(Q_PER_KV) for h in (s, Q_PER_KV + s)]
    aq = aq.reshape(d, ATT_HEADS, HEAD_DIM)[:, head_order, :].reshape(d, ATT_WIDTH)
    ik_rep = jnp.tile(ik, (1, LANES // IDX_DIM))
    iw_pad = jnp.pad(iw, ((0, 0), (0, LANES - IDX_HEADS)))
    g_pad = jnp.pad(jnp.concatenate([mi, mf], axis=1), ((0, 0), (0, LANES - 2 * MLSTM_HEADS)))
    packed = jnp.concatenate([aq, ak, av, iq, ik_rep, iw_pad, mq, mk, mv, mo, g_pad], axis=1)
    assert packed.shape[1] == PACKED_WIDTH
    return packed.astype(BF16)


def _in_projection(x2, norm_w, w_in, igate_b, fgate_b, seq, tm=1024):
    t, d = x2.shape
    assert t % tm == 0 and seq % tm == 0
    wp = _pack_w_in(w_in)
    gb = jnp.pad(jnp.concatenate([igate_b, fgate_b]), (0, LANES - 2 * MLSTM_HEADS)).reshape(1, LANES).astype(F32)
    ca, sa1, sa2 = _rope_tables(seq, ATT_ROT_DIM, HEAD_DIM)
    ci, si1, si2 = _rope_tables(seq, IDX_ROT_DIM, IDX_DIM)
    n_pos_blocks = seq // tm
    row = lambda i: (i, 0)
    fixed = lambda i: (0, 0)
    pos = lambda i: (i % n_pos_blocks, 0)
    tab = pl.BlockSpec((tm, LANES), pos)
    out_defs = (("q", 512, BF16), ("k", 128, BF16), ("v", 128, BF16), ("iq", 256, BF16), ("ik", 128, BF16),
                ("iw", 128, F32), ("mq", 512, F32), ("mk", 512, F32), ("mv", 512, BF16), ("mo", 512, F32),
                ("g", 128, F32))
    idx_w_scale = (IDX_HEADS ** -0.5) * (IDX_DIM ** -0.5)
    outs = pl.pallas_call(
        functools.partial(_inproj_kernel, idx_w_scale=idx_w_scale),
        grid=(t // tm,),
        in_specs=[pl.BlockSpec((tm, d), row), pl.BlockSpec((1, d), fixed),
                  pl.BlockSpec((d, PACKED_WIDTH), fixed), pl.BlockSpec((1, LANES), fixed),
                  tab, tab, tab, tab, tab, tab],
        out_specs=[pl.BlockSpec((tm, w), row) for _, w, _ in out_defs],
        out_shape=[jax.ShapeDtypeStruct((t, w), dt) for _, w, dt in out_defs],
        compiler_params=_cparams("parallel"),
        name="in_projection",
    )(x2, norm_w.reshape(1, d), wp, gb, ca, sa1, sa2, ci, si1, si2)
    return dict(zip([n for n, _, _ in out_defs], outs))


def _dsa_kernel(q_ref, iq_ref, iw_ref, k_ref, vt_ref, ik_ref, o_ref, score_ref, acc_ref, vk_ref, jcut_ref, p_ref,
                *, k_sel, index_iters):
    i = pl.program_id(1)
    n_keys = Q_BLOCK * (i + 1)
    n_tiles = (n_keys + KEY_TILE - 1) // KEY_TILE
    n_steps = n_tiles // TILES_PER_STEP

    def for_tiles(body, init, lead=None):
        def run(tiles, carry):
            opened = [lead(kt) if lead is not None else None for kt in tiles]
            for kt, head in zip(tiles, opened):
                carry = body(kt, carry) if lead is None else body(kt, head, carry)
            return carry

        carry = lax.fori_loop(0, n_steps, lambda j, c: run([j * TILES_PER_STEP + u for u in range(TILES_PER_STEP)], c),
                              init)
        done = n_steps * TILES_PER_STEP
        for size in LEFTOVER_GROUPS:
            take = (n_tiles - done) >= size
            carry = lax.cond(take, functools.partial(run, [done + u for u in range(size)]), lambda c: c, carry)
            done = done + jnp.where(take, size, 0)
        return carry
    n_stack = ATT_HEADS * Q_BLOCK

    lane = lax.broadcasted_iota(jnp.int32, (Q_BLOCK, LANES), 1)
    iq = iq_ref[0]
    per_slab = LANES // IDX_DIM
    qi_stack = jnp.concatenate(
        [jnp.where((lane // IDX_DIM) == (h % per_slab), iq[:, (h // per_slab) * LANES:(h // per_slab + 1) * LANES], 0)
         for h in range(IDX_HEADS)], axis=0)
    q = q_ref[0]
    q_stack = jnp.concatenate(
        [jnp.where((lane // HEAD_DIM) == g, q[:, s * LANES:(s + 1) * LANES], 0)
         for s in range(Q_PER_KV) for g in range(ATT_KV_HEADS)], axis=0)
    w_t = iw_ref[0].T

    q_lane = lax.broadcasted_iota(jnp.int32, (1, Q_BLOCK), 1)
    key_limit = (2 * i + 1 + (q_lane >= CHUNK).astype(jnp.int32)) * CHUNK
    k_eff = jnp.minimum(key_limit, k_sel).astype(F32)
    key_iota = lax.broadcasted_iota(jnp.int32, (KEY_TILE, Q_BLOCK), 0)
    nt_dims = (((1,), (1,)), ((), ()))

    def idx_lead(kt):
        start = pl.multiple_of(kt * KEY_TILE, KEY_TILE)
        return lax.dot_general(ik_ref[0, pl.ds(start, KEY_TILE), :], qi_stack, nt_dims,
                               preferred_element_type=F32)

    def idx_body(kt, z, carry):
        rmin, rmax = carry
        start = pl.multiple_of(kt * KEY_TILE, KEY_TILE)
        sc = jnp.zeros((KEY_TILE, Q_BLOCK), F32)
        for h in range(IDX_HEADS):
            sc = sc + jnp.maximum(z[:, h * Q_BLOCK:(h + 1) * Q_BLOCK], 0.0) * w_t[h:h + 1, :]
        adm = (key_iota + start) < key_limit
        score_ref[pl.ds(start, KEY_TILE), :] = jnp.where(adm, sc, -jnp.inf)
        rmin = jnp.minimum(rmin, jnp.min(jnp.where(adm, sc, jnp.inf), axis=0, keepdims=True))
        rmax = jnp.maximum(rmax, jnp.max(jnp.where(adm, sc, -jnp.inf), axis=0, keepdims=True))
        return rmin, rmax

    init = (jnp.full((1, Q_BLOCK), jnp.inf, F32), jnp.full((1, Q_BLOCK), -jnp.inf, F32))
    lo, hi = for_tiles(idx_body, init, lead=idx_lead)

    def fold(x):
        return x.reshape(KEY_TILE // FOLD_ROWS, FOLD_ROWS, Q_BLOCK)

    def count(pred):
        def body(kt, acc):
            start = pl.multiple_of(kt * KEY_TILE, KEY_TILE)
            hit = pred(score_ref[pl.ds(start, KEY_TILE), :], key_iota + start).astype(F32)
            return acc + jnp.sum(fold(hit), axis=0)
        acc = for_tiles(body, jnp.zeros((FOLD_ROWS, Q_BLOCK), F32))
        return jnp.sum(acc, axis=0, keepdims=True)

    def unresolved(c_lo):
        return jnp.max(c_lo - k_eff) > 0.0

    def smallest_at_least(lo):
        def min_body(kt, acc):
            start = pl.multiple_of(kt * KEY_TILE, KEY_TILE)
            x = score_ref[pl.ds(start, KEY_TILE), :]
            return jnp.minimum(acc, jnp.min(fold(jnp.where(x >= lo, x, jnp.inf)), axis=0))
        return jnp.min(for_tiles(min_body, jnp.full((FOLD_ROWS, Q_BLOCK), jnp.inf, F32)), axis=0, keepdims=True)

    def bisect(carry):
        it, lo, hi, c_lo = carry
        for _ in range(BISECT_GROUP):
            mid = 0.5 * (lo + hi)
            c_mid = count(lambda x, _: x >= mid)
            ok = c_mid >= k_eff
            lo, hi, c_lo = jnp.where(ok, mid, lo), jnp.where(ok, hi, mid), jnp.where(ok, c_mid, c_lo)
        return it + BISECT_GROUP, lo, hi, c_lo

    _, lo, _, c_lo = lax.while_loop(lambda c: (c[0] < BISECT_ITERS) & unresolved(c[3]), bisect,
                                    (jnp.int32(0), lo, hi, key_limit.astype(F32)))

    vk_ref[...] = jnp.broadcast_to(lo, vk_ref.shape)
    jcut_ref[...] = jnp.full(jcut_ref.shape, jnp.iinfo(jnp.int32).max, jnp.int32)

    @pl.when(unresolved(c_lo))
    def _():
        vk = smallest_at_least(lo)
        need = k_eff - count(lambda x, _: x > vk)

        def jbisect(_, carry):
            jlo, jhi = carry
            jmid = jnp.right_shift(jlo + jhi, 1)
            ok = count(lambda x, kidx: (x == vk) & (kidx < jmid)) >= need
            return jnp.where(ok, jlo, jmid), jnp.where(ok, jmid, jhi)

        j0 = (jnp.zeros((1, Q_BLOCK), jnp.int32), jnp.full((1, Q_BLOCK), n_keys, jnp.int32))
        _, jcut = lax.fori_loop(0, index_iters, jbisect, j0)
        vk_ref[...] = jnp.broadcast_to(vk, vk_ref.shape)
        jcut_ref[...] = jnp.broadcast_to(jcut, jcut_ref.shape)

    vk = vk_ref[0:1, :]
    jcut = jcut_ref[0:1, :]

    r_i = lax.broadcasted_iota(jnp.int32, (Q_BLOCK, LANES), 0)
    eye = jnp.where(r_i == lane, 1.0, 0.0).astype(BF16)
    q_aug = jnp.concatenate([q_stack, jnp.concatenate([eye] * ATT_HEADS, axis=0)], axis=1)
    acc_ref[...] = jnp.zeros_like(acc_ref)

    def att_lead(kt):
        start = pl.multiple_of(kt * KEY_TILE, KEY_TILE)
        x = score_ref[pl.ds(start, KEY_TILE), :]
        sel = (x > vk) | ((x == vk) & ((key_iota + start) < jcut))
        bias = jnp.where(sel, 0.0, NEG_BIG).astype(BF16)
        k_aug = jnp.concatenate([k_ref[0, pl.ds(start, KEY_TILE), :], bias], axis=1)
        return lax.dot_general(k_aug, q_aug, nt_dims, preferred_element_type=F32)

    def accumulate(kt, alpha):
        acc_ref[...] = alpha * acc_ref[...] + jnp.dot(vt_ref[0, kt], p_ref[...], preferred_element_type=F32)

    def att_body(kt, s, carry):
        m, alpha = carry
        accumulate(jnp.maximum(kt - 1, 0), alpha)
        m_new = jnp.maximum(m, jnp.max(s, axis=0, keepdims=True))
        p_ref[...] = jnp.exp2(s - m_new).astype(BF16)
        return m_new, jnp.exp2(m - m_new)

    p_ref[...] = jnp.zeros_like(p_ref)
    init = (jnp.full((1, n_stack), NEG_BIG, F32), jnp.ones((1, n_stack), F32))
    _, alpha = for_tiles(att_body, init, lead=att_lead)
    accumulate(n_tiles - 1, alpha)

    out_t = acc_ref[0:LANES, :] / acc_ref[LANES:LANES + 1, :]
    for s in range(Q_PER_KV):
        o0 = out_t[:, (2 * s) * Q_BLOCK:(2 * s + 1) * Q_BLOCK].T
        o1 = out_t[:, (2 * s + 1) * Q_BLOCK:(2 * s + 2) * Q_BLOCK].T
        o_ref[0, :, s * LANES:(s + 1) * LANES] = jnp.where(lane < HEAD_DIM, o0, o1).astype(o_ref.dtype)


def _dsa_attention(q, iq, iw, k, v, ik, k_sel):
    bsz, seq, _ = q.shape
    assert seq % KEY_TILE == 0 and seq % Q_BLOCK == 0
    n_kt = seq // KEY_TILE
    vt = v.reshape(bsz, n_kt, KEY_TILE, LANES).transpose(0, 1, 3, 2)
    ones_rows = jnp.zeros((bsz, n_kt, V_ROWS - LANES, KEY_TILE), v.dtype).at[:, :, 0, :].set(1.0)
    vt = jnp.concatenate([vt, ones_rows], axis=2)
    blk = lambda w: pl.BlockSpec((1, Q_BLOCK, w), lambda b, i: (b, i, 0))
    full = pl.BlockSpec((1, seq, LANES), lambda b, i: (b, 0, 0))
    return pl.pallas_call(
        functools.partial(_dsa_kernel, k_sel=k_sel, index_iters=seq.bit_length()),
        grid=(bsz, seq // Q_BLOCK),
        in_specs=[blk(ATT_WIDTH), blk(IDX_HEADS * IDX_DIM), blk(LANES), full,
                  pl.BlockSpec((1, n_kt, V_ROWS, KEY_TILE), lambda b, i: (b, 0, 0, 0)), full],
        out_specs=blk(ATT_WIDTH),
        out_shape=jax.ShapeDtypeStruct((bsz, seq, ATT_WIDTH), BF16),
        scratch_shapes=[pltpu.VMEM((seq, Q_BLOCK), F32), pltpu.VMEM((V_ROWS, ATT_HEADS * Q_BLOCK), F32),
                        pltpu.VMEM((SUBLANES, Q_BLOCK), F32), pltpu.VMEM((SUBLANES, Q_BLOCK), jnp.int32),
                        pltpu.VMEM((KEY_TILE, ATT_HEADS * Q_BLOCK), BF16)],
        compiler_params=_cparams("parallel", "parallel"),
        name="dsa_attention",
    )(q, iq, iw, k, vt, ik)


def _mlstm_kernel(mq_ref, mk_ref, mv_ref, mo_ref, gc_ref, gr_ref, cw_ref, cb_ref, nw_ref, o_ref,
                  c_ref, n_ref, m_ref, prev_ref):
    L = MLSTM_CHUNK
    W = MLSTM_WIDTH
    dh = MLSTM_HEAD_DIM

    @pl.when(pl.program_id(1) == 0)
    def _():
        c_ref[...] = jnp.zeros_like(c_ref)
        n_ref[...] = jnp.zeros_like(n_ref)
        m_ref[...] = jnp.zeros_like(m_ref)
        prev_ref[...] = jnp.zeros_like(prev_ref)

    row = lax.broadcasted_iota(jnp.int32, (SUBLANES, W), 0)

    def conv_silu(x, prev8, w4, b):
        y = x * w4[CONV_WIDTH - 1:CONV_WIDTH, :] + b
        for d in range(1, CONV_WIDTH):
            xr = pltpu.roll(x, d, 0)
            head = jnp.where(row < d, pltpu.roll(prev8, d, 0), xr[:SUBLANES])
            xs = jnp.concatenate([head, xr[SUBLANES:]], axis=0)
            y = y + xs * w4[CONV_WIDTH - 1 - d:CONV_WIDTH - d, :]
        return y / (1.0 + jnp.exp(-y))

    xq, xk = mq_ref[0], mk_ref[0]
    cw, cb = cw_ref[...], cb_ref[...]
    q = conv_silu(xq, prev_ref[:, :W], cw[:, :W], cb[:, :W])
    k = conv_silu(xk, prev_ref[:, W:], cw[:, W:], cb[:, W:]) * (dh ** -0.5)
    prev_ref[:, :W] = xq[L - SUBLANES:, :]
    prev_ref[:, W:] = xk[L - SUBLANES:, :]

    g_col = gc_ref[0]
    g_row = gr_ref[0]
    r_i = lax.broadcasted_iota(jnp.int32, (L, L), 0)
    c_i = lax.broadcasted_iota(jnp.int32, (L, L), 1)
    causal = r_i >= c_i
    tri = causal.astype(F32)
    b_col_all = jnp.dot(tri, g_col, preferred_element_type=F32, precision=lax.Precision.HIGHEST)
    b_row_all = jnp.dot(g_row, (r_i <= c_i).astype(F32), preferred_element_type=F32,
                        precision=lax.Precision.HIGHEST)

    v = mv_ref[0]
    mo = mo_ref[0]
    nw = nw_ref[...]
    for h in range(MLSTM_HEADS):
        hs = slice(h * dh, (h + 1) * dh)
        i_col = _lane_col(g_col, h)
        i_row = g_row[h:h + 1, :]
        b_col = _lane_col(b_col_all, MLSTM_HEADS + h)
        b_row = b_row_all[MLSTM_HEADS + h:MLSTM_HEADS + h + 1, :]
        b_last = jnp.sum(g_row[MLSTM_HEADS + h:MLSTM_HEADS + h + 1, :], axis=1, keepdims=True)
        m_prev = m_ref[h:h + 1, 0:1]
        c_mat = c_ref[h]
        n_vec = n_ref[h:h + 1, :]

        d_log = jnp.where(causal, b_col + (i_row - b_row), -jnp.inf)
        inter_log = b_col + m_prev
        m_t = jnp.maximum(inter_log, jnp.max(d_log, axis=1, keepdims=True))
        d_w = jnp.exp(d_log - m_t)
        inter_w = jnp.exp(inter_log - m_t)
        qh, kh, vh = q[:, hs], k[:, hs], v[:, hs]
        qb, kb = qh.astype(BF16), kh.astype(BF16)
        s = lax.dot_general(qb, kb, (((1,), (1,)), ((), ())), preferred_element_type=F32) * d_w
        num = (jnp.dot(s.astype(BF16), vh, preferred_element_type=F32)
               + inter_w * jnp.dot(qb, c_mat.astype(BF16), preferred_element_type=F32))
        den = jnp.sum(s, axis=1, keepdims=True) + inter_w * jnp.sum(qh * n_vec, axis=1, keepdims=True)
        hout = num / jnp.maximum(jnp.abs(den), jnp.exp(-m_t))

        w_log = b_last - b_col + i_col
        m_new = jnp.maximum(b_last + m_prev, jnp.max(w_log, axis=0, keepdims=True))
        decay = jnp.exp(b_last + m_prev - m_new)
        kw = kh * jnp.exp(w_log - m_new)
        c_ref[h] = decay * c_mat + jnp.dot(kw.T.astype(BF16), vh, preferred_element_type=F32)
        n_ref[h:h + 1, :] = decay * n_vec + jnp.sum(kw, axis=0, keepdims=True)
        m_ref[h:h + 1, :] = jnp.broadcast_to(m_new, (1, LANES))

        hn = hout * lax.rsqrt(jnp.mean(hout * hout, axis=-1, keepdims=True) + NORM_EPS)
        gate = 1.0 / (1.0 + jnp.exp(-mo[:, hs]))
        o_ref[0, :, hs] = (hn * nw[:, hs] * gate).astype(o_ref.dtype)


def _mlstm_mixer(mq, mk, mv, mo, g, conv_w, conv_b, norm_w):
    bsz, seq, w = mq.shape
    L = MLSTM_CHUNK
    assert seq % L == 0
    g_row = g[:, :, :SUBLANES].transpose(0, 2, 1)
    blk = lambda width: pl.BlockSpec((1, L, width), lambda b, c: (b, c, 0))
    fixed = lambda shape: pl.BlockSpec(shape, lambda b, c: (0, 0))
    return pl.pallas_call(
        _mlstm_kernel,
        grid=(bsz, seq // L),
        in_specs=[blk(w), blk(w), blk(w), blk(w), blk(LANES),
                  pl.BlockSpec((1, SUBLANES, L), lambda b, c: (b, 0, c)),
                  fixed((CONV_WIDTH, 2 * w)), fixed((1, 2 * w)), fixed((1, w))],
        out_specs=blk(w),
        out_shape=jax.ShapeDtypeStruct((bsz, seq, w), BF16),
        scratch_shapes=[pltpu.VMEM((MLSTM_HEADS, MLSTM_HEAD_DIM, MLSTM_HEAD_DIM), F32),
                        pltpu.VMEM((SUBLANES, MLSTM_HEAD_DIM), F32),
                        pltpu.VMEM((SUBLANES, LANES), F32),
                        pltpu.VMEM((SUBLANES, 2 * w), F32)],
        compiler_params=_cparams("parallel", "arbitrary"),
        name="mlstm_mixer",
    )(mq, mk, mv, mo, g, g_row, conv_w, conv_b.reshape(1, 2 * w), norm_w.reshape(1, w))


def _outproj_router_kernel(att_ref, mem_ref, x_ref, wa_ref, wm_ref, nw_ref, rw_ref, rb_ref,
                           x1_ref, hn_ref, ti_ref, tg_ref, cnt_ref):
    y = (jnp.dot(att_ref[...], wa_ref[...], preferred_element_type=F32)
         + jnp.dot(mem_ref[...], wm_ref[...], preferred_element_type=F32))
    x1 = x_ref[...] + y
    x1_ref[...] = x1
    var = jnp.mean(x1 * x1, axis=-1, keepdims=True)
    hn = (x1 * lax.rsqrt(var + NORM_EPS)) * nw_ref[...]
    hn_ref[...] = hn
    hn_hi = hn.astype(BF16)
    hn_lo = (hn - hn_hi.astype(F32)).astype(BF16)
    rw = rw_ref[...]
    rw_hi = rw.astype(BF16)
    rw_lo = (rw - rw_hi.astype(F32)).astype(BF16)
    part = jnp.dot(hn_hi, jnp.concatenate([rw_hi, rw_lo], axis=1), preferred_element_type=F32)
    logits = (part[:, :LANES] + part[:, LANES:]
              + jnp.dot(hn_lo, rw_hi, preferred_element_type=F32) + rb_ref[...])
    lane_i = lax.broadcasted_iota(jnp.int32, logits.shape, 1)
    lane = lane_i.astype(F32)
    cur = jnp.where(lane_i < N_EXPERTS, logits, -jnp.inf)
    vals, idxs = [], []
    for _ in range(TOP_K):
        mx = jnp.max(cur, axis=1, keepdims=True)
        am = jnp.min(jnp.where(cur == mx, lane, float(LANES)), axis=1, keepdims=True)
        vals.append(mx)
        idxs.append(am)
        cur = jnp.where(lane == am, -jnp.inf, cur)
    exps = [jnp.exp(vk - vals[0]) for vk in vals]
    denom = exps[0]
    for e in exps[1:]:
        denom = denom + e
    @pl.when(pl.program_id(0) == 0)
    def _():
        cnt_ref[...] = jnp.zeros_like(cnt_ref)

    tm = logits.shape[0]
    picks = [lane == am for am in idxs]
    picked = jnp.zeros(logits.shape, F32)
    for pk in picks:
        picked = picked + jnp.where(pk, 1.0, 0.0)
    r_i = lax.broadcasted_iota(jnp.int32, (tm, tm), 0)
    c_i = lax.broadcasted_iota(jnp.int32, (tm, tm), 1)
    before = jnp.dot(jnp.where(c_i < r_i, 1.0, 0.0).astype(BF16), picked.astype(BF16), preferred_element_type=F32)
    base = before + cnt_ref[0:1, :]
    cnt_ref[...] = cnt_ref[...] + jnp.sum(picked, axis=0, keepdims=True)

    ti = jnp.zeros(logits.shape, F32)
    tg = jnp.zeros(logits.shape, F32)
    for kk in range(TOP_K):
        rank = jnp.sum(jnp.where(picks[kk], base, 0.0), axis=1, keepdims=True)
        ti = jnp.where(lane_i == kk, idxs[kk], ti)
        ti = jnp.where(lane_i == TOP_K + kk, rank, ti)
        tg = jnp.where(lane_i == kk, exps[kk] / denom, tg)
    ti_ref[...] = ti.T[0:2 * TOP_K, :].astype(jnp.int32)
    tg_ref[...] = tg


def _outproj_router(att, mem, x2, w_att, w_mem, ffn_norm_w, router_w, router_b, tm=512):
    t, d = x2.shape
    row = lambda i: (i, 0)
    fixed = lambda i: (0, 0)
    rw = jnp.pad(router_w, ((0, 0), (0, LANES - N_EXPERTS)))
    rb = jnp.pad(router_b, (0, LANES - N_EXPERTS)).reshape(1, LANES)
    return pl.pallas_call(
        _outproj_router_kernel,
        grid=(t // tm,),
        in_specs=[pl.BlockSpec((tm, ATT_WIDTH), row), pl.BlockSpec((tm, MLSTM_WIDTH), row),
                  pl.BlockSpec((tm, d), row), pl.BlockSpec((ATT_WIDTH, d), fixed),
                  pl.BlockSpec((MLSTM_WIDTH, d), fixed), pl.BlockSpec((1, d), fixed),
                  pl.BlockSpec((d, LANES), fixed), pl.BlockSpec((1, LANES), fixed)],
        out_specs=[pl.BlockSpec((tm, d), row), pl.BlockSpec((tm, d), row),
                   pl.BlockSpec((2 * TOP_K, tm), lambda i: (0, i)), pl.BlockSpec((tm, LANES), row),
                   pl.BlockSpec((SUBLANES, LANES), fixed)],
        out_shape=[jax.ShapeDtypeStruct((t, d), F32), jax.ShapeDtypeStruct((t, d), F32),
                   jax.ShapeDtypeStruct((2 * TOP_K, t), jnp.int32), jax.ShapeDtypeStruct((t, LANES), F32),
                   jax.ShapeDtypeStruct((SUBLANES, LANES), F32)],
        compiler_params=_cparams("arbitrary"),
        name="outproj_router",
    )(att, mem, x2, w_att, w_mem, ffn_norm_w.reshape(1, d), rw, rb)


DISPATCH_ROWS = ROUTE_ROWS
ISSUE_ROWS = 16
DMA_PRIORITIES = 2


def _dispatch_kernel(fill_ref, nused_ref, dest_ref, hn_ref, x_hbm, stage, zero_buf, sem, zsem, csem, bsem):
    n_blocks = x_hbm.shape[0] // EXPERT_ROWS
    step = pl.program_id(0)
    slot = step % 2
    stage[slot] = hn_ref[...].reshape(stage.shape[1:])

    def token_row(g, r, kk):
        return pltpu.make_async_copy(stage.at[slot, g, pl.ds(r, 1), :],
                                     x_hbm.at[pl.ds(dest_ref[0, 0, g * ISSUE_ROWS + (kk * DISPATCH_ROWS + r)], 1), :],
                                     sem.at[slot])

    def wait_tokens(which):
        all_rows = x_hbm.at[pl.ds(0, TOP_K * DISPATCH_ROWS), :]
        pltpu.make_async_copy(all_rows, all_rows, sem.at[which]).wait()

    def pad_row(r):
        return pltpu.make_async_copy(zero_buf.at[pl.ds(0, 1), :], x_hbm.at[pl.ds(r, 1), :], zsem)

    def pad_chunk(start, size):
        return pltpu.make_async_copy(zero_buf.at[pl.ds(0, size), :],
                                     x_hbm.at[pl.ds(pl.multiple_of(start, SUBLANES), size), :], csem)

    def pad_block(b):
        start = pl.multiple_of(b * EXPERT_ROWS, EXPERT_ROWS)
        return pltpu.make_async_copy(zero_buf, x_hbm.at[pl.ds(start, EXPERT_ROWS), :], bsem)

    def start_tokens(g, c):
        for r in range(ISSUE_ROWS):
            for kk in range(TOP_K):
                token_row(g, r, kk).start(priority=(r * TOP_K + kk) % DMA_PRIORITIES)
        return c

    chunk_sizes = [s for s in (256, 128, 64, 32, 16, 8) if s < EXPERT_ROWS]

    def for_padding(fn):
        def per_expert(e, c):
            first = fill_ref[0, e]
            end = first + fill_ref[1, e]
            aligned = jnp.minimum(((first + SUBLANES - 1) // SUBLANES) * SUBLANES, end)
            lax.fori_loop(first, aligned, lambda r, cc: (fn(pad_row(r)), cc)[1], 0)
            left = end - aligned
            cur = aligned
            for size in chunk_sizes:
                take = (left & size) != 0
                pl.when(take)(functools.partial(lambda cur, size: fn(pad_chunk(cur, size)), cur, size))
                cur = cur + jnp.where(take, size, 0)
            return c
        lax.fori_loop(0, N_EXPERTS, per_expert, 0)
        lax.fori_loop(nused_ref[0], n_blocks, lambda b, cc: (fn(pad_block(b)), cc)[1], 0)

    lax.fori_loop(0, DISPATCH_ROWS // ISSUE_ROWS, start_tokens, 0)

    @pl.when(pl.program_id(0) == 0)
    def _():
        zero_buf[...] = jnp.zeros_like(zero_buf)
        for_padding(lambda cp: cp.start())
        for_padding(lambda cp: cp.wait())

    @pl.when(step > 0)
    def _():
        wait_tokens(1 - slot)

    @pl.when(step == pl.num_programs(0) - 1)
    def _():
        wait_tokens(slot)


def _dispatch_rows(hn, dest, fill, n_blocks_used, n_rows):
    t, d = hn.shape
    tm = DISPATCH_ROWS
    nb = t // tm
    return pl.pallas_call(
        _dispatch_kernel,
        grid_spec=pltpu.PrefetchScalarGridSpec(
            num_scalar_prefetch=2,
            grid=(nb,),
            in_specs=[pl.BlockSpec((1, 1, TOP_K * tm), lambda i, f, n: (i, 0, 0), memory_space=pltpu.SMEM),
                      pl.BlockSpec((tm, d), lambda i, f, n: (i, 0))],
            out_specs=pl.BlockSpec(memory_space=pl.ANY),
            scratch_shapes=[pltpu.VMEM((2, tm // ISSUE_ROWS, ISSUE_ROWS, d), hn.dtype),
                            pltpu.VMEM((EXPERT_ROWS, d), hn.dtype),
                            pltpu.SemaphoreType.DMA((2,)), pltpu.SemaphoreType.DMA(()),
                            pltpu.SemaphoreType.DMA(()), pltpu.SemaphoreType.DMA(())],
        ),
        out_shape=jax.ShapeDtypeStruct((n_rows, d), hn.dtype),
        compiler_params=_cparams("arbitrary"),
        name="moe_dispatch",
    )(fill, n_blocks_used, dest, hn)


def _expert_kernel(be_ref, nu_ref, nxt_ref, par_ref, x_ref, bgu_ref, bdn_ref, wgu_hbm, wdn_hbm, o_ref,
                   wgu_f, wdn_f, wgu_s, wdn_s, sem):
    b = pl.program_id(0)
    e = be_ref[b]
    e_prev = be_ref[jnp.maximum(b - 1, 0)]
    slot = par_ref[b]
    used = b < nu_ref[0]
    d_e = wdn_s.shape[0]

    def fetch(expert, into):
        return (pltpu.make_async_copy(wgu_hbm.at[expert], wgu_f.at[into], sem.at[0, into]),
                pltpu.make_async_copy(wdn_hbm.at[expert], wdn_f.at[into], sem.at[1, into]))

    @pl.when(b == 0)
    def _():
        for cp in fetch(e, slot):
            cp.start()

    @pl.when(used & ((b == 0) | (e != e_prev)))
    def _():
        for cp in fetch(e, slot):
            cp.wait()

        @pl.when(nxt_ref[b] >= 0)
        def _():
            for cp in fetch(nxt_ref[b], 1 - slot):
                cp.start()

        wgu_s[...] = wgu_f[slot].astype(BF16)
        wdn_s[...] = wdn_f[slot].astype(BF16)

    @pl.when(used)
    def _():
        xb = x_ref[...].astype(BF16)
        bgu = bgu_ref[0]
        glu = jnp.dot(xb, wgu_s[:, :d_e], preferred_element_type=F32) + bgu[:, :d_e]
        lin = jnp.dot(xb, wgu_s[:, d_e:], preferred_element_type=F32) + bgu[:, d_e:]
        glu = jnp.minimum(glu, SWIGLU_LIMIT)
        lin = jnp.clip(lin, -SWIGLU_LIMIT, SWIGLU_LIMIT)
        act = glu * (1.0 / (1.0 + jnp.exp(-SWIGLU_ALPHA * glu))) * (lin + 1.0)
        o_ref[...] = jnp.dot(act.astype(BF16), wdn_s[...], preferred_element_type=F32) + bdn_ref[0]

    @pl.when(jnp.logical_not(used))
    def _():
        o_ref[...] = jnp.zeros_like(o_ref)


def _expert_ffn(x_buf, blk_expert, n_blocks_used, next_expert, run_parity, w_gu, b_gu, w_dn, b_dn):
    n_rows, d = x_buf.shape
    ne, _, d2 = w_gu.shape
    d_e = w_dn.shape[1]
    nb = n_rows // EXPERT_ROWS
    last_used = lambda b, be, nu, nx, pr: (jnp.minimum(b, jnp.maximum(nu[0] - 1, 0)), 0)
    by_expert = lambda b, be, nu, nx, pr: (be[b], 0, 0)
    return pl.pallas_call(
        _expert_kernel,
        grid_spec=pltpu.PrefetchScalarGridSpec(
            num_scalar_prefetch=4,
            grid=(nb,),
            in_specs=[pl.BlockSpec((EXPERT_ROWS, d), last_used),
                      pl.BlockSpec((1, 1, d2), by_expert),
                      pl.BlockSpec((1, 1, d), by_expert),
                      pl.BlockSpec(memory_space=pl.ANY),
                      pl.BlockSpec(memory_space=pl.ANY)],
            out_specs=pl.BlockSpec((EXPERT_ROWS, d), lambda b, be, nu, nx, pr: (b, 0)),
            scratch_shapes=[pltpu.VMEM((2, d, d2), w_gu.dtype), pltpu.VMEM((2, d_e, d), w_dn.dtype),
                            pltpu.VMEM((d, d2), BF16), pltpu.VMEM((d_e, d), BF16),
                            pltpu.SemaphoreType.DMA((2, 2))],
        ),
        out_shape=jax.ShapeDtypeStruct((n_rows, d), F32),
        compiler_params=_cparams("arbitrary"),
        name="moe_experts",
    )(blk_expert, n_blocks_used, next_expert, run_parity, x_buf, b_gu.reshape(ne, 1, d2), b_dn.reshape(ne, 1, d),
      w_gu, w_dn)


COMBINE_ROWS = ROUTE_ROWS


def _combine_kernel(pos_ref, pos_next_ref, g_ref, x1_ref, nw_ref, y_hbm, o_ref, buf, sem, *, apply_norm):
    tm = COMBINE_ROWS
    step = pl.program_id(0)
    slot = step % 2

    def gather(p_ref, into):
        def issue(g, c):
            base = g * ISSUE_ROWS
            for r in range(ISSUE_ROWS):
                for kk in range(TOP_K):
                    pltpu.make_async_copy(y_hbm.at[pl.ds(p_ref[0, 0, base + (kk * tm + r)], 1), :],
                                          buf.at[into, g, pl.ds(kk * ISSUE_ROWS + r, 1), :],
                                          sem.at[into]).start(priority=(r * TOP_K + kk) % DMA_PRIORITIES)
            return c
        lax.fori_loop(0, tm // ISSUE_ROWS, issue, 0)

    @pl.when(step == 0)
    def _():
        gather(pos_ref, 0)

    @pl.when(step + 1 < pl.num_programs(0))
    def _():
        gather(pos_next_ref, 1 - slot)

    pltpu.make_async_copy(buf.at[slot], buf.at[slot], sem.at[slot]).wait()

    g = g_ref[...]
    acc = x1_ref[...]
    for kk in range(TOP_K):
        rows = buf[slot, :, pl.ds(kk * ISSUE_ROWS, ISSUE_ROWS), :].reshape(tm, acc.shape[1])
        acc = acc + rows * _lane_col(g, kk)
    if apply_norm:
        var = jnp.mean(acc * acc, axis=-1, keepdims=True)
        acc = (acc * lax.rsqrt(var + NORM_EPS)) * nw_ref[...]
    o_ref[...] = acc


def _combine(y_buf, pos, gates, x1, final_norm_w, apply_norm):
    t, d = x1.shape
    tm = COMBINE_ROWS
    nb = t // tm
    row = lambda i: (i, 0)
    pos3 = pos
    return pl.pallas_call(
        functools.partial(_combine_kernel, apply_norm=apply_norm),
        grid=(nb,),
        in_specs=[pl.BlockSpec((1, 1, TOP_K * tm), lambda i: (i, 0, 0), memory_space=pltpu.SMEM),
                  pl.BlockSpec((1, 1, TOP_K * tm), lambda i: (jnp.minimum(i + 1, nb - 1), 0, 0),
                               memory_space=pltpu.SMEM),
                  pl.BlockSpec((tm, LANES), row), pl.BlockSpec((tm, d), row),
                  pl.BlockSpec((1, d), lambda i: (0, 0)), pl.BlockSpec(memory_space=pl.ANY)],
        out_specs=pl.BlockSpec((tm, d), row),
        out_shape=jax.ShapeDtypeStruct((t, d), F32),
        scratch_shapes=[pltpu.VMEM((2, tm // ISSUE_ROWS, TOP_K * ISSUE_ROWS, d), F32),
                        pltpu.SemaphoreType.DMA((2,))],
        compiler_params=_cparams("arbitrary"),
        name="moe_combine",
    )(pos3, pos3, gates, x1, final_norm_w.reshape(1, d), y_buf)


def _routing_tables(top_idx, rank, counts, n_tokens):
    padded = (counts + EXPERT_ROWS - 1) // EXPERT_ROWS * EXPERT_ROWS
    pad_end = jnp.cumsum(padded)
    pad_start = pad_end - padded
    onehot = (top_idx[:, :, None] == jnp.arange(N_EXPERTS, dtype=jnp.int32)[None, None, :]).astype(F32)
    start_of = jnp.einsum("kte,e->kt", onehot, pad_start.astype(F32), precision=lax.Precision.HIGHEST)
    dest = (start_of.astype(jnp.int32) + rank).astype(jnp.int32)
    n_tiles = n_tokens // ROUTE_ROWS
    dest = dest.reshape(TOP_K, n_tiles, ROUTE_ROWS).transpose(1, 0, 2).reshape(n_tiles, 1, TOP_K * ROUTE_ROWS)
    n_rows = n_tokens * TOP_K + N_EXPERTS * EXPERT_ROWS
    n_blocks = n_rows // EXPERT_ROWS
    blk_start = jnp.arange(n_blocks, dtype=jnp.int32) * EXPERT_ROWS
    blk_expert = jnp.minimum(jnp.sum((pad_end[None, :] <= blk_start[:, None]).astype(jnp.int32), axis=1),
                             N_EXPERTS - 1).astype(jnp.int32)
    n_blocks_used = (pad_end[-1:] // EXPERT_ROWS).astype(jnp.int32)
    fill = jnp.stack([pad_start + counts, padded - counts]).astype(jnp.int32)
    experts = jnp.arange(N_EXPERTS, dtype=jnp.int32)[None, :]
    has_rows = (counts > 0)[None, :]
    earlier = (experts < blk_expert[:, None]) & has_rows
    run_parity = (jnp.sum(earlier.astype(jnp.int32), axis=1) % 2).astype(jnp.int32)
    later = jnp.min(jnp.where((experts > blk_expert[:, None]) & has_rows, experts, N_EXPERTS), axis=1)
    next_expert = jnp.where(later < N_EXPERTS, later, -1).astype(jnp.int32)
    return dest, fill, blk_expert, n_blocks_used, next_expert, run_parity, n_rows


def kernel(x, mix_norm_w, w_in, conv_w, conv_b, igate_b, fgate_b, mlstm_norm_w, w_out, ffn_norm_w,
           router_w, router_b, w_gate_up, b_gate_up, w_down, b_down, final_norm_w):
    bsz, seq, d = x.shape
    depth = w_in.shape[0]
    t = bsz * seq
    k_sel = min(TOPK_MAX, seq // 4)
    head_order = [h for s in range(Q_PER_KV) for h in (s, Q_PER_KV + s)]
    x2 = x.reshape(t, d)
    for layer in range(depth):
        p = _in_projection(x2, mix_norm_w[layer], w_in[layer], igate_b[layer], fgate_b[layer], seq)
        r3 = lambda a: a.reshape(bsz, seq, a.shape[-1])
        att = _dsa_attention(r3(p["q"]), r3(p["iq"]), r3(p["iw"]), r3(p["k"]), r3(p["v"]), r3(p["ik"]), k_sel)
        mem = _mlstm_mixer(r3(p["mq"]), r3(p["mk"]), r3(p["mv"]), r3(p["mo"]), r3(p["g"]),
                           conv_w[layer], conv_b[layer], mlstm_norm_w[layer])
        wo = w_out[layer]
        w_att = wo[:ATT_WIDTH].reshape(ATT_HEADS, HEAD_DIM, d)[jnp.asarray(head_order)].reshape(ATT_WIDTH, d)
        x1, hn, top_i, top_g, counts = _outproj_router(
            att.reshape(t, ATT_WIDTH), mem.reshape(t, MLSTM_WIDTH), x2, w_att.astype(BF16),
            wo[ATT_WIDTH:].astype(BF16), ffn_norm_w[layer], router_w[layer], router_b[layer])
        dest, fill, blk_expert, n_blocks_used, next_expert, run_parity, n_rows = _routing_tables(
            top_i[:TOP_K], top_i[TOP_K:], counts[0, :N_EXPERTS].astype(jnp.int32), t)
        x_buf = _dispatch_rows(hn, dest, fill, n_blocks_used, n_rows)
        y_buf = _expert_ffn(x_buf, blk_expert, n_blocks_used, next_expert, run_parity, w_gate_up[layer],
                            b_gate_up[layer], w_down[layer], b_down[layer])
        x2 = _combine(y_buf, dest, top_g, x1, final_norm_w, apply_norm=(layer == depth - 1))
    return x2.reshape(bsz, seq, d)
```
